```python
import jax, jax.numpy as jnp
from jax import lax
import numpy as np

D_MODEL = 1024
BATCH = 1
SEQ = 16384
DEPTH = 1
DEC_BATCH = 128
DEC_SEQ = 8
PAST_LEN = 8192
PAGE_SIZE = 128

HEAD_DIM = 64
H_RET = 8
H_FOX = 8
D_RET = H_RET * HEAD_DIM
D_FOX = H_FOX * HEAD_DIM
D_MIX = D_RET + D_FOX
D_FF = 2816
D_PLE = 256
RET_CHUNK = 128
FOX_BLOCK = 128
ROPE_BASE = 10000.0
NORM_EPS = 1e-6
GN_EPS = 1e-5
NEG_INF = -1e30
SPLITS = [D_RET, 2 * D_RET, 3 * D_RET, 4 * D_RET, 4 * D_RET + D_FOX, 4 * D_RET + 2 * D_FOX, 4 * D_RET + 3 * D_FOX]
N_IN_COLS = 4 * D_RET + 3 * D_FOX + H_FOX

kernel_name = "hymba_retnet_fox_macaron_step"


def rmsnorm(x, g):
    xf = x.astype(jnp.float32)
    r = lax.rsqrt(jnp.mean(xf * xf, axis=-1, keepdims=True) + NORM_EPS)
    return (xf * r * g.astype(jnp.float32)).astype(x.dtype)


def swiglu(x, w1, w3, w2):
    return (jax.nn.silu(x @ w1) * (x @ w3)) @ w2


def rope(x, pos):
    half = HEAD_DIM // 2
    inv = ROPE_BASE ** (-jnp.arange(half, dtype=jnp.float32) / half)
    ang = pos[:, None] * inv[None, :]
    cos = jnp.cos(ang)[None, :, None, :]
    sin = jnp.sin(ang)[None, :, None, :]
    x1, x2 = x[..., :half], x[..., half:]
    return jnp.concatenate([x1 * cos - x2 * sin, x2 * cos + x1 * sin], axis=-1)


def ret_log_decay():
    return jnp.log1p(-(2.0 ** (-5.0 - jnp.arange(H_RET, dtype=jnp.float32))))


def retention_chunkwise(q, k, v, s0):
    B, T, H, dk = q.shape
    dv = v.shape[-1]
    C = min(RET_CHUNK, T)
    n = T // C
    lg = ret_log_decay()
    i = jnp.arange(C, dtype=jnp.float32)
    diff = i[:, None] - i[None, :]
    intra = jnp.where(diff[None] >= 0, jnp.exp(jnp.maximum(diff, 0.0)[None] * lg[:, None, None]), 0.0)
    q_dec = jnp.exp((i[:, None] + 1.0) * lg[None, :])
    k_dec = jnp.exp((C - 1.0 - i)[:, None] * lg[None, :])
    c_dec = jnp.exp(C * lg)

    def to_chunks(a):
        return jnp.moveaxis(a.reshape(B, n, C, H, a.shape[-1]), 1, 0)

    def step(S, xs):
        qc, kc, vc = xs
        att = jnp.einsum('bihd,bjhd->bhij', qc, kc) * intra[None]
        o = jnp.einsum('bhij,bjhe->bihe', att, vc)
        o = o + jnp.einsum('bihd,bhde->bihe', qc, S) * q_dec[None, :, :, None]
        S = S * c_dec[None, :, None, None] + jnp.einsum('bjhd,bjhe->bhde', kc * k_dec[None, :, :, None], vc)
        return S, o

    S, o = lax.scan(step, s0, (to_chunks(q), to_chunks(k), to_chunks(v)))
    return jnp.moveaxis(o, 0, 1).reshape(B, T, H, dv), S


def forgetting_attention(q, c_q, k_past, v_past, c_past, k_new, v_new):
    B, T, H, d = q.shape
    blk = min(FOX_BLOCK, T)
    nb = T // blk
    scale = d ** -0.5
    ck_past = jnp.moveaxis(c_past, 1, 2)[:, :, None, :]
    ck_new = jnp.moveaxis(c_q, 1, 2)[:, :, None, :]
    P = k_past.shape[1]
    kpos = jnp.arange(T)

    def one_block(xs):
        qb, cqb, b = xs
        qpos = b * blk + jnp.arange(blk)
        cq = jnp.moveaxis(cqb, 1, 2)[..., None]
        s_p = jnp.einsum('bqhd,bkhd->bhqk', qb, k_past, preferred_element_type=jnp.float32) * scale + cq - ck_past
        s_n = jnp.einsum('bqhd,bkhd->bhqk', qb, k_new, preferred_element_type=jnp.float32) * scale + cq - ck_new
        s_n = jnp.where(kpos[None, None, None, :] <= qpos[None, None, :, None], s_n, NEG_INF)
        pr = jax.nn.softmax(jnp.concatenate([s_p, s_n], axis=-1), axis=-1)
        o = jnp.einsum('bhqk,bkhd->bqhd', pr[..., :P], v_past, preferred_element_type=jnp.float32)
        return o + jnp.einsum('bhqk,bkhd->bqhd', pr[..., P:], v_new, preferred_element_type=jnp.float32)

    qb = jnp.moveaxis(q.reshape(B, nb, blk, H, d), 1, 0)
    cqb = jnp.moveaxis(c_q.reshape(B, nb, blk, H), 1, 0)
    o = lax.map(one_block, (qb, cqb, jnp.arange(nb)))
    return jnp.moveaxis(o, 0, 1).reshape(B, T, H, d)


def hybrid_layer(x, p, pos0, s_ret0, k_past, v_past, logf_past,
                 g_ffn1, w1_ffn1, w3_ffn1, w2_ffn1, g_mix, w_in, b_forget, gn_w, gn_b, w_out,
                 g_ffn2, w1_ffn2, w3_ffn2, w2_ffn2, g_ple, w_ple_gate, b_ple_gate, w_ple_proj):
    B, T, _ = x.shape
    h = x + 0.5 * swiglu(rmsnorm(x, g_ffn1), w1_ffn1, w3_ffn1, w2_ffn1)
    u = rmsnorm(h, g_mix)
    proj = u @ w_in
    rq, rk, rv, rg, fq, fk, fv, ff = jnp.split(proj, SPLITS, axis=-1)
    pos = (pos0 + jnp.arange(T)).astype(jnp.float32)
    rq = rope(rq.reshape(B, T, H_RET, HEAD_DIM).astype(jnp.float32), pos)
    rk = rope(rk.reshape(B, T, H_RET, HEAD_DIM).astype(jnp.float32), pos) * HEAD_DIM ** -0.5
    rv = rv.reshape(B, T, H_RET, HEAD_DIM).astype(jnp.float32)
    o_ret, s_ret = retention_chunkwise(rq, rk, rv, s_ret0.astype(jnp.float32))
    mu = jnp.mean(o_ret, axis=-1, keepdims=True)
    var = jnp.mean(jnp.square(o_ret - mu), axis=-1, keepdims=True)
    o_ret = ((o_ret - mu) * lax.rsqrt(var + GN_EPS)).reshape(B, T, D_RET)
    o_ret = (o_ret * gn_w.astype(jnp.float32) + gn_b.astype(jnp.float32)) * jax.nn.silu(rg.astype(jnp.float32))
    fq = fq.reshape(B, T, H_FOX, HEAD_DIM)
    fk = fk.reshape(B, T, H_FOX, HEAD_DIM)
    fv = fv.reshape(B, T, H_FOX, HEAD_DIM)
    logf = jax.nn.log_sigmoid(ff.astype(jnp.float32) + b_forget.astype(jnp.float32))
    logf_past = logf_past.astype(jnp.float32)
    c_past = jnp.cumsum(logf_past, axis=1)
    if k_past.shape[1] > 0:
        c_new = c_past[:, -1:] + jnp.cumsum(logf, axis=1)
    else:
        c_new = jnp.cumsum(logf, axis=1)
    o_fox = forgetting_attention(fq, c_new, k_past, v_past, c_past, fk, fv)
    mix = jnp.concatenate([o_ret, o_fox.reshape(B, T, D_FOX)], axis=-1).astype(x.dtype) @ w_out
    h = h + mix
    h = h + 0.5 * swiglu(rmsnorm(h, g_ffn2), w1_ffn2, w3_ffn2, w2_ffn2)
    gate = jax.nn.sigmoid(rmsnorm(h, g_ple) @ w_ple_gate + b_ple_gate)
    h = h + gate * (p @ w_ple_proj)
    return h, fk, fv, logf, s_ret


def setup_inputs(seed: int = 0) -> dict:
    key = jax.random.key(seed)
    ks = jax.random.split(key, 40)
    n_pages = PAST_LEN // PAGE_SIZE
    n_used = DEC_BATCH * n_pages
    n_pool = n_used + max(1, n_used // 4)

    def nrm(k, shape, s):
        return jax.random.normal(k, shape, jnp.float32) * s

    def gain(k, n):
        return 1.0 + nrm(k, (DEPTH, n), 0.02)

    page_table = jax.random.permutation(ks[0], n_pool)[:n_used].reshape(DEC_BATCH, n_pages).astype(jnp.int32)
    return {
        'x_prompt': nrm(ks[1], (BATCH, SEQ, D_MODEL), 1.0),
        'x_sample': nrm(ks[2], (DEC_BATCH, DEC_SEQ, D_MODEL), 1.0),
        'cache_fox_k': nrm(ks[3], (DEPTH, n_pool, PAGE_SIZE, H_FOX, HEAD_DIM), 1.0),
        'cache_fox_v': nrm(ks[4], (DEPTH, n_pool, PAGE_SIZE, H_FOX, HEAD_DIM), 1.0),
        'cache_fox_logf': jax.nn.log_sigmoid(3.0 + nrm(ks[5], (DEPTH, n_pool, PAGE_SIZE, H_FOX), 1.0)),
        'state_ret': nrm(ks[6], (DEPTH, DEC_BATCH, H_RET, HEAD_DIM, HEAD_DIM), 1.0),
        'page_table': page_table,
        'p_prompt': nrm(ks[7], (DEPTH, BATCH, SEQ, D_PLE), 1.0),
        'p_sample': nrm(ks[8], (DEPTH, DEC_BATCH, DEC_SEQ, D_PLE), 1.0),
        'g_ffn1': gain(ks[9], D_MODEL),
        'w1_ffn1': nrm(ks[10], (DEPTH, D_MODEL, D_FF), D_MODEL ** -0.5),
        'w3_ffn1': nrm(ks[11], (DEPTH, D_MODEL, D_FF), D_MODEL ** -0.5),
        'w2_ffn1': nrm(ks[12], (DEPTH, D_FF, D_MODEL), D_FF ** -0.5),
        'g_mix': gain(ks[13], D_MODEL),
        'w_in': nrm(ks[14], (DEPTH, D_MODEL, N_IN_COLS), D_MODEL ** -0.5),
        'b_forget': jnp.linspace(1.0, 5.0, H_FOX, dtype=jnp.float32)[None, :] + nrm(ks[15], (DEPTH, H_FOX), 0.1),
        'gn_w': gain(ks[16], D_RET),
        'gn_b': nrm(ks[17], (DEPTH, D_RET), 0.02),
        'w_out': nrm(ks[18], (DEPTH, D_MIX, D_MODEL), D_MIX ** -0.5),
        'g_ffn2': gain(ks[19], D_MODEL),
        'w1_ffn2': nrm(ks[20], (DEPTH, D_MODEL, D_FF), D_MODEL ** -0.5),
        'w3_ffn2': nrm(ks[21], (DEPTH, D_MODEL, D_FF), D_MODEL ** -0.5),
        'w2_ffn2': nrm(ks[22], (DEPTH, D_FF, D_MODEL), D_FF ** -0.5),
        'g_ple': gain(ks[23], D_MODEL),
        'w_ple_gate': nrm(ks[24], (DEPTH, D_MODEL, D_MODEL), D_MODEL ** -0.5),
        'b_ple_gate': nrm(ks[25], (DEPTH, D_MODEL), 0.02),
        'w_ple_proj': nrm(ks[26], (DEPTH, D_PLE, D_MODEL), D_PLE ** -0.5),
        'g_final': 1.0 + nrm(ks[27], (D_MODEL,), 0.02),
    }


def reference(x_prompt, x_sample, cache_fox_k, cache_fox_v, cache_fox_logf, state_ret, page_table,
              p_prompt, p_sample, g_ffn1, w1_ffn1, w3_ffn1, w2_ffn1, g_mix, w_in, b_forget, gn_w, gn_b,
              w_out, g_ffn2, w1_ffn2, w3_ffn2, w2_ffn2, g_ple, w_ple_gate, b_ple_gate, w_ple_proj, g_final):
    n_pages = PAST_LEN // PAGE_SIZE
    hp, hs = x_prompt, x_sample
    kp_l, vp_l, lp_l, sp_l, ks_l, vs_l, ls_l, ss_l = [], [], [], [], [], [], [], []
    for i in range(DEPTH):
        lw = (g_ffn1[i], w1_ffn1[i], w3_ffn1[i], w2_ffn1[i], g_mix[i], w_in[i], b_forget[i], gn_w[i], gn_b[i],
              w_out[i], g_ffn2[i], w1_ffn2[i], w3_ffn2[i], w2_ffn2[i], g_ple[i], w_ple_gate[i], b_ple_gate[i],
              w_ple_proj[i])
        s0 = jnp.zeros((BATCH, H_RET, HEAD_DIM, HEAD_DIM), jnp.float32)
        k0 = jnp.zeros((BATCH, 0, H_FOX, HEAD_DIM), cache_fox_k.dtype)
        v0 = jnp.zeros((BATCH, 0, H_FOX, HEAD_DIM), cache_fox_v.dtype)
        l0 = jnp.zeros((BATCH, 0, H_FOX), cache_fox_logf.dtype)
        hp, kp, vp, lp, sp = hybrid_layer(hp, p_prompt[i], 0, s0, k0, v0, l0, *lw)
        past_k = cache_fox_k[i, page_table].reshape(DEC_BATCH, n_pages * PAGE_SIZE, H_FOX, HEAD_DIM)
        past_v = cache_fox_v[i, page_table].reshape(DEC_BATCH, n_pages * PAGE_SIZE, H_FOX, HEAD_DIM)
        past_l = cache_fox_logf[i, page_table].reshape(DEC_BATCH, n_pages * PAGE_SIZE, H_FOX)
        hs, ks_, vs_, ls_, ss_ = hybrid_layer(hs, p_sample[i], PAST_LEN, state_ret[i], past_k, past_v, past_l, *lw)
        kp_l.append(kp); vp_l.append(vp); lp_l.append(lp); sp_l.append(sp)
        ks_l.append(ks_); vs_l.append(vs_); ls_l.append(ls_); ss_l.append(ss_)
    y_prompt = rmsnorm(hp, g_final)
    y_sample = rmsnorm(hs, g_final)
    k_prompt = jnp.stack(kp_l); v_prompt = jnp.stack(vp_l); logf_prompt = jnp.stack(lp_l); ret_prompt = jnp.stack(sp_l)
    k_sample = jnp.stack(ks_l); v_sample = jnp.stack(vs_l); logf_sample = jnp.stack(ls_l); ret_sample = jnp.stack(ss_l)
    return (y_prompt, y_sample, k_prompt, v_prompt, logf_prompt, ret_prompt, k_sample, v_sample, logf_sample, ret_sample)
```

```python
import functools

import numpy as np
import jax
import jax.numpy as jnp
from jax import lax
from jax.experimental import pallas as pl
from jax.experimental.pallas import tpu as pltpu

F32 = jnp.float32
BF16 = jnp.bfloat16

HEAD_DIM = 64
N_HEADS = 8
D_GROUP = N_HEADS * HEAD_DIM
RET_CHUNK = 128
ROPE_BASE = 10000.0
NORM_EPS = 1e-6
GN_EPS = 1e-5
NEG_INF = -1e30
LANES = 128
VMEM_LIMIT = 52 * 1024 * 1024

NT_DIMS = (((1,), (1,)), ((), ()))
TN_DIMS = (((0,), (0,)), ((), ()))


def _pick_tile(n, pref, mult):
    best = None
    for t in range(mult, min(n, pref) + 1, mult):
        if n % t == 0:
            best = t
    return best if best is not None else n


def _params(*sem):
    return pltpu.CompilerParams(dimension_semantics=sem, vmem_limit_bytes=VMEM_LIMIT)


def _rms(x, g):
    r = lax.rsqrt(jnp.mean(x * x, axis=-1, keepdims=True) + NORM_EPS)
    return x * r * g


def _dot(a, b):
    return jnp.dot(a, b, preferred_element_type=F32)


def _ffn_kernel(x_ref, g_ref, w1_ref, w3_ref, w2_ref, o_ref, xn_ref, acc_ref):
    j = pl.program_id(1)

    @pl.when(j == 0)
    def _():
        xn_ref[...] = _rms(x_ref[...], g_ref[...]).astype(BF16)
        acc_ref[...] = jnp.zeros_like(acc_ref)

    xn = xn_ref[...]
    h1 = _dot(xn, w1_ref[...])
    h3 = _dot(xn, w3_ref[...])
    a = (h1 * jax.nn.sigmoid(h1) * h3).astype(BF16)
    acc_ref[...] += _dot(a, w2_ref[...])

    @pl.when(j == pl.num_programs(1) - 1)
    def _():
        o_ref[...] = x_ref[...] + 0.5 * acc_ref[...]


def _ffn(x, g, w1, w3, w2):
    t, d = x.shape
    d_ff = w1.shape[1]
    tm = _pick_tile(t, 512, 8)
    tf = _pick_tile(d_ff, 1408, LANES)
    return pl.pallas_call(
        _ffn_kernel,
        grid=(t // tm, d_ff // tf),
        in_specs=[
            pl.BlockSpec((tm, d), lambda i, j: (i, 0)),
            pl.BlockSpec((1, d), lambda i, j: (0, 0)),
            pl.BlockSpec((d, tf), lambda i, j: (0, j)),
            pl.BlockSpec((d, tf), lambda i, j: (0, j)),
            pl.BlockSpec((tf, d), lambda i, j: (j, 0)),
        ],
        out_specs=pl.BlockSpec((tm, d), lambda i, j: (i, 0)),
        out_shape=jax.ShapeDtypeStruct((t, d), F32),
        scratch_shapes=[pltpu.VMEM((tm, d), BF16), pltpu.VMEM((tm, d), F32)],
        compiler_params=_params("parallel", "arbitrary"),
        name="ffn",
    )(x, g, w1, w3, w2)


def _inproj_kernel(head_major, h_ref, g_ref, w_ref, wff_ref, bf_ref, cos_ref, sin_ref,
                   rq_ref, rk_ref, rv_ref, rg_ref, fk_ref, fv_ref, lf_ref, *q_refs):
    tm = h_ref.shape[0]
    u = _rms(h_ref[...], g_ref[...]).astype(BF16)

    def piece(p):
        return _dot(u, w_ref[:, p * D_GROUP:(p + 1) * D_GROUP])

    cos = cos_ref[...]
    sin = sin_ref[...]
    lane = lax.broadcasted_iota(jnp.int32, (tm, LANES), 1)
    first_half = (lane % HEAD_DIM) < (HEAD_DIM // 2)

    def rope(x):
        outs = []
        for cb in range(D_GROUP // LANES):
            xb = x[:, cb * LANES:(cb + 1) * LANES]
            partner = jnp.where(first_half,
                                pltpu.roll(xb, LANES - HEAD_DIM // 2, 1),
                                pltpu.roll(xb, HEAD_DIM // 2, 1))
            outs.append(xb * cos + partner * sin)
        return jnp.concatenate(outs, axis=1)

    scale = HEAD_DIM ** -0.5
    rq_ref[...] = rope(piece(0)).astype(rq_ref.dtype)
    rk_ref[...] = (rope(piece(1)) * scale).astype(rk_ref.dtype)
    rv_ref[...] = piece(2).astype(rv_ref.dtype)
    rg_ref[...] = piece(3)
    fq = piece(4) * scale
    fk = piece(5)
    fv = piece(6)
    fk_ref[...] = fk
    fv_ref[...] = fv
    if head_major:
        qh_ref, kh_ref, vh_ref = q_refs
        for h in range(N_HEADS):
            sl = slice(h * HEAD_DIM, (h + 1) * HEAD_DIM)
            qh_ref[h] = fq[:, sl].astype(BF16)
            kh_ref[h] = fk[:, sl].astype(BF16)
            vh_ref[h] = fv[:, sl].astype(BF16)
    else:
        (fq_ref,) = q_refs
        fq_ref[...] = fq
    ff = _dot(u, wff_ref[...]) + bf_ref[...]
    lf_ref[...] = jnp.minimum(ff, 0.0) - jnp.log1p(jnp.exp(-jnp.abs(ff)))


def _inproj(h, g, w_main, w_ff, b_ff, cos_t, sin_t, head_major):
    t, d = h.shape
    tm = _pick_tile(t, 512, 8)
    act = BF16 if head_major else F32
    row = lambda i: (i, 0)
    fixed = lambda i: (0, 0)
    wide = pl.BlockSpec((tm, D_GROUP), row)
    out_shape = [jax.ShapeDtypeStruct((t, D_GROUP), act)] * 3 + [
        jax.ShapeDtypeStruct((t, D_GROUP), F32)] * 3 + [jax.ShapeDtypeStruct((t, LANES), F32)]
    out_specs = [wide] * 6 + [pl.BlockSpec((tm, LANES), row)]
    if head_major:
        out_shape += [jax.ShapeDtypeStruct((N_HEADS, t, HEAD_DIM), BF16)] * 3
        out_specs += [pl.BlockSpec((N_HEADS, tm, HEAD_DIM), lambda i: (0, i, 0))] * 3
    else:
        out_shape += [jax.ShapeDtypeStruct((t, D_GROUP), F32)]
        out_specs += [wide]
    return pl.pallas_call(
        functools.partial(_inproj_kernel, head_major),
        grid=(t // tm,),
        in_specs=[
            pl.BlockSpec((tm, d), row),
            pl.BlockSpec((1, d), fixed),
            pl.BlockSpec(w_main.shape, fixed),
            pl.BlockSpec(w_ff.shape, fixed),
            pl.BlockSpec((1, LANES), fixed),
            pl.BlockSpec((tm, LANES), row),
            pl.BlockSpec((tm, LANES), row),
        ],
        out_specs=out_specs,
        out_shape=out_shape,
        compiler_params=_params("parallel"),
        name="inproj",
    )(h, g, w_main, w_ff, b_ff, cos_t, sin_t)


def _cumsum_kernel(seg, x_ref, c_ref, ct_ref, carry_ref):
    tb = x_ref.shape[0]
    row = lax.broadcasted_iota(jnp.int32, (tb, tb), 0)
    col = lax.broadcasted_iota(jnp.int32, (tb, tb), 1)
    keep = col <= row
    carried = seg > tb
    if not carried:
        keep = keep & ((row // seg) == (col // seg))
    tri = jnp.where(keep, 1.0, 0.0).astype(F32)
    c = jnp.dot(tri, x_ref[...], preferred_element_type=F32, precision=lax.Precision.HIGHEST)
    if carried:
        @pl.when(pl.program_id(0) == 0)
        def _():
            carry_ref[...] = jnp.zeros_like(carry_ref)

        c = c + carry_ref[...]
        carry_ref[...] = c[tb - 1:tb, :]
    c_ref[...] = c
    ct_ref[...] = c.T


def _cumsum(x, seg):
    t = x.shape[0]
    tb = _pick_tile(t, 512, LANES)
    assert seg % tb == 0 or tb % seg == 0
    return pl.pallas_call(
        functools.partial(_cumsum_kernel, seg),
        grid=(t // tb,),
        in_specs=[pl.BlockSpec((tb, LANES), lambda i: (i, 0))],
        out_specs=[pl.BlockSpec((tb, LANES), lambda i: (i, 0)),
                   pl.BlockSpec((LANES, tb), lambda i: (0, i))],
        out_shape=[jax.ShapeDtypeStruct((t, LANES), F32), jax.ShapeDtypeStruct((LANES, t), F32)],
        scratch_shapes=[pltpu.VMEM((1, LANES), F32)],
        compiler_params=_params("arbitrary"),
        name="forget_cumsum",
    )(x)


def _ret_kernel(q_ref, k_ref, v_ref, g_ref, s0_ref, intra_ref, qdec_ref, kdec_ref, cdec_ref,
                gnw_ref, gnb_ref, o_ref, sout_ref, s_ref):
    c = pl.program_id(1)

    @pl.when(c == 0)
    def _():
        s_ref[...] = s0_ref[0]

    mm = BF16 if q_ref.shape[0] >= 16 else F32
    q = q_ref[...].astype(F32)
    k = k_ref[...].astype(F32)
    qb = q.astype(mm)
    kb = k.astype(mm)
    vb = v_ref[...].astype(mm)
    qd = (q * qdec_ref[...]).astype(mm)
    kd = (k * kdec_ref[...]).astype(mm)
    outs = []
    for h in range(N_HEADS):
        sl = slice(h * HEAD_DIM, (h + 1) * HEAD_DIM)
        att = lax.dot_general(qb[:, sl], kb[:, sl], NT_DIMS, preferred_element_type=F32) * intra_ref[h]
        s_old = s_ref[h]
        o = _dot(att.astype(mm), vb[:, sl]) + _dot(qd[:, sl], s_old.astype(mm))
        s_ref[h] = s_old * cdec_ref[h] + lax.dot_general(kd[:, sl], vb[:, sl], TN_DIMS,
                                                         preferred_element_type=F32)
        mu = jnp.mean(o, axis=-1, keepdims=True)
        dev = o - mu
        var = jnp.mean(dev * dev, axis=-1, keepdims=True)
        outs.append(dev * lax.rsqrt(var + GN_EPS))
    on = jnp.concatenate(outs, axis=1)
    gate = g_ref[...]
    o_ref[...] = ((on * gnw_ref[...] + gnb_ref[...]) * (gate * jax.nn.sigmoid(gate))).astype(o_ref.dtype)

    @pl.when(c == pl.num_programs(1) - 1)
    def _():
        sout_ref[0] = s_ref[...]


def _ret_tables(chunk):
    lg = jnp.log1p(-(2.0 ** (-5.0 - jnp.arange(N_HEADS, dtype=F32))))
    i = jnp.arange(chunk, dtype=F32)
    diff = i[:, None] - i[None, :]
    intra = jnp.where(diff[None] >= 0, jnp.exp(jnp.maximum(diff, 0.0)[None] * lg[:, None, None]), 0.0)
    q_dec = jnp.exp((i[:, None] + 1.0) * lg[None, :])
    k_dec = jnp.exp((chunk - 1.0 - i)[:, None] * lg[None, :])
    c_dec = jnp.exp(chunk * lg)
    expand = lambda a: jnp.repeat(a, HEAD_DIM, axis=1)
    return intra, expand(q_dec), expand(k_dec), jnp.broadcast_to(c_dec[:, None, None], (N_HEADS, 1, HEAD_DIM))


def _retention(q, k, v, g, s0, gn_w, gn_b, seq, out_dtype):
    t = q.shape[0]
    batch = t // seq
    chunk = min(RET_CHUNK, seq)
    n = seq // chunk
    intra, q_dec, k_dec, c_dec = _ret_tables(chunk)
    blk = pl.BlockSpec((chunk, D_GROUP), lambda b, c: (b * n + c, 0))
    fixed2 = lambda b, c: (0, 0)
    fixed3 = lambda b, c: (0, 0, 0)
    state = pl.BlockSpec((1, N_HEADS, HEAD_DIM, HEAD_DIM), lambda b, c: (b, 0, 0, 0))
    return pl.pallas_call(
        _ret_kernel,
        grid=(batch, n),
        in_specs=[blk, blk, blk, blk, state,
                  pl.BlockSpec(intra.shape, fixed3),
                  pl.BlockSpec(q_dec.shape, fixed2),
                  pl.BlockSpec(k_dec.shape, fixed2),
                  pl.BlockSpec(c_dec.shape, fixed3),
                  pl.BlockSpec((1, D_GROUP), fixed2),
                  pl.BlockSpec((1, D_GROUP), fixed2)],
        out_specs=[blk, state],
        out_shape=[jax.ShapeDtypeStruct((t, D_GROUP), out_dtype),
                   jax.ShapeDtypeStruct((batch, N_HEADS, HEAD_DIM, HEAD_DIM), F32)],
        scratch_shapes=[pltpu.VMEM((N_HEADS, HEAD_DIM, HEAD_DIM), F32)],
        compiler_params=_params("parallel", "arbitrary"),
        name="retention",
    )(q, k, v, g, s0, intra, q_dec, k_dec, c_dec, gn_w, gn_b)


def _fox_prompt_kernel(qi_ref, kj_ref, q_ref, k_ref, v_ref, cq_ref, ck_ref, o_ref, m_ref, l_ref, acc_ref):
    step = pl.program_id(0)
    i = qi_ref[step]
    j = kj_ref[step]
    tq = q_ref.shape[1]
    tk = k_ref.shape[1]

    @pl.when(j == 0)
    def _():
        m_ref[...] = jnp.full_like(m_ref, NEG_INF)
        l_ref[...] = jnp.zeros_like(l_ref)
        acc_ref[...] = jnp.zeros_like(acc_ref)

    def update(masked):
        cq = cq_ref[...]
        ck = ck_ref[...]
        if masked:
            causal = (lax.broadcasted_iota(jnp.int32, (tq, tk), 1)
                      <= lax.broadcasted_iota(jnp.int32, (tq, tk), 0))
        for h in range(N_HEADS):
            s = lax.dot_general(q_ref[h], k_ref[h], NT_DIMS, preferred_element_type=F32)
            s = s + cq[:, h:h + 1] - ck[h:h + 1, :]
            if masked:
                s = jnp.where(causal, s, NEG_INF)
            m_prev = m_ref[h]
            m_new = jnp.maximum(m_prev, jnp.max(s, axis=1, keepdims=True))
            alpha = jnp.exp(m_prev - m_new)
            p = jnp.exp(s - m_new)
            l_ref[h] = alpha * l_ref[h] + jnp.sum(p, axis=1, keepdims=True)
            acc_ref[h] = alpha * acc_ref[h] + _dot(p.astype(BF16), v_ref[h])
            m_ref[h] = m_new

    @pl.when(j < i)
    def _():
        update(False)

    @pl.when(j == i)
    def _():
        update(True)
        for h in range(N_HEADS):
            o_ref[:, h * HEAD_DIM:(h + 1) * HEAD_DIM] = (acc_ref[h] / l_ref[h]).astype(o_ref.dtype)


def _fox_prompt(qh, kh, vh, c, ct):
    t = qh.shape[1]
    tq = _pick_tile(t, 512, LANES)
    nq = t // tq
    qi = np.array([i for i in range(nq) for _ in range(i + 1)], np.int32)
    kj = np.array([j for i in range(nq) for j in range(i + 1)], np.int32)
    head_blk = lambda which: pl.BlockSpec(
        (N_HEADS, tq, HEAD_DIM), (lambda s, qi, kj: (0, qi[s], 0)) if which == "q" else (lambda s, qi, kj: (0, kj[s], 0)))
    grid_spec = pltpu.PrefetchScalarGridSpec(
        num_scalar_prefetch=2,
        grid=(len(qi),),
        in_specs=[head_blk("q"), head_blk("k"), head_blk("k"),
                  pl.BlockSpec((tq, LANES), lambda s, qi, kj: (qi[s], 0)),
                  pl.BlockSpec((N_HEADS, tq), lambda s, qi, kj: (0, kj[s]))],
        out_specs=pl.BlockSpec((tq, D_GROUP), lambda s, qi, kj: (qi[s], 0)),
        scratch_shapes=[pltpu.VMEM((N_HEADS, tq, 1), F32), pltpu.VMEM((N_HEADS, tq, 1), F32),
                        pltpu.VMEM((N_HEADS, tq, HEAD_DIM), F32)],
    )
    return pl.pallas_call(
        _fox_prompt_kernel,
        grid_spec=grid_spec,
        out_shape=jax.ShapeDtypeStruct((t, D_GROUP), BF16),
        compiler_params=_params("arbitrary"),
        name="fox_prompt",
    )(jnp.asarray(qi), jnp.asarray(kj), qh, kh, vh, c, ct)


def _fox_decode_kernel(n_group, pt_ref, q_ref, kn_ref, vn_ref, cn_ref, *refs):
    lt_refs = refs[:n_group]
    k_refs = refs[n_group:2 * n_group]
    v_refs = refs[2 * n_group:3 * n_group]
    o_ref, qbd_ref, m_ref, l_ref, acc_ref, carry_ref = refs[3 * n_group:]
    g = pl.program_id(1)
    n_q = q_ref.shape[0]
    rows = n_q * N_HEADS
    page = k_refs[0].shape[1]
    cn = cn_ref[0]

    def attend(s, v):
        m_prev = m_ref[...]
        m_new = jnp.maximum(m_prev, jnp.max(s, axis=1, keepdims=True))
        alpha = jnp.exp(m_prev - m_new)
        p = jnp.exp(s - m_new)
        l_ref[...] = alpha * l_ref[...] + jnp.sum(p, axis=1, keepdims=True)
        acc_ref[...] = alpha * acc_ref[...] + _dot(p.astype(BF16), v)
        m_ref[...] = m_new

    @pl.when(g == 0)
    def _():
        q = q_ref[...]
        rep = jnp.concatenate([jnp.broadcast_to(q[t:t + 1, :], (N_HEADS, D_GROUP)) for t in range(n_q)], axis=0)
        r_head = lax.broadcasted_iota(jnp.int32, (rows, D_GROUP), 0) % N_HEADS
        l_head = lax.broadcasted_iota(jnp.int32, (rows, D_GROUP), 1) // HEAD_DIM
        qbd_ref[...] = jnp.where(r_head == l_head, rep, 0.0).astype(BF16)
        m_ref[...] = jnp.full_like(m_ref, NEG_INF)
        l_ref[...] = jnp.zeros_like(l_ref)
        acc_ref[...] = jnp.zeros_like(acc_ref)
        carry_ref[...] = jnp.zeros_like(carry_ref)
        pad = jnp.zeros((page - n_q, D_GROUP), F32)
        k_new = jnp.concatenate([kn_ref[...], pad], axis=0).astype(BF16)
        v_new = jnp.concatenate([vn_ref[...], pad], axis=0).astype(BF16)
        s = lax.dot_general(qbd_ref[...], k_new, NT_DIMS, preferred_element_type=F32)
        key_pos = lax.broadcasted_iota(jnp.int32, (N_HEADS, page), 1)
        parts = []
        for t in range(n_q):
            s_t = s[t * N_HEADS:(t + 1) * N_HEADS, :] + cn[:, t:t + 1] - cn
            parts.append(jnp.where(key_pos <= t, s_t, NEG_INF))
        attend(jnp.concatenate(parts, axis=0), v_new)

    x = jnp.concatenate([r[0] for r in lt_refs], axis=0)
    later = (lax.broadcasted_iota(jnp.int32, (page, page), 0)
             > lax.broadcasted_iota(jnp.int32, (page, page), 1))
    d_loc = jnp.dot(x, jnp.where(later, 1.0, 0.0).astype(F32), preferred_element_type=F32,
                    precision=lax.Precision.HIGHEST)
    tot = jnp.sum(x, axis=1, keepdims=True)
    run = carry_ref[...]
    d_pages = [None] * n_group
    for r in reversed(range(n_group)):
        d_pages[r] = d_loc[r * N_HEADS:(r + 1) * N_HEADS, :] + run
        run = run + tot[r * N_HEADS:(r + 1) * N_HEADS, :]
    carry_ref[...] = run
    d_full = jnp.concatenate(d_pages, axis=1)
    k_all = jnp.concatenate([r[0].astype(BF16) for r in k_refs], axis=0)
    v_all = jnp.concatenate([r[0].astype(BF16) for r in v_refs], axis=0)
    s = lax.dot_general(qbd_ref[...], k_all, NT_DIMS, preferred_element_type=F32)
    parts = [s[t * N_HEADS:(t + 1) * N_HEADS, :] + d_full + cn[:, t:t + 1] for t in range(n_q)]
    attend(jnp.concatenate(parts, axis=0), v_all)

    @pl.when(g == pl.num_programs(1) - 1)
    def _():
        o = acc_ref[...] / l_ref[...]
        r_head = lax.broadcasted_iota(jnp.int32, (rows, D_GROUP), 0) % N_HEADS
        l_head = lax.broadcasted_iota(jnp.int32, (rows, D_GROUP), 1) // HEAD_DIM
        o = jnp.where(r_head == l_head, o, 0.0)
        o_ref[...] = jnp.concatenate(
            [jnp.sum(o[t * N_HEADS:(t + 1) * N_HEADS, :], axis=0, keepdims=True) for t in range(n_q)], axis=0)


def _fox_decode(fq, fk, fv, cn, cache_k, cache_v, cache_lt, page_table, n_q):
    batch, n_pages = page_table.shape
    n_pool, page, _ = cache_k.shape
    n_group = _pick_tile(n_pages, 8, 1)
    n_steps = n_pages // n_group
    rows = n_q * N_HEADS

    def page_map(r):
        def index(b, g, pt):
            return (pt[b * n_pages + (n_steps - 1 - g) * n_group + r], 0, 0)
        return index

    per_seq = pl.BlockSpec((n_q, D_GROUP), lambda b, g, pt: (b, 0))
    in_specs = [per_seq, per_seq, per_seq, pl.BlockSpec((1, N_HEADS, LANES), lambda b, g, pt: (b, 0, 0))]
    in_specs += [pl.BlockSpec((1, N_HEADS, page), page_map(r)) for r in range(n_group)]
    in_specs += [pl.BlockSpec((1, page, D_GROUP), page_map(r)) for r in range(n_group)]
    in_specs += [pl.BlockSpec((1, page, D_GROUP), page_map(r)) for r in range(n_group)]
    grid_spec = pltpu.PrefetchScalarGridSpec(
        num_scalar_prefetch=1,
        grid=(batch, n_steps),
        in_specs=in_specs,
        out_specs=per_seq,
        scratch_shapes=[pltpu.VMEM((rows, D_GROUP), BF16), pltpu.VMEM((rows, 1), F32),
                        pltpu.VMEM((rows, 1), F32), pltpu.VMEM((rows, D_GROUP), F32),
                        pltpu.VMEM((N_HEADS, 1), F32)],
    )
    return pl.pallas_call(
        functools.partial(_fox_decode_kernel, n_group),
        grid_spec=grid_spec,
        out_shape=jax.ShapeDtypeStruct((batch * n_q, D_GROUP), F32),
        compiler_params=_params("parallel", "arbitrary"),
        name="fox_decode",
    )(page_table.reshape(-1), fq, fk, fv, cn,
      *([cache_lt] * n_group), *([cache_k] * n_group), *([cache_v] * n_group))


def _outproj_kernel(h_ref, a_ref, b_ref, w_ref, o_ref):
    half = a_ref.shape[1]
    mix = _dot(a_ref[...].astype(BF16), w_ref[:half, :]) + _dot(b_ref[...].astype(BF16), w_ref[half:, :])
    o_ref[...] = h_ref[...] + mix


def _outproj(h, o_ret, o_fox, w_out):
    t, d = h.shape
    tm = _pick_tile(t, 512, 8)
    row = lambda i: (i, 0)
    return pl.pallas_call(
        _outproj_kernel,
        grid=(t // tm,),
        in_specs=[pl.BlockSpec((tm, d), row), pl.BlockSpec((tm, D_GROUP), row),
                  pl.BlockSpec((tm, D_GROUP), row), pl.BlockSpec(w_out.shape, lambda i: (0, 0))],
        out_specs=pl.BlockSpec((tm, d), row),
        out_shape=jax.ShapeDtypeStruct((t, d), F32),
        compiler_params=_params("parallel"),
        name="outproj",
    )(h, o_ret, o_fox, w_out)


def _ple_kernel(final, h_ref, p_ref, g_ref, wg_ref, bg_ref, wp_ref, gf_ref, o_ref):
    h = h_ref[...]
    gate = jax.nn.sigmoid(_dot(_rms(h, g_ref[...]).astype(BF16), wg_ref[...]) + bg_ref[...])
    h = h + gate * _dot(p_ref[...].astype(BF16), wp_ref[...])
    o_ref[...] = _rms(h, gf_ref[...]) if final else h


def _ple(h, p, g, w_gate, b_gate, w_proj, g_final, final):
    t, d = h.shape
    tm = _pick_tile(t, 512, 8)
    row = lambda i: (i, 0)
    fixed = lambda i: (0, 0)
    return pl.pallas_call(
        functools.partial(_ple_kernel, final),
        grid=(t // tm,),
        in_specs=[pl.BlockSpec((tm, d), row), pl.BlockSpec((tm, p.shape[1]), row),
                  pl.BlockSpec((1, d), fixed), pl.BlockSpec(w_gate.shape, fixed),
                  pl.BlockSpec((1, d), fixed), pl.BlockSpec(w_proj.shape, fixed),
                  pl.BlockSpec((1, d), fixed)],
        out_specs=pl.BlockSpec((tm, d), row),
        out_shape=jax.ShapeDtypeStruct((t, d), F32),
        compiler_params=_params("parallel"),
        name="ple_gate",
    )(h, p, g, w_gate, b_gate, w_proj, g_final)


def _rope_tables(pos):
    half = HEAD_DIM // 2
    inv = ROPE_BASE ** (-jnp.arange(half, dtype=F32) / half)
    ang = pos.astype(F32)[:, None] * inv[None, :]
    cos = jnp.cos(ang)
    sin = jnp.sin(ang)
    return jnp.tile(cos, (1, LANES // half)), jnp.tile(jnp.concatenate([-sin, sin], axis=1), (1, LANES // HEAD_DIM))


def _row(v):
    return v.reshape(1, -1)


def _layer(x, p, pos, seq, s0, past, lw, g_final, final):
    t = x.shape[0]
    batch = t // seq
    prompt = past is None
    cos_t, sin_t = _rope_tables(pos)
    h = _ffn(x, _row(lw["g_ffn1"]), lw["w1_ffn1"], lw["w3_ffn1"], lw["w2_ffn1"])
    outs = _inproj(h, _row(lw["g_mix"]), lw["w_main"], lw["w_ff"], lw["b_ff"], cos_t, sin_t, prompt)
    rq, rk, rv, rg, fk, fv, lf = outs[:7]
    o_ret, s_ret = _retention(rq, rk, rv, rg, s0, _row(lw["gn_w"]), _row(lw["gn_b"]), seq,
                              BF16 if prompt else F32)
    c, ct = _cumsum(lf, seq)
    if prompt:
        qh, kh, vh = outs[7:]
        o_fox = _fox_prompt(qh, kh, vh, c, ct[:N_HEADS])
    else:
        (fq,) = outs[7:]
        cache_k, cache_v, cache_lt, page_table = past
        cn = ct[:N_HEADS].reshape(N_HEADS, batch, seq).transpose(1, 0, 2)
        cn = jnp.pad(cn, ((0, 0), (0, 0), (0, LANES - seq)))
        o_fox = _fox_decode(fq, fk, fv, cn, cache_k, cache_v, cache_lt, page_table, seq)
    h = _outproj(h, o_ret, o_fox, lw["w_out"])
    h = _ffn(h, _row(lw["g_ffn2"]), lw["w1_ffn2"], lw["w3_ffn2"], lw["w2_ffn2"])
    h = _ple(h, p, _row(lw["g_ple"]), lw["w_ple_gate"], _row(lw["b_ple_gate"]), lw["w_ple_proj"],
             _row(g_final), final)
    return h, fk, fv, lf[:, :N_HEADS], s_ret


def kernel(x_prompt, x_sample, cache_fox_k, cache_fox_v, cache_fox_logf, state_ret, page_table, p_prompt, p_sample, g_ffn1, w1_ffn1, w3_ffn1, w2_ffn1, g_mix, w_in, b_forget, gn_w, gn_b, w_out, g_ffn2, w1_ffn2, w3_ffn2, w2_ffn2, g_ple, w_ple_gate, b_ple_gate, w_ple_proj, g_final):
    depth = w_in.shape[0]
    batch_p, seq_p, d = x_prompt.shape
    batch_s, seq_s, _ = x_sample.shape
    n_pages = page_table.shape[1]
    page = cache_fox_k.shape[2]
    n_pool = cache_fox_k.shape[1]
    n_main = w_in.shape[2] - N_HEADS
    hp = x_prompt.reshape(batch_p * seq_p, d)
    hs = x_sample.reshape(batch_s * seq_s, d)
    pos_p = jnp.tile(jnp.arange(seq_p), batch_p)
    pos_s = jnp.tile(n_pages * page + jnp.arange(seq_s), batch_s)
    per_layer = [[] for _ in range(8)]
    for i in range(depth):
        lw = {
            "g_ffn1": g_ffn1[i], "w1_ffn1": w1_ffn1[i].astype(BF16), "w3_ffn1": w3_ffn1[i].astype(BF16),
            "w2_ffn1": w2_ffn1[i].astype(BF16), "g_mix": g_mix[i],
            "w_main": w_in[i, :, :n_main].astype(BF16),
            "w_ff": jnp.pad(w_in[i, :, n_main:], ((0, 0), (0, LANES - N_HEADS))).astype(BF16),
            "b_ff": jnp.pad(b_forget[i], (0, LANES - N_HEADS)).reshape(1, LANES),
            "gn_w": gn_w[i], "gn_b": gn_b[i], "w_out": w_out[i].astype(BF16),
            "g_ffn2": g_ffn2[i], "w1_ffn2": w1_ffn2[i].astype(BF16), "w3_ffn2": w3_ffn2[i].astype(BF16),
            "w2_ffn2": w2_ffn2[i].astype(BF16), "g_ple": g_ple[i], "w_ple_gate": w_ple_gate[i].astype(BF16),
            "b_ple_gate": b_ple_gate[i], "w_ple_proj": w_ple_proj[i].astype(BF16),
        }
        final = i == depth - 1
        s0_p = jnp.zeros((batch_p, N_HEADS, HEAD_DIM, HEAD_DIM), F32)
        hp, kp, vp, lp, sp = _layer(hp, p_prompt[i].reshape(batch_p * seq_p, -1), pos_p, seq_p, s0_p, None,
                                    lw, g_final, final)
        past = (cache_fox_k[i].reshape(n_pool, page, D_GROUP), cache_fox_v[i].reshape(n_pool, page, D_GROUP),
                cache_fox_logf[i].transpose(0, 2, 1), page_table)
        hs, ks, vs, ls, ss = _layer(hs, p_sample[i].reshape(batch_s * seq_s, -1), pos_s, seq_s, state_ret[i],
                                    past, lw, g_final, final)
        for lst, val in zip(per_layer, (
                kp.reshape(batch_p, seq_p, N_HEADS, HEAD_DIM), vp.reshape(batch_p, seq_p, N_HEADS, HEAD_DIM),
                lp.reshape(batch_p, seq_p, N_HEADS), sp,
                ks.reshape(batch_s, seq_s, N_HEADS, HEAD_DIM), vs.reshape(batch_s, seq_s, N_HEADS, HEAD_DIM),
                ls.reshape(batch_s, seq_s, N_HEADS), ss)):
            lst.append(val)
    stacked = [jnp.stack(lst) for lst in per_layer]
    return (hp.reshape(batch_p, seq_p, d), hs.reshape(batch_s, seq_s, d), *stacked)
```

```python
import functools

import numpy as np
import jax
import jax.numpy as jnp
from jax import lax
from jax.experimental import pallas as pl
from jax.experimental.pallas import tpu as pltpu

F32 = jnp.float32
BF16 = jnp.bfloat16

HEAD_DIM = 64
N_HEADS = 8
D_GROUP = N_HEADS * HEAD_DIM
RET_CHUNK = 128
ROPE_BASE = 10000.0
NORM_EPS = 1e-6
GN_EPS = 1e-5
NEG_INF = -1e30
LANES = 128
VMEM_LIMIT = 52 * 1024 * 1024

NN_DIMS = (((1,), (0,)), ((), ()))
NT_DIMS = (((1,), (1,)), ((), ()))
TN_DIMS = (((0,), (0,)), ((), ()))


def _pick_tile(n, pref, mult):
    best = None
    for t in range(mult, min(n, pref) + 1, mult):
        if n % t == 0:
            best = t
    return best if best is not None else n


def _params(*sem):
    return pltpu.CompilerParams(dimension_semantics=sem, vmem_limit_bytes=VMEM_LIMIT)


def _rms(x, g):
    r = lax.rsqrt(jnp.mean(x * x, axis=-1, keepdims=True) + NORM_EPS)
    return x * r * g


def _dot(a, b):
    return jnp.dot(a, b, preferred_element_type=F32)


def _ffn_kernel(x_ref, g_ref, w1_ref, w3_ref, w2_ref, o_ref, xn_ref, acc_ref):
    j = pl.program_id(1)

    @pl.when(j == 0)
    def _():
        xn_ref[...] = _rms(x_ref[...], g_ref[...]).astype(BF16)
        acc_ref[...] = jnp.zeros_like(acc_ref)

    xn = xn_ref[...]
    h1 = _dot(xn, w1_ref[...])
    h3 = _dot(xn, w3_ref[...])
    a = (h1 * jax.nn.sigmoid(h1) * h3).astype(BF16)
    acc_ref[...] += _dot(a, w2_ref[...])

    @pl.when(j == pl.num_programs(1) - 1)
    def _():
        o_ref[...] = x_ref[...] + 0.5 * acc_ref[...]


def _ffn(x, g, w1, w3, w2):
    t, d = x.shape
    d_ff = w1.shape[1]
    tm = _pick_tile(t, 512, 8)
    tf = _pick_tile(d_ff, 1408, LANES)
    return pl.pallas_call(
        _ffn_kernel,
        grid=(t // tm, d_ff // tf),
        in_specs=[
            pl.BlockSpec((tm, d), lambda i, j: (i, 0)),
            pl.BlockSpec((1, d), lambda i, j: (0, 0)),
            pl.BlockSpec((d, tf), lambda i, j: (0, j)),
            pl.BlockSpec((d, tf), lambda i, j: (0, j)),
            pl.BlockSpec((tf, d), lambda i, j: (j, 0)),
        ],
        out_specs=pl.BlockSpec((tm, d), lambda i, j: (i, 0)),
        out_shape=jax.ShapeDtypeStruct((t, d), F32),
        scratch_shapes=[pltpu.VMEM((tm, d), BF16), pltpu.VMEM((tm, d), F32)],
        compiler_params=_params("parallel", "arbitrary"),
        name="ffn",
    )(x, g, w1, w3, w2)


def _inproj_kernel(head_major, h_ref, g_ref, w_ref, wff_ref, bf_ref, cos_ref, sin_ref,
                   rq_ref, rk_ref, rv_ref, rg_ref, fk_ref, fv_ref, lf_ref, *q_refs):
    tm = h_ref.shape[0]
    u = _rms(h_ref[...], g_ref[...]).astype(BF16)

    def piece(p):
        return _dot(u, w_ref[:, p * D_GROUP:(p + 1) * D_GROUP])

    cos = cos_ref[...]
    sin = sin_ref[...]
    lane = lax.broadcasted_iota(jnp.int32, (tm, LANES), 1)
    first_half = (lane % HEAD_DIM) < (HEAD_DIM // 2)

    def rope(x):
        outs = []
        for cb in range(D_GROUP // LANES):
            xb = x[:, cb * LANES:(cb + 1) * LANES]
            partner = jnp.where(first_half,
                                pltpu.roll(xb, LANES - HEAD_DIM // 2, 1),
                                pltpu.roll(xb, HEAD_DIM // 2, 1))
            outs.append(xb * cos + partner * sin)
        return jnp.concatenate(outs, axis=1)

    scale = HEAD_DIM ** -0.5
    rq_ref[...] = rope(piece(0)).astype(rq_ref.dtype)
    rk_ref[...] = (rope(piece(1)) * scale).astype(rk_ref.dtype)
    rv_ref[...] = piece(2).astype(rv_ref.dtype)
    rg_ref[...] = piece(3)
    fq = piece(4) * scale
    fk = piece(5)
    fv = piece(6)
    fk_ref[...] = fk
    fv_ref[...] = fv
    if head_major:
        qt_ref, kh_ref, vt_ref = q_refs
        qt_ref[...] = fq.T.astype(BF16).reshape(N_HEADS, HEAD_DIM, tm)
        vt_ref[...] = fv.T.astype(BF16).reshape(N_HEADS, HEAD_DIM, tm)
        for h in range(N_HEADS):
            kh_ref[h] = fk[:, h * HEAD_DIM:(h + 1) * HEAD_DIM].astype(BF16)
    else:
        (fq_ref,) = q_refs
        fq_ref[...] = fq
    ff = _dot(u, wff_ref[...]) + bf_ref[...]
    lf_ref[...] = jnp.minimum(ff, 0.0) - jnp.log1p(jnp.exp(-jnp.abs(ff)))


def _inproj(h, g, w_main, w_ff, b_ff, cos_t, sin_t, head_major):
    t, d = h.shape
    tm = _pick_tile(t, 512, 8)
    act = BF16 if head_major else F32
    row = lambda i: (i, 0)
    fixed = lambda i: (0, 0)
    wide = pl.BlockSpec((tm, D_GROUP), row)
    out_shape = [jax.ShapeDtypeStruct((t, D_GROUP), act)] * 3 + [
        jax.ShapeDtypeStruct((t, D_GROUP), F32)] * 3 + [jax.ShapeDtypeStruct((t, LANES), F32)]
    out_specs = [wide] * 6 + [pl.BlockSpec((tm, LANES), row)]
    if head_major:
        transposed = jax.ShapeDtypeStruct((N_HEADS, HEAD_DIM, t), BF16)
        transposed_spec = pl.BlockSpec((N_HEADS, HEAD_DIM, tm), lambda i: (0, 0, i))
        out_shape += [transposed, jax.ShapeDtypeStruct((N_HEADS, t, HEAD_DIM), BF16), transposed]
        out_specs += [transposed_spec, pl.BlockSpec((N_HEADS, tm, HEAD_DIM), lambda i: (0, i, 0)), transposed_spec]
    else:
        out_shape += [jax.ShapeDtypeStruct((t, D_GROUP), F32)]
        out_specs += [wide]
    return pl.pallas_call(
        functools.partial(_inproj_kernel, head_major),
        grid=(t // tm,),
        in_specs=[
            pl.BlockSpec((tm, d), row),
            pl.BlockSpec((1, d), fixed),
            pl.BlockSpec(w_main.shape, fixed),
            pl.BlockSpec(w_ff.shape, fixed),
            pl.BlockSpec((1, LANES), fixed),
            pl.BlockSpec((tm, LANES), row),
            pl.BlockSpec((tm, LANES), row),
        ],
        out_specs=out_specs,
        out_shape=out_shape,
        compiler_params=_params("parallel"),
        name="inproj",
    )(h, g, w_main, w_ff, b_ff, cos_t, sin_t)


def _cumsum_kernel(seg, x_ref, c_ref, ct_ref, carry_ref):
    tb = x_ref.shape[0]
    row = lax.broadcasted_iota(jnp.int32, (tb, tb), 0)
    col = lax.broadcasted_iota(jnp.int32, (tb, tb), 1)
    keep = col <= row
    carried = seg > tb
    if not carried:
        keep = keep & ((row // seg) == (col // seg))
    tri = jnp.where(keep, 1.0, 0.0).astype(F32)
    c = jnp.dot(tri, x_ref[...], preferred_element_type=F32, precision=lax.Precision.HIGHEST)
    if carried:
        @pl.when(pl.program_id(0) == 0)
        def _():
            carry_ref[...] = jnp.zeros_like(carry_ref)

        c = c + carry_ref[...]
        carry_ref[...] = c[tb - 1:tb, :]
    c_ref[...] = c
    ct_ref[...] = c.T


def _cumsum(x, seg):
    t = x.shape[0]
    tb = _pick_tile(t, 512, LANES)
    assert seg % tb == 0 or tb % seg == 0
    return pl.pallas_call(
        functools.partial(_cumsum_kernel, seg),
        grid=(t // tb,),
        in_specs=[pl.BlockSpec((tb, LANES), lambda i: (i, 0))],
        out_specs=[pl.BlockSpec((tb, LANES), lambda i: (i, 0)),
                   pl.BlockSpec((LANES, tb), lambda i: (0, i))],
        out_shape=[jax.ShapeDtypeStruct((t, LANES), F32), jax.ShapeDtypeStruct((LANES, t), F32)],
        scratch_shapes=[pltpu.VMEM((1, LANES), F32)],
        compiler_params=_params("arbitrary"),
        name="forget_cumsum",
    )(x)


BIAS_ROWS = 16


def _split3(x):
    hi = x.astype(BF16).astype(F32)
    r = x - hi
    mid = r.astype(BF16).astype(F32)
    lo = (r - mid).astype(BF16).astype(F32)
    return hi, mid, lo


def _cumsum_pack_kernel(x_ref, qt_ref, kh_ref, qa_ref, ka_ref, carry_ref):
    tb = x_ref.shape[0]
    row = lax.broadcasted_iota(jnp.int32, (tb, tb), 0)
    col = lax.broadcasted_iota(jnp.int32, (tb, tb), 1)
    tri = jnp.where(col <= row, 1.0, 0.0).astype(F32)

    @pl.when(pl.program_id(0) == 0)
    def _():
        carry_ref[...] = jnp.zeros_like(carry_ref)

    c = jnp.dot(tri, x_ref[...], preferred_element_type=F32, precision=lax.Precision.HIGHEST) + carry_ref[...]
    carry_ref[...] = c[tb - 1:tb, :]
    ct = c.T
    sub = lax.broadcasted_iota(jnp.int32, (BIAS_ROWS, tb), 0)
    lane = lax.broadcasted_iota(jnp.int32, (tb, HEAD_DIM), 1)
    for h in range(N_HEADS):
        hi, mid, lo = _split3(ct[h:h + 1, :])
        extra = jnp.where(sub == 0, hi, jnp.where(sub == 1, mid, jnp.where(sub == 2, lo,
                                                                          jnp.where(sub < 6, 1.0, 0.0))))
        qa_ref[h, 0:HEAD_DIM, :] = qt_ref[h]
        qa_ref[h, HEAD_DIM:HEAD_DIM + BIAS_ROWS, :] = extra.astype(BF16)
        qa_ref[h, HEAD_DIM + BIAS_ROWS:, :] = jnp.zeros((LANES - HEAD_DIM - BIAS_ROWS, tb), BF16)
        hi, mid, lo = _split3(c[:, h:h + 1])
        extra = jnp.where(lane < 3, 1.0, jnp.where(lane == 3, -hi, jnp.where(lane == 4, -mid,
                                                                             jnp.where(lane == 5, -lo, 0.0))))
        ka_ref[h, :, 0:HEAD_DIM] = kh_ref[h]
        ka_ref[h, :, HEAD_DIM:] = extra.astype(BF16)


def _cumsum_pack(x, qt, kh):
    t = x.shape[0]
    tb = _pick_tile(t, 512, LANES)
    return pl.pallas_call(
        _cumsum_pack_kernel,
        grid=(t // tb,),
        in_specs=[pl.BlockSpec((tb, LANES), lambda i: (i, 0)),
                  pl.BlockSpec((N_HEADS, HEAD_DIM, tb), lambda i: (0, 0, i)),
                  pl.BlockSpec((N_HEADS, tb, HEAD_DIM), lambda i: (0, i, 0))],
        out_specs=[pl.BlockSpec((N_HEADS, LANES, tb), lambda i: (0, 0, i)),
                   pl.BlockSpec((N_HEADS, tb, LANES), lambda i: (0, i, 0))],
        out_shape=[jax.ShapeDtypeStruct((N_HEADS, LANES, t), BF16),
                   jax.ShapeDtypeStruct((N_HEADS, t, LANES), BF16)],
        scratch_shapes=[pltpu.VMEM((1, LANES), F32)],
        compiler_params=_params("arbitrary"),
        name="forget_cumsum_pack",
    )(x, qt, kh)


def _ret_kernel(q_ref, k_ref, v_ref, g_ref, s0_ref, intra_ref, qdec_ref, kdec_ref, cdec_ref,
                gnw_ref, gnb_ref, o_ref, sout_ref, s_ref):
    c = pl.program_id(1)

    @pl.when(c == 0)
    def _():
        s_ref[...] = s0_ref[0]

    mm = BF16 if q_ref.shape[0] >= 16 else F32
    q = q_ref[...].astype(F32)
    k = k_ref[...].astype(F32)
    qb = q.astype(mm)
    kb = k.astype(mm)
    vb = v_ref[...].astype(mm)
    qd = (q * qdec_ref[...]).astype(mm)
    kd = (k * kdec_ref[...]).astype(mm)
    outs = []
    for h in range(N_HEADS):
        sl = slice(h * HEAD_DIM, (h + 1) * HEAD_DIM)
        att = lax.dot_general(qb[:, sl], kb[:, sl], NT_DIMS, preferred_element_type=F32) * intra_ref[h]
        s_old = s_ref[h]
        o = _dot(att.astype(mm), vb[:, sl]) + _dot(qd[:, sl], s_old.astype(mm))
        s_ref[h] = s_old * cdec_ref[h] + lax.dot_general(kd[:, sl], vb[:, sl], TN_DIMS,
                                                         preferred_element_type=F32)
        mu = jnp.mean(o, axis=-1, keepdims=True)
        dev = o - mu
        var = jnp.mean(dev * dev, axis=-1, keepdims=True)
        outs.append(dev * lax.rsqrt(var + GN_EPS))
    on = jnp.concatenate(outs, axis=1)
    gate = g_ref[...]
    o_ref[...] = ((on * gnw_ref[...] + gnb_ref[...]) * (gate * jax.nn.sigmoid(gate))).astype(o_ref.dtype)

    @pl.when(c == pl.num_programs(1) - 1)
    def _():
        sout_ref[0] = s_ref[...]


def _ret_tables(chunk):
    lg = jnp.log1p(-(2.0 ** (-5.0 - jnp.arange(N_HEADS, dtype=F32))))
    i = jnp.arange(chunk, dtype=F32)
    diff = i[:, None] - i[None, :]
    intra = jnp.where(diff[None] >= 0, jnp.exp(jnp.maximum(diff, 0.0)[None] * lg[:, None, None]), 0.0)
    q_dec = jnp.exp((i[:, None] + 1.0) * lg[None, :])
    k_dec = jnp.exp((chunk - 1.0 - i)[:, None] * lg[None, :])
    c_dec = jnp.exp(chunk * lg)
    expand = lambda a: jnp.repeat(a, HEAD_DIM, axis=1)
    return intra, expand(q_dec), expand(k_dec), jnp.broadcast_to(c_dec[:, None, None], (N_HEADS, 1, HEAD_DIM))


def _retention(q, k, v, g, s0, gn_w, gn_b, seq, out_dtype):
    t = q.shape[0]
    batch = t // seq
    chunk = min(RET_CHUNK, seq)
    n = seq // chunk
    intra, q_dec, k_dec, c_dec = _ret_tables(chunk)
    blk = pl.BlockSpec((chunk, D_GROUP), lambda b, c: (b * n + c, 0))
    fixed2 = lambda b, c: (0, 0)
    fixed3 = lambda b, c: (0, 0, 0)
    state = pl.BlockSpec((1, N_HEADS, HEAD_DIM, HEAD_DIM), lambda b, c: (b, 0, 0, 0))
    return pl.pallas_call(
        _ret_kernel,
        grid=(batch, n),
        in_specs=[blk, blk, blk, blk, state,
                  pl.BlockSpec(intra.shape, fixed3),
                  pl.BlockSpec(q_dec.shape, fixed2),
                  pl.BlockSpec(k_dec.shape, fixed2),
                  pl.BlockSpec(c_dec.shape, fixed3),
                  pl.BlockSpec((1, D_GROUP), fixed2),
                  pl.BlockSpec((1, D_GROUP), fixed2)],
        out_specs=[blk, state],
        out_shape=[jax.ShapeDtypeStruct((t, D_GROUP), out_dtype),
                   jax.ShapeDtypeStruct((batch, N_HEADS, HEAD_DIM, HEAD_DIM), F32)],
        scratch_shapes=[pltpu.VMEM((N_HEADS, HEAD_DIM, HEAD_DIM), F32)],
        compiler_params=_params("parallel", "arbitrary"),
        name="retention",
    )(q, k, v, g, s0, intra, q_dec, k_dec, c_dec, gn_w, gn_b)


def _fox_prompt_kernel(qi_ref, kj_ref, qa_ref, ka_ref, vt_ref, o_ref, m_ref, l_ref, acc_ref):
    step = pl.program_id(0)
    i = qi_ref[step]
    j = kj_ref[step]
    tk = ka_ref.shape[1]
    tq = qa_ref.shape[2]

    @pl.when(j == 0)
    def _():
        m_ref[...] = jnp.full_like(m_ref, NEG_INF)
        l_ref[...] = jnp.zeros_like(l_ref)
        acc_ref[...] = jnp.zeros_like(acc_ref)

    def update(masked):
        if masked:
            causal = (lax.broadcasted_iota(jnp.int32, (tk, tq), 0)
                      <= lax.broadcasted_iota(jnp.int32, (tk, tq), 1))
        for h in range(N_HEADS):
            s = _dot(ka_ref[h], qa_ref[h])
            if masked:
                s = jnp.where(causal, s, NEG_INF)
            m_prev = m_ref[h]
            m_new = jnp.maximum(m_prev, jnp.max(s, axis=0, keepdims=True))
            alpha = jnp.exp(m_prev - m_new)
            p = jnp.exp(s - m_new)
            l_ref[h] = alpha * l_ref[h] + jnp.sum(p, axis=0, keepdims=True)
            acc_ref[h] = alpha * acc_ref[h] + _dot(vt_ref[h], p.astype(BF16))
            m_ref[h] = m_new

    @pl.when(j < i)
    def _():
        update(False)

    @pl.when(j == i)
    def _():
        update(True)
        for h in range(N_HEADS):
            o_ref[h] = (acc_ref[h] / l_ref[h]).astype(o_ref.dtype)


def _fox_prompt(qa, ka, vt):
    t = ka.shape[1]
    tq = _pick_tile(t, 512, LANES)
    nq = t // tq
    qi = np.array([i for i in range(nq) for _ in range(i + 1)], np.int32)
    kj = np.array([j for i in range(nq) for j in range(i + 1)], np.int32)
    grid_spec = pltpu.PrefetchScalarGridSpec(
        num_scalar_prefetch=2,
        grid=(len(qi),),
        in_specs=[pl.BlockSpec((N_HEADS, LANES, tq), lambda s, qi, kj: (0, 0, qi[s])),
                  pl.BlockSpec((N_HEADS, tq, LANES), lambda s, qi, kj: (0, kj[s], 0)),
                  pl.BlockSpec((N_HEADS, HEAD_DIM, tq), lambda s, qi, kj: (0, 0, kj[s]))],
        out_specs=pl.BlockSpec((N_HEADS, HEAD_DIM, tq), lambda s, qi, kj: (0, 0, qi[s])),
        scratch_shapes=[pltpu.VMEM((N_HEADS, 1, tq), F32), pltpu.VMEM((N_HEADS, 1, tq), F32),
                        pltpu.VMEM((N_HEADS, HEAD_DIM, tq), F32)],
    )
    return pl.pallas_call(
        _fox_prompt_kernel,
        grid_spec=grid_spec,
        out_shape=jax.ShapeDtypeStruct((N_HEADS, HEAD_DIM, t), BF16),
        compiler_params=_params("arbitrary"),
        name="fox_prompt",
    )(jnp.asarray(qi), jnp.asarray(kj), qa, ka, vt)


def _fox_decode_kernel(n_group, pt_ref, q_ref, kn_ref, vn_ref, cn_ref, *refs):
    lt_refs = refs[:n_group]
    k_refs = refs[n_group:2 * n_group]
    v_refs = refs[2 * n_group:3 * n_group]
    o_ref, qbd_ref, m_ref, l_ref, acc_ref, carry_ref = refs[3 * n_group:]
    g = pl.program_id(1)
    n_q = q_ref.shape[0]
    rows = n_q * N_HEADS
    page = k_refs[0].shape[2]
    cn = cn_ref[0]

    def attend(s, v, v_dims):
        m_prev = m_ref[...]
        m_new = jnp.maximum(m_prev, jnp.max(s, axis=1, keepdims=True))
        alpha = jnp.exp(m_prev - m_new)
        p = jnp.exp(s - m_new)
        l_ref[...] = alpha * l_ref[...] + jnp.sum(p, axis=1, keepdims=True)
        acc_ref[...] = alpha * acc_ref[...] + lax.dot_general(p.astype(BF16), v, v_dims,
                                                              preferred_element_type=F32)
        m_ref[...] = m_new

    @pl.when(g == 0)
    def _():
        q = q_ref[...]
        rep = jnp.concatenate([jnp.broadcast_to(q[t:t + 1, :], (N_HEADS, D_GROUP)) for t in range(n_q)], axis=0)
        r_head = lax.broadcasted_iota(jnp.int32, (rows, D_GROUP), 0) % N_HEADS
        l_head = lax.broadcasted_iota(jnp.int32, (rows, D_GROUP), 1) // HEAD_DIM
        qbd_ref[...] = jnp.where(r_head == l_head, rep, 0.0).astype(BF16)
        m_ref[...] = jnp.full_like(m_ref, NEG_INF)
        l_ref[...] = jnp.zeros_like(l_ref)
        acc_ref[...] = jnp.zeros_like(acc_ref)
        carry_ref[...] = jnp.zeros_like(carry_ref)
        pad = jnp.zeros((page - n_q, D_GROUP), F32)
        k_new = jnp.concatenate([kn_ref[...], pad], axis=0).astype(BF16)
        v_new = jnp.concatenate([vn_ref[...], pad], axis=0).astype(BF16)
        s = lax.dot_general(qbd_ref[...], k_new, NT_DIMS, preferred_element_type=F32)
        key_pos = lax.broadcasted_iota(jnp.int32, (N_HEADS, page), 1)
        parts = []
        for t in range(n_q):
            s_t = s[t * N_HEADS:(t + 1) * N_HEADS, :] + cn[:, t:t + 1] - cn
            parts.append(jnp.where(key_pos <= t, s_t, NEG_INF))
        attend(jnp.concatenate(parts, axis=0), v_new, NN_DIMS)

    x = jnp.concatenate([r[0] for r in lt_refs], axis=0)
    later = (lax.broadcasted_iota(jnp.int32, (page, page), 0)
             > lax.broadcasted_iota(jnp.int32, (page, page), 1))
    d_loc = jnp.dot(x, jnp.where(later, 1.0, 0.0).astype(F32), preferred_element_type=F32,
                    precision=lax.Precision.HIGHEST)
    tot = jnp.sum(x, axis=1, keepdims=True)
    run = carry_ref[...]
    d_pages = [None] * n_group
    for r in reversed(range(n_group)):
        d_pages[r] = d_loc[r * N_HEADS:(r + 1) * N_HEADS, :] + run
        run = run + tot[r * N_HEADS:(r + 1) * N_HEADS, :]
    carry_ref[...] = run
    d_full = jnp.concatenate(d_pages, axis=1)
    kt_all = jnp.concatenate([r[0].astype(BF16) for r in k_refs], axis=1)
    vt_all = jnp.concatenate([r[0].astype(BF16) for r in v_refs], axis=1)
    s = _dot(qbd_ref[...], kt_all)
    parts = [s[t * N_HEADS:(t + 1) * N_HEADS, :] + d_full + cn[:, t:t + 1] for t in range(n_q)]
    attend(jnp.concatenate(parts, axis=0), vt_all, NT_DIMS)

    @pl.when(g == pl.num_programs(1) - 1)
    def _():
        o = acc_ref[...] / l_ref[...]
        r_head = lax.broadcasted_iota(jnp.int32, (rows, D_GROUP), 0) % N_HEADS
        l_head = lax.broadcasted_iota(jnp.int32, (rows, D_GROUP), 1) // HEAD_DIM
        o = jnp.where(r_head == l_head, o, 0.0)
        o_ref[...] = jnp.concatenate(
            [jnp.sum(o[t * N_HEADS:(t + 1) * N_HEADS, :], axis=0, keepdims=True) for t in range(n_q)], axis=0)


def _fox_decode(fq, fk, fv, cn, cache_kt, cache_vt, cache_lt, page_table, n_q):
    batch, n_pages = page_table.shape
    n_pool, _, page = cache_kt.shape
    n_group = _pick_tile(n_pages, 16, 1)
    n_steps = n_pages // n_group
    rows = n_q * N_HEADS

    def page_map(r):
        def index(b, g, pt):
            return (pt[b * n_pages + (n_steps - 1 - g) * n_group + r], 0, 0)
        return index

    per_seq = pl.BlockSpec((n_q, D_GROUP), lambda b, g, pt: (b, 0))
    in_specs = [per_seq, per_seq, per_seq, pl.BlockSpec((1, N_HEADS, LANES), lambda b, g, pt: (b, 0, 0))]
    in_specs += [pl.BlockSpec((1, N_HEADS, page), page_map(r)) for r in range(n_group)]
    in_specs += [pl.BlockSpec((1, D_GROUP, page), page_map(r)) for r in range(n_group)]
    in_specs += [pl.BlockSpec((1, D_GROUP, page), page_map(r)) for r in range(n_group)]
    grid_spec = pltpu.PrefetchScalarGridSpec(
        num_scalar_prefetch=1,
        grid=(batch, n_steps),
        in_specs=in_specs,
        out_specs=per_seq,
        scratch_shapes=[pltpu.VMEM((rows, D_GROUP), BF16), pltpu.VMEM((rows, 1), F32),
                        pltpu.VMEM((rows, 1), F32), pltpu.VMEM((rows, D_GROUP), F32),
                        pltpu.VMEM((N_HEADS, 1), F32)],
    )
    return pl.pallas_call(
        functools.partial(_fox_decode_kernel, n_group),
        grid_spec=grid_spec,
        out_shape=jax.ShapeDtypeStruct((batch * n_q, D_GROUP), F32),
        compiler_params=_params("parallel", "arbitrary"),
        name="fox_decode",
    )(page_table.reshape(-1), fq, fk, fv, cn,
      *([cache_lt] * n_group), *([cache_kt] * n_group), *([cache_vt] * n_group))


def _outproj_kernel(fox_transposed, h_ref, a_ref, b_ref, w_ref, o_ref):
    half = a_ref.shape[1]
    b = b_ref[...].astype(F32).T if fox_transposed else b_ref[...]
    mix = _dot(a_ref[...].astype(BF16), w_ref[:half, :]) + _dot(b.astype(BF16), w_ref[half:, :])
    o_ref[...] = h_ref[...] + mix


def _outproj(h, o_ret, o_fox, w_out, fox_transposed):
    t, d = h.shape
    tm = _pick_tile(t, 512, LANES if fox_transposed else 8)
    row = lambda i: (i, 0)
    fox_spec = pl.BlockSpec((D_GROUP, tm), lambda i: (0, i)) if fox_transposed else pl.BlockSpec((tm, D_GROUP), row)
    return pl.pallas_call(
        functools.partial(_outproj_kernel, fox_transposed),
        grid=(t // tm,),
        in_specs=[pl.BlockSpec((tm, d), row), pl.BlockSpec((tm, D_GROUP), row),
                  fox_spec, pl.BlockSpec(w_out.shape, lambda i: (0, 0))],
        out_specs=pl.BlockSpec((tm, d), row),
        out_shape=jax.ShapeDtypeStruct((t, d), F32),
        compiler_params=_params("parallel"),
        name="outproj",
    )(h, o_ret, o_fox, w_out)


def _ple_kernel(final, h_ref, p_ref, g_ref, wg_ref, bg_ref, wp_ref, gf_ref, o_ref):
    h = h_ref[...]
    gate = jax.nn.sigmoid(_dot(_rms(h, g_ref[...]).astype(BF16), wg_ref[...]) + bg_ref[...])
    h = h + gate * _dot(p_ref[...].astype(BF16), wp_ref[...])
    o_ref[...] = _rms(h, gf_ref[...]) if final else h


def _ple(h, p, g, w_gate, b_gate, w_proj, g_final, final):
    t, d = h.shape
    tm = _pick_tile(t, 512, 8)
    row = lambda i: (i, 0)
    fixed = lambda i: (0, 0)
    return pl.pallas_call(
        functools.partial(_ple_kernel, final),
        grid=(t // tm,),
        in_specs=[pl.BlockSpec((tm, d), row), pl.BlockSpec((tm, p.shape[1]), row),
                  pl.BlockSpec((1, d), fixed), pl.BlockSpec(w_gate.shape, fixed),
                  pl.BlockSpec((1, d), fixed), pl.BlockSpec(w_proj.shape, fixed),
                  pl.BlockSpec((1, d), fixed)],
        out_specs=pl.BlockSpec((tm, d), row),
        out_shape=jax.ShapeDtypeStruct((t, d), F32),
        compiler_params=_params("parallel"),
        name="ple_gate",
    )(h, p, g, w_gate, b_gate, w_proj, g_final)


def _rope_tables(pos):
    half = HEAD_DIM // 2
    inv = ROPE_BASE ** (-jnp.arange(half, dtype=F32) / half)
    ang = pos.astype(F32)[:, None] * inv[None, :]
    cos = jnp.cos(ang)
    sin = jnp.sin(ang)
    return jnp.tile(cos, (1, LANES // half)), jnp.tile(jnp.concatenate([-sin, sin], axis=1), (1, LANES // HEAD_DIM))


def _row(v):
    return v.reshape(1, -1)


def _layer(x, p, pos, seq, s0, past, lw, g_final, final):
    t = x.shape[0]
    batch = t // seq
    prompt = past is None
    cos_t, sin_t = _rope_tables(pos)
    h = _ffn(x, _row(lw["g_ffn1"]), lw["w1_ffn1"], lw["w3_ffn1"], lw["w2_ffn1"])
    outs = _inproj(h, _row(lw["g_mix"]), lw["w_main"], lw["w_ff"], lw["b_ff"], cos_t, sin_t, prompt)
    rq, rk, rv, rg, fk, fv, lf = outs[:7]
    o_ret, s_ret = _retention(rq, rk, rv, rg, s0, _row(lw["gn_w"]), _row(lw["gn_b"]), seq,
                              BF16 if prompt else F32)
    if prompt:
        assert batch == 1, "the prompt path handles one sequence"
        qt, kh, vt = outs[7:]
        qa, ka = _cumsum_pack(lf, qt, kh)
        o_fox = _fox_prompt(qa, ka, vt).reshape(D_GROUP, t)
    else:
        (fq,) = outs[7:]
        _, ct = _cumsum(lf, seq)
        cache_kt, cache_vt, cache_lt, page_table = past
        cn = ct[:N_HEADS].reshape(N_HEADS, batch, seq).transpose(1, 0, 2)
        cn = jnp.pad(cn, ((0, 0), (0, 0), (0, LANES - seq)))
        o_fox = _fox_decode(fq, fk, fv, cn, cache_kt, cache_vt, cache_lt, page_table, seq)
    h = _outproj(h, o_ret, o_fox, lw["w_out"], prompt)
    h = _ffn(h, _row(lw["g_ffn2"]), lw["w1_ffn2"], lw["w3_ffn2"], lw["w2_ffn2"])
    h = _ple(h, p, _row(lw["g_ple"]), lw["w_ple_gate"], _row(lw["b_ple_gate"]), lw["w_ple_proj"],
             _row(g_final), final)
    return h, fk, fv, lf[:, :N_HEADS], s_ret


def kernel(x_prompt, x_sample, cache_fox_k, cache_fox_v, cache_fox_logf, state_ret, page_table, p_prompt, p_sample, g_ffn1, w1_ffn1, w3_ffn1, w2_ffn1, g_mix, w_in, b_forget, gn_w, gn_b, w_out, g_ffn2, w1_ffn2, w3_ffn2, w2_ffn2, g_ple, w_ple_gate, b_ple_gate, w_ple_proj, g_final):
    depth = w_in.shape[0]
    batch_p, seq_p, d = x_prompt.shape
    batch_s, seq_s, _ = x_sample.shape
    n_pages = page_table.shape[1]
    page = cache_fox_k.shape[2]
    n_pool = cache_fox_k.shape[1]
    n_main = w_in.shape[2] - N_HEADS
    hp = x_prompt.reshape(batch_p * seq_p, d)
    hs = x_sample.reshape(batch_s * seq_s, d)
    pos_p = jnp.tile(jnp.arange(seq_p), batch_p)
    pos_s = jnp.tile(n_pages * page + jnp.arange(seq_s), batch_s)
    per_layer = [[] for _ in range(8)]
    for i in range(depth):
        lw = {
            "g_ffn1": g_ffn1[i], "w1_ffn1": w1_ffn1[i].astype(BF16), "w3_ffn1": w3_ffn1[i].astype(BF16),
            "w2_ffn1": w2_ffn1[i].astype(BF16), "g_mix": g_mix[i],
            "w_main": w_in[i, :, :n_main].astype(BF16),
            "w_ff": jnp.pad(w_in[i, :, n_main:], ((0, 0), (0, LANES - N_HEADS))).astype(BF16),
            "b_ff": jnp.pad(b_forget[i], (0, LANES - N_HEADS)).reshape(1, LANES),
            "gn_w": gn_w[i], "gn_b": gn_b[i], "w_out": w_out[i].astype(BF16),
            "g_ffn2": g_ffn2[i], "w1_ffn2": w1_ffn2[i].astype(BF16), "w3_ffn2": w3_ffn2[i].astype(BF16),
            "w2_ffn2": w2_ffn2[i].astype(BF16), "g_ple": g_ple[i], "w_ple_gate": w_ple_gate[i].astype(BF16),
            "b_ple_gate": b_ple_gate[i], "w_ple_proj": w_ple_proj[i].astype(BF16),
        }
        final = i == depth - 1
        s0_p = jnp.zeros((batch_p, N_HEADS, HEAD_DIM, HEAD_DIM), F32)
        hp, kp, vp, lp, sp = _layer(hp, p_prompt[i].reshape(batch_p * seq_p, -1), pos_p, seq_p, s0_p, None,
                                    lw, g_final, final)
        past = (cache_fox_k[i].transpose(0, 2, 3, 1).reshape(n_pool, D_GROUP, page),
                cache_fox_v[i].transpose(0, 2, 3, 1).reshape(n_pool, D_GROUP, page),
                cache_fox_logf[i].transpose(0, 2, 1), page_table)
        hs, ks, vs, ls, ss = _layer(hs, p_sample[i].reshape(batch_s * seq_s, -1), pos_s, seq_s, state_ret[i],
                                    past, lw, g_final, final)
        for lst, val in zip(per_layer, (
                kp.reshape(batch_p, seq_p, N_HEADS, HEAD_DIM), vp.reshape(batch_p, seq_p, N_HEADS, HEAD_DIM),
                lp.reshape(batch_p, seq_p, N_HEADS), sp,
                ks.reshape(batch_s, seq_s, N_HEADS, HEAD_DIM), vs.reshape(batch_s, seq_s, N_HEADS, HEAD_DIM),
                ls.reshape(batch_s, seq_s, N_HEADS), ss)):
            lst.append(val)
    stacked = [jnp.stack(lst) for lst in per_layer]
    return (hp.reshape(batch_p, seq_p, d), hs.reshape(batch_s, seq_s, d), *stacked)
```

```python
import functools

import numpy as np
import jax
import jax.numpy as jnp
from jax import lax
from jax.experimental import pallas as pl
from jax.experimental.pallas import tpu as pltpu

F32 = jnp.float32
BF16 = jnp.bfloat16

HEAD_DIM = 64
N_HEADS = 8
D_GROUP = N_HEADS * HEAD_DIM
RET_CHUNK = 128
ROPE_BASE = 10000.0
NORM_EPS = 1e-6
GN_EPS = 1e-5
NEG_INF = -1e30
LANES = 128
LOG2E = 1.4426950408889634
BIAS_ROWS = 16
SKIP_LOG2 = -100.0
VMEM_LIMIT = 52 * 1024 * 1024

NN_DIMS = (((1,), (0,)), ((), ()))
NT_DIMS = (((1,), (1,)), ((), ()))
TN_DIMS = (((0,), (0,)), ((), ()))


def _pick_tile(n, pref, mult):
    best = None
    for t in range(mult, min(n, pref) + 1, mult):
        if n % t == 0:
            best = t
    return best if best is not None else n


def _params(*sem):
    return pltpu.CompilerParams(dimension_semantics=sem, vmem_limit_bytes=VMEM_LIMIT)


def _rms(x, g):
    r = lax.rsqrt(jnp.mean(x * x, axis=-1, keepdims=True) + NORM_EPS)
    return x * r * g


def _dot(a, b):
    return jnp.dot(a, b, preferred_element_type=F32)


def _ffn_kernel(x_ref, g_ref, w1_ref, w3_ref, w2_ref, o_ref, xn_ref, acc_ref):
    j = pl.program_id(1)

    @pl.when(j == 0)
    def _():
        xn_ref[...] = _rms(x_ref[...], g_ref[...]).astype(BF16)
        acc_ref[...] = jnp.zeros_like(acc_ref)

    xn = xn_ref[...]
    h1 = _dot(xn, w1_ref[...])
    h3 = _dot(xn, w3_ref[...])
    a = (h1 * jax.nn.sigmoid(h1) * h3).astype(BF16)
    acc_ref[...] += _dot(a, w2_ref[...])

    @pl.when(j == pl.num_programs(1) - 1)
    def _():
        o_ref[...] = x_ref[...] + 0.5 * acc_ref[...]


def _ffn(x, g, w1, w3, w2):
    t, d = x.shape
    d_ff = w1.shape[1]
    tm = _pick_tile(t, 512, 8)
    tf = _pick_tile(d_ff, 1408, LANES)
    return pl.pallas_call(
        _ffn_kernel,
        grid=(t // tm, d_ff // tf),
        in_specs=[
            pl.BlockSpec((tm, d), lambda i, j: (i, 0)),
            pl.BlockSpec((1, d), lambda i, j: (0, 0)),
            pl.BlockSpec((d, tf), lambda i, j: (0, j)),
            pl.BlockSpec((d, tf), lambda i, j: (0, j)),
            pl.BlockSpec((tf, d), lambda i, j: (j, 0)),
        ],
        out_specs=pl.BlockSpec((tm, d), lambda i, j: (i, 0)),
        out_shape=jax.ShapeDtypeStruct((t, d), F32),
        scratch_shapes=[pltpu.VMEM((tm, d), BF16), pltpu.VMEM((tm, d), F32)],
        compiler_params=_params("parallel", "arbitrary"),
        name="ffn",
    )(x, g, w1, w3, w2)


def _inproj_kernel(head_major, h_ref, g_ref, w_ref, wff_ref, bf_ref, cos_ref, sin_ref,
                   rq_ref, rk_ref, rv_ref, rg_ref, fk_ref, fv_ref, lf_ref, *q_refs):
    tm = h_ref.shape[0]
    u = _rms(h_ref[...], g_ref[...]).astype(BF16)

    def piece(p):
        return _dot(u, w_ref[:, p * D_GROUP:(p + 1) * D_GROUP])

    cos = cos_ref[...]
    sin = sin_ref[...]
    lane = lax.broadcasted_iota(jnp.int32, (tm, LANES), 1)
    first_half = (lane % HEAD_DIM) < (HEAD_DIM // 2)

    def rope(x):
        outs = []
        for cb in range(D_GROUP // LANES):
            xb = x[:, cb * LANES:(cb + 1) * LANES]
            partner = jnp.where(first_half,
                                pltpu.roll(xb, LANES - HEAD_DIM // 2, 1),
                                pltpu.roll(xb, HEAD_DIM // 2, 1))
            outs.append(xb * cos + partner * sin)
        return jnp.concatenate(outs, axis=1)

    scale = HEAD_DIM ** -0.5
    rq_ref[...] = rope(piece(0)).astype(rq_ref.dtype)
    rk_ref[...] = (rope(piece(1)) * scale).astype(rk_ref.dtype)
    rv_ref[...] = piece(2).astype(rv_ref.dtype)
    rg_ref[...] = piece(3)
    fq = piece(4) * (scale * LOG2E if head_major else scale)
    fk = piece(5)
    fv = piece(6)
    fk_ref[...] = fk
    fv_ref[...] = fv
    if head_major:
        qt_ref, kh_ref, vt_ref = q_refs
        qt_ref[...] = fq.T.astype(BF16).reshape(N_HEADS, HEAD_DIM, tm)
        vt_ref[:, 0:HEAD_DIM, :] = fv.T.astype(BF16).reshape(N_HEADS, HEAD_DIM, tm)
        one_row = lax.broadcasted_iota(jnp.int32, (N_HEADS, BIAS_ROWS, tm), 1) == 0
        vt_ref[:, HEAD_DIM:, :] = jnp.where(one_row, 1.0, 0.0).astype(BF16)
        for h in range(N_HEADS):
            kh_ref[h] = fk[:, h * HEAD_DIM:(h + 1) * HEAD_DIM].astype(BF16)
    else:
        (fq_ref,) = q_refs
        fq_ref[...] = fq
    ff = _dot(u, wff_ref[...]) + bf_ref[...]
    lf_ref[...] = jnp.minimum(ff, 0.0) - jnp.log1p(jnp.exp(-jnp.abs(ff)))


def _inproj(h, g, w_main, w_ff, b_ff, cos_t, sin_t, head_major):
    t, d = h.shape
    tm = _pick_tile(t, 512, 8)
    act = BF16 if head_major else F32
    row = lambda i: (i, 0)
    fixed = lambda i: (0, 0)
    wide = pl.BlockSpec((tm, D_GROUP), row)
    out_shape = [jax.ShapeDtypeStruct((t, D_GROUP), act)] * 3 + [
        jax.ShapeDtypeStruct((t, D_GROUP), F32)] * 3 + [jax.ShapeDtypeStruct((t, LANES), F32)]
    out_specs = [wide] * 6 + [pl.BlockSpec((tm, LANES), row)]
    if head_major:
        transposed = lambda rows: jax.ShapeDtypeStruct((N_HEADS, rows, t), BF16)
        transposed_spec = lambda rows: pl.BlockSpec((N_HEADS, rows, tm), lambda i: (0, 0, i))
        out_shape += [transposed(HEAD_DIM), jax.ShapeDtypeStruct((N_HEADS, t, HEAD_DIM), BF16),
                      transposed(HEAD_DIM + BIAS_ROWS)]
        out_specs += [transposed_spec(HEAD_DIM), pl.BlockSpec((N_HEADS, tm, HEAD_DIM), lambda i: (0, i, 0)),
                      transposed_spec(HEAD_DIM + BIAS_ROWS)]
    else:
        out_shape += [jax.ShapeDtypeStruct((t, D_GROUP), F32)]
        out_specs += [wide]
    return pl.pallas_call(
        functools.partial(_inproj_kernel, head_major),
        grid=(t // tm,),
        in_specs=[
            pl.BlockSpec((tm, d), row),
            pl.BlockSpec((1, d), fixed),
            pl.BlockSpec(w_main.shape, fixed),
            pl.BlockSpec(w_ff.shape, fixed),
            pl.BlockSpec((1, LANES), fixed),
            pl.BlockSpec((tm, LANES), row),
            pl.BlockSpec((tm, LANES), row),
        ],
        out_specs=out_specs,
        out_shape=out_shape,
        compiler_params=_params("parallel"),
        name="inproj",
    )(h, g, w_main, w_ff, b_ff, cos_t, sin_t)


def _cumsum_kernel(seg, x_ref, c_ref, ct_ref, carry_ref):
    tb = x_ref.shape[0]
    row = lax.broadcasted_iota(jnp.int32, (tb, tb), 0)
    col = lax.broadcasted_iota(jnp.int32, (tb, tb), 1)
    keep = col <= row
    carried = seg > tb
    if not carried:
        keep = keep & ((row // seg) == (col // seg))
    tri = jnp.where(keep, 1.0, 0.0).astype(F32)
    c = jnp.dot(tri, x_ref[...], preferred_element_type=F32, precision=lax.Precision.HIGHEST)
    if carried:
        @pl.when(pl.program_id(0) == 0)
        def _():
            carry_ref[...] = jnp.zeros_like(carry_ref)

        c = c + carry_ref[...]
        carry_ref[...] = c[tb - 1:tb, :]
    c_ref[...] = c
    ct_ref[...] = c.T


def _cumsum(x, seg):
    t = x.shape[0]
    tb = _pick_tile(t, 512, LANES)
    assert seg % tb == 0 or tb % seg == 0
    return pl.pallas_call(
        functools.partial(_cumsum_kernel, seg),
        grid=(t // tb,),
        in_specs=[pl.BlockSpec((tb, LANES), lambda i: (i, 0))],
        out_specs=[pl.BlockSpec((tb, LANES), lambda i: (i, 0)),
                   pl.BlockSpec((LANES, tb), lambda i: (0, i))],
        out_shape=[jax.ShapeDtypeStruct((t, LANES), F32), jax.ShapeDtypeStruct((LANES, t), F32)],
        scratch_shapes=[pltpu.VMEM((1, LANES), F32)],
        compiler_params=_params("arbitrary"),
        name="forget_cumsum",
    )(x)


def _split3(x):
    hi = x.astype(BF16).astype(F32)
    r = x - hi
    mid = r.astype(BF16).astype(F32)
    lo = (r - mid).astype(BF16).astype(F32)
    return hi, mid, lo


def _cumsum_pack_kernel(x_ref, qt_ref, kh_ref, qa_ref, ka_ref, skip_ref, carry_ref, kn_ref, cmin_ref):
    i = pl.program_id(0)
    tb = x_ref.shape[0]
    row = lax.broadcasted_iota(jnp.int32, (tb, tb), 0)
    col = lax.broadcasted_iota(jnp.int32, (tb, tb), 1)
    tri = jnp.where(col <= row, 1.0, 0.0).astype(F32)

    @pl.when(i == 0)
    def _():
        carry_ref[...] = jnp.zeros_like(carry_ref)
        kn_ref[...] = jnp.zeros_like(kn_ref)
        cmin_ref[...] = jnp.zeros_like(cmin_ref)

    c_nat = jnp.dot(tri, x_ref[...], preferred_element_type=F32, precision=lax.Precision.HIGHEST) + carry_ref[...]
    carry_ref[...] = c_nat[tb - 1:tb, :]
    c = c_nat * LOG2E
    ct = c.T
    sub = lax.broadcasted_iota(jnp.int32, (BIAS_ROWS, tb), 0)
    lane = lax.broadcasted_iota(jnp.int32, (tb, HEAD_DIM), 1)
    head_lane = lax.broadcasted_iota(jnp.int32, (1, LANES), 1)
    kn = jnp.zeros((1, LANES), F32)
    qn = jnp.zeros((1, LANES), F32)
    for h in range(N_HEADS):
        hi, mid, lo = _split3(ct[h:h + 1, :])
        extra = jnp.where(sub == 0, hi, jnp.where(sub == 1, mid, jnp.where(sub == 2, lo,
                                                                          jnp.where(sub < 6, 1.0, 0.0))))
        qa_ref[h, 0:HEAD_DIM, :] = qt_ref[h]
        qa_ref[h, HEAD_DIM:HEAD_DIM + BIAS_ROWS, :] = extra.astype(BF16)
        qa_ref[h, HEAD_DIM + BIAS_ROWS:, :] = jnp.zeros((LANES - HEAD_DIM - BIAS_ROWS, tb), BF16)
        hi, mid, lo = _split3(c[:, h:h + 1])
        extra = jnp.where(lane < 3, 1.0, jnp.where(lane == 3, -hi, jnp.where(lane == 4, -mid,
                                                                             jnp.where(lane == 5, -lo, 0.0))))
        ka_ref[h, :, 0:HEAD_DIM] = kh_ref[h]
        ka_ref[h, :, HEAD_DIM:] = extra.astype(BF16)
        kf = kh_ref[h].astype(F32)
        qf = qt_ref[h].astype(F32)
        k_norm = jnp.sqrt(jnp.max(jnp.sum(kf * kf, axis=1, keepdims=True), axis=0, keepdims=True))
        q_norm = jnp.sqrt(jnp.max(jnp.sum(qf * qf, axis=0, keepdims=True), axis=1, keepdims=True))
        kn = jnp.where(head_lane == h, k_norm, kn)
        qn = jnp.where(head_lane == h, q_norm, qn)
    c_max = jnp.max(c, axis=0, keepdims=True)
    c_min = jnp.min(c, axis=0, keepdims=True)
    kn_ref[pl.ds(i, 1), :] = kn
    cmin_ref[pl.ds(i, 1), :] = c_min
    bound = kn_ref[...] * qn + kn * qn + c_max - cmin_ref[...]
    skip_ref[0] = jnp.where(bound <= SKIP_LOG2, 1, 0).astype(jnp.int32)


def _cumsum_pack(x, qt, kh):
    t = x.shape[0]
    tb = _pick_tile(t, 512, LANES)
    nb = t // tb
    nb_pad = -(-nb // 8) * 8
    return pl.pallas_call(
        _cumsum_pack_kernel,
        grid=(nb,),
        in_specs=[pl.BlockSpec((tb, LANES), lambda i: (i, 0)),
                  pl.BlockSpec((N_HEADS, HEAD_DIM, tb), lambda i: (0, 0, i)),
                  pl.BlockSpec((N_HEADS, tb, HEAD_DIM), lambda i: (0, i, 0))],
        out_specs=[pl.BlockSpec((N_HEADS, LANES, tb), lambda i: (0, 0, i)),
                   pl.BlockSpec((N_HEADS, tb, LANES), lambda i: (0, i, 0)),
                   pl.BlockSpec((1, nb_pad, LANES), lambda i: (i, 0, 0))],
        out_shape=[jax.ShapeDtypeStruct((N_HEADS, LANES, t), BF16),
                   jax.ShapeDtypeStruct((N_HEADS, t, LANES), BF16),
                   jax.ShapeDtypeStruct((nb, nb_pad, LANES), jnp.int32)],
        scratch_shapes=[pltpu.VMEM((1, LANES), F32), pltpu.VMEM((nb_pad, LANES), F32),
                        pltpu.VMEM((nb_pad, LANES), F32)],
        compiler_params=_params("arbitrary"),
        name="forget_cumsum_pack",
    )(x, qt, kh)


def _ret_kernel(q_ref, k_ref, v_ref, g_ref, s0_ref, intra_ref, qdec_ref, kdec_ref, cdec_ref,
                gnw_ref, gnb_ref, o_ref, sout_ref, s_ref):
    c = pl.program_id(1)

    @pl.when(c == 0)
    def _():
        s_ref[...] = s0_ref[0]

    mm = BF16 if q_ref.shape[0] >= 16 else F32
    q = q_ref[...].astype(F32)
    k = k_ref[...].astype(F32)
    qb = q.astype(mm)
    kb = k.astype(mm)
    vb = v_ref[...].astype(mm)
    qd = (q * qdec_ref[...]).astype(mm)
    kd = (k * kdec_ref[...]).astype(mm)
    outs = []
    for h in range(N_HEADS):
        sl = slice(h * HEAD_DIM, (h + 1) * HEAD_DIM)
        att = lax.dot_general(qb[:, sl], kb[:, sl], NT_DIMS, preferred_element_type=F32) * intra_ref[h]
        s_old = s_ref[h]
        o = _dot(att.astype(mm), vb[:, sl]) + _dot(qd[:, sl], s_old.astype(mm))
        s_ref[h] = s_old * cdec_ref[h] + lax.dot_general(kd[:, sl], vb[:, sl], TN_DIMS,
                                                         preferred_element_type=F32)
        mu = jnp.mean(o, axis=-1, keepdims=True)
        dev = o - mu
        var = jnp.mean(dev * dev, axis=-1, keepdims=True)
        outs.append(dev * lax.rsqrt(var + GN_EPS))
    on = jnp.concatenate(outs, axis=1)
    gate = g_ref[...]
    o_ref[...] = ((on * gnw_ref[...] + gnb_ref[...]) * (gate * jax.nn.sigmoid(gate))).astype(o_ref.dtype)

    @pl.when(c == pl.num_programs(1) - 1)
    def _():
        sout_ref[0] = s_ref[...]


def _ret_tables(chunk):
    lg = jnp.log1p(-(2.0 ** (-5.0 - jnp.arange(N_HEADS, dtype=F32))))
    i = jnp.arange(chunk, dtype=F32)
    diff = i[:, None] - i[None, :]
    intra = jnp.where(diff[None] >= 0, jnp.exp(jnp.maximum(diff, 0.0)[None] * lg[:, None, None]), 0.0)
    q_dec = jnp.exp((i[:, None] + 1.0) * lg[None, :])
    k_dec = jnp.exp((chunk - 1.0 - i)[:, None] * lg[None, :])
    c_dec = jnp.exp(chunk * lg)
    expand = lambda a: jnp.repeat(a, HEAD_DIM, axis=1)
    return intra, expand(q_dec), expand(k_dec), jnp.broadcast_to(c_dec[:, None, None], (N_HEADS, 1, HEAD_DIM))


def _retention(q, k, v, g, s0, gn_w, gn_b, seq, out_dtype):
    t = q.shape[0]
    batch = t // seq
    chunk = min(RET_CHUNK, seq)
    n = seq // chunk
    intra, q_dec, k_dec, c_dec = _ret_tables(chunk)
    blk = pl.BlockSpec((chunk, D_GROUP), lambda b, c: (b * n + c, 0))
    fixed2 = lambda b, c: (0, 0)
    fixed3 = lambda b, c: (0, 0, 0)
    state = pl.BlockSpec((1, N_HEADS, HEAD_DIM, HEAD_DIM), lambda b, c: (b, 0, 0, 0))
    return pl.pallas_call(
        _ret_kernel,
        grid=(batch, n),
        in_specs=[blk, blk, blk, blk, state,
                  pl.BlockSpec(intra.shape, fixed3),
                  pl.BlockSpec(q_dec.shape, fixed2),
                  pl.BlockSpec(k_dec.shape, fixed2),
                  pl.BlockSpec(c_dec.shape, fixed3),
                  pl.BlockSpec((1, D_GROUP), fixed2),
                  pl.BlockSpec((1, D_GROUP), fixed2)],
        out_specs=[blk, state],
        out_shape=[jax.ShapeDtypeStruct((t, D_GROUP), out_dtype),
                   jax.ShapeDtypeStruct((batch, N_HEADS, HEAD_DIM, HEAD_DIM), F32)],
        scratch_shapes=[pltpu.VMEM((N_HEADS, HEAD_DIM, HEAD_DIM), F32)],
        compiler_params=_params("parallel", "arbitrary"),
        name="retention",
    )(q, k, v, g, s0, intra, q_dec, k_dec, c_dec, gn_w, gn_b)


HEAD_GROUP = 2


def _fox_prompt_kernel(qi_ref, kj_ref, skip_ref, qa_ref, ka_ref, vt_ref, o_ref, m_ref, acc_ref):
    step = pl.program_id(0)
    i = qi_ref[step]
    j = kj_ref[step]
    tk = ka_ref.shape[1]
    tq = qa_ref.shape[2]
    n_groups = N_HEADS // HEAD_GROUP

    @pl.when(j == 0)
    def _():
        m_ref[...] = jnp.full_like(m_ref, NEG_INF)
        acc_ref[...] = jnp.zeros_like(acc_ref)

    def update(h, causal):
        s = _dot(ka_ref[h], qa_ref[h])
        if causal is not None:
            s = jnp.where(causal, s, NEG_INF)
        m_prev = m_ref[h]
        m_new = jnp.maximum(m_prev, jnp.max(s, axis=0, keepdims=True))
        p = jnp.exp2(s - m_new).astype(BF16)
        acc_ref[h] = jnp.exp2(m_prev - m_new) * acc_ref[h] + _dot(vt_ref[h], p)
        m_ref[h] = m_new

    @pl.when(j < i)
    def _():
        for grp in range(n_groups):
            @pl.when(skip_ref[step * n_groups + grp] == 0)
            def _():
                for h in range(grp * HEAD_GROUP, (grp + 1) * HEAD_GROUP):
                    update(h, None)

    @pl.when(j == i)
    def _():
        causal = (lax.broadcasted_iota(jnp.int32, (tk, tq), 0)
                  <= lax.broadcasted_iota(jnp.int32, (tk, tq), 1))
        for h in range(N_HEADS):
            update(h, causal)
            acc = acc_ref[h]
            o_ref[h] = (acc[:HEAD_DIM] / acc[HEAD_DIM:HEAD_DIM + 1]).astype(o_ref.dtype)


def _fox_prompt(qa, ka, vt, skip):
    t = ka.shape[1]
    tq = _pick_tile(t, 512, LANES)
    nq = t // tq
    qi = np.array([i for i in range(nq) for _ in range(i + 1)], np.int32)
    kj = np.array([j for i in range(nq) for j in range(i + 1)], np.int32)
    per_head = skip[qi, kj, :N_HEADS].reshape(len(qi), N_HEADS // HEAD_GROUP, HEAD_GROUP)
    group_skip = jnp.min(per_head, axis=-1).reshape(-1)
    v_rows = vt.shape[1]
    grid_spec = pltpu.PrefetchScalarGridSpec(
        num_scalar_prefetch=3,
        grid=(len(qi),),
        in_specs=[pl.BlockSpec((N_HEADS, LANES, tq), lambda s, qi, kj, sk: (0, 0, qi[s])),
                  pl.BlockSpec((N_HEADS, tq, LANES), lambda s, qi, kj, sk: (0, kj[s], 0)),
                  pl.BlockSpec((N_HEADS, v_rows, tq), lambda s, qi, kj, sk: (0, 0, kj[s]))],
        out_specs=pl.BlockSpec((N_HEADS, HEAD_DIM, tq), lambda s, qi, kj, sk: (0, 0, qi[s])),
        scratch_shapes=[pltpu.VMEM((N_HEADS, 1, tq), F32), pltpu.VMEM((N_HEADS, v_rows, tq), F32)],
    )
    return pl.pallas_call(
        _fox_prompt_kernel,
        grid_spec=grid_spec,
        out_shape=jax.ShapeDtypeStruct((N_HEADS, HEAD_DIM, t), BF16),
        compiler_params=_params("arbitrary"),
        name="fox_prompt",
    )(jnp.asarray(qi), jnp.asarray(kj), group_skip, qa, ka, vt)


def _page_suffix_kernel(x_ref, o_ref):
    n_pages, _, page = x_ref.shape
    x = x_ref[...].reshape(n_pages * N_HEADS, page)
    later = (lax.broadcasted_iota(jnp.int32, (page, page), 0)
             > lax.broadcasted_iota(jnp.int32, (page, page), 1))
    suffix = jnp.dot(x, jnp.where(later, 1.0, 0.0).astype(F32), preferred_element_type=F32,
                     precision=lax.Precision.HIGHEST)
    total = jnp.broadcast_to(jnp.sum(x, axis=1, keepdims=True), x.shape)
    o_ref[:, :N_HEADS, :] = suffix.reshape(n_pages, N_HEADS, page)
    o_ref[:, N_HEADS:, :] = total.reshape(n_pages, N_HEADS, page)


def _page_suffix(cache_lt):
    n_pool, _, page = cache_lt.shape
    pb = _pick_tile(n_pool, 128, 1)
    return pl.pallas_call(
        _page_suffix_kernel,
        grid=(n_pool // pb,),
        in_specs=[pl.BlockSpec((pb, N_HEADS, page), lambda i: (i, 0, 0))],
        out_specs=pl.BlockSpec((pb, 2 * N_HEADS, page), lambda i: (i, 0, 0)),
        out_shape=jax.ShapeDtypeStruct((n_pool, 2 * N_HEADS, page), F32),
        compiler_params=_params("parallel"),
        name="page_suffix",
    )(cache_lt)


def _fox_decode_kernel(n_group, pt_ref, q_ref, kn_ref, vn_ref, cn_ref, *refs):
    lt_refs = refs[:n_group]
    k_refs = refs[n_group:2 * n_group]
    v_refs = refs[2 * n_group:3 * n_group]
    o_ref, qbd_ref, m_ref, l_ref, acc_ref, carry_ref = refs[3 * n_group:]
    g = pl.program_id(1)
    n_q = q_ref.shape[0]
    rows = n_q * N_HEADS
    page = k_refs[0].shape[2]
    cn = cn_ref[0]

    def attend(s, v, v_dims):
        m_prev = m_ref[...]
        m_new = jnp.maximum(m_prev, jnp.max(s, axis=1, keepdims=True))
        alpha = jnp.exp(m_prev - m_new)
        p = jnp.exp(s - m_new)
        l_ref[...] = alpha * l_ref[...] + jnp.sum(p, axis=1, keepdims=True)
        acc_ref[...] = alpha * acc_ref[...] + lax.dot_general(p.astype(BF16), v, v_dims,
                                                              preferred_element_type=F32)
        m_ref[...] = m_new

    @pl.when(g == 0)
    def _():
        q = q_ref[...]
        rep = jnp.concatenate([jnp.broadcast_to(q[t:t + 1, :], (N_HEADS, D_GROUP)) for t in range(n_q)], axis=0)
        r_head = lax.broadcasted_iota(jnp.int32, (rows, D_GROUP), 0) % N_HEADS
        l_head = lax.broadcasted_iota(jnp.int32, (rows, D_GROUP), 1) // HEAD_DIM
        qbd_ref[...] = jnp.where(r_head == l_head, rep, 0.0).astype(BF16)
        m_ref[...] = jnp.full_like(m_ref, NEG_INF)
        l_ref[...] = jnp.zeros_like(l_ref)
        acc_ref[...] = jnp.zeros_like(acc_ref)
        carry_ref[...] = jnp.zeros_like(carry_ref)
        pad = jnp.zeros((page - n_q, D_GROUP), F32)
        k_new = jnp.concatenate([kn_ref[...], pad], axis=0).astype(BF16)
        v_new = jnp.concatenate([vn_ref[...], pad], axis=0).astype(BF16)
        s = lax.dot_general(qbd_ref[...], k_new, NT_DIMS, preferred_element_type=F32)
        key_pos = lax.broadcasted_iota(jnp.int32, (N_HEADS, page), 1)
        parts = []
        for t in range(n_q):
            s_t = s[t * N_HEADS:(t + 1) * N_HEADS, :] + cn[:, t:t + 1] - cn
            parts.append(jnp.where(key_pos <= t, s_t, NEG_INF))
        attend(jnp.concatenate(parts, axis=0), v_new, NN_DIMS)

    run = carry_ref[...]
    d_pages = [None] * n_group
    for r in reversed(range(n_group)):
        d_pages[r] = lt_refs[r][0, :N_HEADS, :] + run
        run = run + lt_refs[r][0, N_HEADS:, :]
    carry_ref[...] = run
    d_full = jnp.concatenate(d_pages, axis=1)
    kt_all = jnp.concatenate([r[0].astype(BF16) for r in k_refs], axis=1)
    vt_all = jnp.concatenate([r[0].astype(BF16) for r in v_refs], axis=1)
    s = _dot(qbd_ref[...], kt_all)
    parts = [s[t * N_HEADS:(t + 1) * N_HEADS, :] + d_full + cn[:, t:t + 1] for t in range(n_q)]
    attend(jnp.concatenate(parts, axis=0), vt_all, NT_DIMS)

    @pl.when(g == pl.num_programs(1) - 1)
    def _():
        o = acc_ref[...] / l_ref[...]
        r_head = lax.broadcasted_iota(jnp.int32, (rows, D_GROUP), 0) % N_HEADS
        l_head = lax.broadcasted_iota(jnp.int32, (rows, D_GROUP), 1) // HEAD_DIM
        o = jnp.where(r_head == l_head, o, 0.0)
        o_ref[...] = jnp.concatenate(
            [jnp.sum(o[t * N_HEADS:(t + 1) * N_HEADS, :], axis=0, keepdims=True) for t in range(n_q)], axis=0)


def _fox_decode(fq, fk, fv, cn, cache_kt, cache_vt, cache_lt, page_table, n_q):
    batch, n_pages = page_table.shape
    n_pool, _, page = cache_kt.shape
    n_group = _pick_tile(n_pages, 16, 1)
    n_steps = n_pages // n_group
    rows = n_q * N_HEADS

    def page_map(r):
        def index(b, g, pt):
            return (pt[b * n_pages + (n_steps - 1 - g) * n_group + r], 0, 0)
        return index

    per_seq = pl.BlockSpec((n_q, D_GROUP), lambda b, g, pt: (b, 0))
    in_specs = [per_seq, per_seq, per_seq, pl.BlockSpec((1, N_HEADS, LANES), lambda b, g, pt: (b, 0, 0))]
    in_specs += [pl.BlockSpec((1, 2 * N_HEADS, page), page_map(r)) for r in range(n_group)]
    in_specs += [pl.BlockSpec((1, D_GROUP, page), page_map(r)) for r in range(n_group)]
    in_specs += [pl.BlockSpec((1, D_GROUP, page), page_map(r)) for r in range(n_group)]
    grid_spec = pltpu.PrefetchScalarGridSpec(
        num_scalar_prefetch=1,
        grid=(batch, n_steps),
        in_specs=in_specs,
        out_specs=per_seq,
        scratch_shapes=[pltpu.VMEM((rows, D_GROUP), BF16), pltpu.VMEM((rows, 1), F32),
                        pltpu.VMEM((rows, 1), F32), pltpu.VMEM((rows, D_GROUP), F32),
                        pltpu.VMEM((N_HEADS, page), F32)],
    )
    return pl.pallas_call(
        functools.partial(_fox_decode_kernel, n_group),
        grid_spec=grid_spec,
        out_shape=jax.ShapeDtypeStruct((batch * n_q, D_GROUP), F32),
        compiler_params=_params("parallel", "arbitrary"),
        name="fox_decode",
    )(page_table.reshape(-1), fq, fk, fv, cn,
      *([cache_lt] * n_group), *([cache_kt] * n_group), *([cache_vt] * n_group))


def _outproj_kernel(fox_transposed, h_ref, a_ref, b_ref, w_ref, o_ref):
    half = a_ref.shape[1]
    b = b_ref[...].astype(F32).T if fox_transposed else b_ref[...]
    mix = _dot(a_ref[...].astype(BF16), w_ref[:half, :]) + _dot(b.astype(BF16), w_ref[half:, :])
    o_ref[...] = h_ref[...] + mix


def _outproj(h, o_ret, o_fox, w_out, fox_transposed):
    t, d = h.shape
    tm = _pick_tile(t, 512, LANES if fox_transposed else 8)
    row = lambda i: (i, 0)
    fox_spec = pl.BlockSpec((D_GROUP, tm), lambda i: (0, i)) if fox_transposed else pl.BlockSpec((tm, D_GROUP), row)
    return pl.pallas_call(
        functools.partial(_outproj_kernel, fox_transposed),
        grid=(t // tm,),
        in_specs=[pl.BlockSpec((tm, d), row), pl.BlockSpec((tm, D_GROUP), row),
                  fox_spec, pl.BlockSpec(w_out.shape, lambda i: (0, 0))],
        out_specs=pl.BlockSpec((tm, d), row),
        out_shape=jax.ShapeDtypeStruct((t, d), F32),
        compiler_params=_params("parallel"),
        name="outproj",
    )(h, o_ret, o_fox, w_out)


def _ple_kernel(final, h_ref, p_ref, g_ref, wg_ref, bg_ref, wp_ref, gf_ref, o_ref):
    h = h_ref[...]
    gate = jax.nn.sigmoid(_dot(_rms(h, g_ref[...]).astype(BF16), wg_ref[...]) + bg_ref[...])
    h = h + gate * _dot(p_ref[...].astype(BF16), wp_ref[...])
    o_ref[...] = _rms(h, gf_ref[...]) if final else h


def _ple(h, p, g, w_gate, b_gate, w_proj, g_final, final):
    t, d = h.shape
    tm = _pick_tile(t, 512, 8)
    row = lambda i: (i, 0)
    fixed = lambda i: (0, 0)
    return pl.pallas_call(
        functools.partial(_ple_kernel, final),
        grid=(t // tm,),
        in_specs=[pl.BlockSpec((tm, d), row), pl.BlockSpec((tm, p.shape[1]), row),
                  pl.BlockSpec((1, d), fixed), pl.BlockSpec(w_gate.shape, fixed),
                  pl.BlockSpec((1, d), fixed), pl.BlockSpec(w_proj.shape, fixed),
                  pl.BlockSpec((1, d), fixed)],
        out_specs=pl.BlockSpec((tm, d), row),
        out_shape=jax.ShapeDtypeStruct((t, d), F32),
        compiler_params=_params("parallel"),
        name="ple_gate",
    )(h, p, g, w_gate, b_gate, w_proj, g_final)


def _rope_tables(pos):
    half = HEAD_DIM // 2
    inv = ROPE_BASE ** (-jnp.arange(half, dtype=F32) / half)
    ang = pos.astype(F32)[:, None] * inv[None, :]
    cos = jnp.cos(ang)
    sin = jnp.sin(ang)
    return jnp.tile(cos, (1, LANES // half)), jnp.tile(jnp.concatenate([-sin, sin], axis=1), (1, LANES // HEAD_DIM))


def _row(v):
    return v.reshape(1, -1)


def _layer(x, p, pos, seq, s0, past, lw, g_final, final):
    t = x.shape[0]
    batch = t // seq
    prompt = past is None
    cos_t, sin_t = _rope_tables(pos)
    h = _ffn(x, _row(lw["g_ffn1"]), lw["w1_ffn1"], lw["w3_ffn1"], lw["w2_ffn1"])
    outs = _inproj(h, _row(lw["g_mix"]), lw["w_main"], lw["w_ff"], lw["b_ff"], cos_t, sin_t, prompt)
    rq, rk, rv, rg, fk, fv, lf = outs[:7]
    o_ret, s_ret = _retention(rq, rk, rv, rg, s0, _row(lw["gn_w"]), _row(lw["gn_b"]), seq,
                              BF16 if prompt else F32)
    if prompt:
        assert batch == 1, "the prompt path handles one sequence"
        qt, kh, vt = outs[7:]
        qa, ka, skip = _cumsum_pack(lf, qt, kh)
        o_fox = _fox_prompt(qa, ka, vt, skip).reshape(D_GROUP, t)
    else:
        (fq,) = outs[7:]
        _, ct = _cumsum(lf, seq)
        cache_kt, cache_vt, cache_lt, page_table = past
        cn = ct[:N_HEADS].reshape(N_HEADS, batch, seq).transpose(1, 0, 2)
        cn = jnp.pad(cn, ((0, 0), (0, 0), (0, LANES - seq)))
        o_fox = _fox_decode(fq, fk, fv, cn, cache_kt, cache_vt, cache_lt, page_table, seq)
    h = _outproj(h, o_ret, o_fox, lw["w_out"], prompt)
    h = _ffn(h, _row(lw["g_ffn2"]), lw["w1_ffn2"], lw["w3_ffn2"], lw["w2_ffn2"])
    h = _ple(h, p, _row(lw["g_ple"]), lw["w_ple_gate"], _row(lw["b_ple_gate"]), lw["w_ple_proj"],
             _row(g_final), final)
    return h, fk, fv, lf[:, :N_HEADS], s_ret


def kernel(x_prompt, x_sample, cache_fox_k, cache_fox_v, cache_fox_logf, state_ret, page_table, p_prompt, p_sample, g_ffn1, w1_ffn1, w3_ffn1, w2_ffn1, g_mix, w_in, b_forget, gn_w, gn_b, w_out, g_ffn2, w1_ffn2, w3_ffn2, w2_ffn2, g_ple, w_ple_gate, b_ple_gate, w_ple_proj, g_final):
    depth = w_in.shape[0]
    batch_p, seq_p, d = x_prompt.shape
    batch_s, seq_s, _ = x_sample.shape
    n_pages = page_table.shape[1]
    page = cache_fox_k.shape[2]
    n_pool = cache_fox_k.shape[1]
    n_main = w_in.shape[2] - N_HEADS
    hp = x_prompt.reshape(batch_p * seq_p, d)
    hs = x_sample.reshape(batch_s * seq_s, d)
    pos_p = jnp.tile(jnp.arange(seq_p), batch_p)
    pos_s = jnp.tile(n_pages * page + jnp.arange(seq_s), batch_s)
    per_layer = [[] for _ in range(8)]
    for i in range(depth):
        lw = {
            "g_ffn1": g_ffn1[i], "w1_ffn1": w1_ffn1[i].astype(BF16), "w3_ffn1": w3_ffn1[i].astype(BF16),
            "w2_ffn1": w2_ffn1[i].astype(BF16), "g_mix": g_mix[i],
            "w_main": w_in[i, :, :n_main].astype(BF16),
            "w_ff": jnp.pad(w_in[i, :, n_main:], ((0, 0), (0, LANES - N_HEADS))).astype(BF16),
            "b_ff": jnp.pad(b_forget[i], (0, LANES - N_HEADS)).reshape(1, LANES),
            "gn_w": gn_w[i], "gn_b": gn_b[i], "w_out": w_out[i].astype(BF16),
            "g_ffn2": g_ffn2[i], "w1_ffn2": w1_ffn2[i].astype(BF16), "w3_ffn2": w3_ffn2[i].astype(BF16),
            "w2_ffn2": w2_ffn2[i].astype(BF16), "g_ple": g_ple[i], "w_ple_gate": w_ple_gate[i].astype(BF16),
            "b_ple_gate": b_ple_gate[i], "w_ple_proj": w_ple_proj[i].astype(BF16),
        }
        final = i == depth - 1
        s0_p = jnp.zeros((batch_p, N_HEADS, HEAD_DIM, HEAD_DIM), F32)
        hp, kp, vp, lp, sp = _layer(hp, p_prompt[i].reshape(batch_p * seq_p, -1), pos_p, seq_p, s0_p, None,
                                    lw, g_final, final)
        past = (cache_fox_k[i].transpose(0, 2, 3, 1).reshape(n_pool, D_GROUP, page),
                cache_fox_v[i].transpose(0, 2, 3, 1).reshape(n_pool, D_GROUP, page),
                _page_suffix(cache_fox_logf[i].transpose(0, 2, 1)), page_table)
        hs, ks, vs, ls, ss = _layer(hs, p_sample[i].reshape(batch_s * seq_s, -1), pos_s, seq_s, state_ret[i],
                                    past, lw, g_final, final)
        for lst, val in zip(per_layer, (
                kp.reshape(batch_p, seq_p, N_HEADS, HEAD_DIM), vp.reshape(batch_p, seq_p, N_HEADS, HEAD_DIM),
                lp.reshape(batch_p, seq_p, N_HEADS), sp,
                ks.reshape(batch_s, seq_s, N_HEADS, HEAD_DIM), vs.reshape(batch_s, seq_s, N_HEADS, HEAD_DIM),
                ls.reshape(batch_s, seq_s, N_HEADS), ss)):
            lst.append(val)
    stacked = [jnp.stack(lst) for lst in per_layer]
    return (hp.reshape(batch_p, seq_p, d), hs.reshape(batch_s, seq_s, d), *stacked)
```

```python
import functools

import numpy as np
import jax
import jax.numpy as jnp
from jax import lax
from jax.experimental import pallas as pl
from jax.experimental.pallas import tpu as pltpu

F32 = jnp.float32
BF16 = jnp.bfloat16

HEAD_DIM = 64
N_HEADS = 8
D_GROUP = N_HEADS * HEAD_DIM
RET_CHUNK = 512
ROPE_BASE = 10000.0
NORM_EPS = 1e-6
GN_EPS = 1e-5
NEG_INF = -1e30
LANES = 128
LOG2E = 1.4426950408889634
BIAS_ROWS = 16
SKIP_LOG2 = -100.0
VMEM_LIMIT = 52 * 1024 * 1024

NN_DIMS = (((1,), (0,)), ((), ()))
NT_DIMS = (((1,), (1,)), ((), ()))
TN_DIMS = (((0,), (0,)), ((), ()))


def _pick_tile(n, pref, mult):
    best = None
    for t in range(mult, min(n, pref) + 1, mult):
        if n % t == 0:
            best = t
    return best if best is not None else n


def _params(*sem):
    return pltpu.CompilerParams(dimension_semantics=sem, vmem_limit_bytes=VMEM_LIMIT)


def _rms(x, g):
    r = lax.rsqrt(jnp.mean(x * x, axis=-1, keepdims=True) + NORM_EPS)
    return x * r * g


def _dot(a, b):
    return jnp.dot(a, b, preferred_element_type=F32)


def _ffn_kernel(x_ref, g_ref, w1_ref, w3_ref, w2_ref, o_ref, xn_ref, acc_ref):
    j = pl.program_id(1)

    @pl.when(j == 0)
    def _():
        xn_ref[...] = _rms(x_ref[...], g_ref[...]).astype(BF16)
        acc_ref[...] = jnp.zeros_like(acc_ref)

    xn = xn_ref[...]
    h1 = _dot(xn, w1_ref[...])
    h3 = _dot(xn, w3_ref[...])
    a = (h1 * jax.nn.sigmoid(h1) * h3).astype(BF16)
    acc_ref[...] += _dot(a, w2_ref[...])

    @pl.when(j == pl.num_programs(1) - 1)
    def _():
        o_ref[...] = x_ref[...] + 0.5 * acc_ref[...]


def _ffn(x, g, w1, w3, w2):
    t, d = x.shape
    d_ff = w1.shape[1]
    tm = _pick_tile(t, 512, 8)
    tf = _pick_tile(d_ff, 1408, LANES)
    return pl.pallas_call(
        _ffn_kernel,
        grid=(t // tm, d_ff // tf),
        in_specs=[
            pl.BlockSpec((tm, d), lambda i, j: (i, 0)),
            pl.BlockSpec((1, d), lambda i, j: (0, 0)),
            pl.BlockSpec((d, tf), lambda i, j: (0, j)),
            pl.BlockSpec((d, tf), lambda i, j: (0, j)),
            pl.BlockSpec((tf, d), lambda i, j: (j, 0)),
        ],
        out_specs=pl.BlockSpec((tm, d), lambda i, j: (i, 0)),
        out_shape=jax.ShapeDtypeStruct((t, d), F32),
        scratch_shapes=[pltpu.VMEM((tm, d), BF16), pltpu.VMEM((tm, d), F32)],
        compiler_params=_params("parallel", "arbitrary"),
        name="ffn",
    )(x, g, w1, w3, w2)


def _inproj_kernel(head_major, h_ref, g_ref, w_ref, wff_ref, bf_ref, cos_ref, sin_ref,
                   rq_ref, rk_ref, rv_ref, rg_ref, fk_ref, fv_ref, lf_ref, *q_refs):
    tm = h_ref.shape[0]
    u = _rms(h_ref[...], g_ref[...]).astype(BF16)

    def piece(p):
        return _dot(u, w_ref[:, p * D_GROUP:(p + 1) * D_GROUP])

    cos = cos_ref[...]
    sin = sin_ref[...]
    lane = lax.broadcasted_iota(jnp.int32, (tm, LANES), 1)
    first_half = (lane % HEAD_DIM) < (HEAD_DIM // 2)

    def rope(x):
        outs = []
        for cb in range(D_GROUP // LANES):
            xb = x[:, cb * LANES:(cb + 1) * LANES]
            partner = jnp.where(first_half,
                                pltpu.roll(xb, LANES - HEAD_DIM // 2, 1),
                                pltpu.roll(xb, HEAD_DIM // 2, 1))
            outs.append(xb * cos + partner * sin)
        return jnp.concatenate(outs, axis=1)

    scale = HEAD_DIM ** -0.5
    rq_ref[...] = rope(piece(0)).astype(rq_ref.dtype)
    rk_ref[...] = (rope(piece(1)) * scale).astype(rk_ref.dtype)
    rv_ref[...] = piece(2).astype(rv_ref.dtype)
    rg_ref[...] = piece(3)
    fq = piece(4) * (scale * LOG2E if head_major else scale)
    fk = piece(5)
    fv = piece(6)
    fk_ref[...] = fk
    fv_ref[...] = fv
    if head_major:
        qt_ref, kh_ref, vt_ref = q_refs
        qt_ref[...] = fq.T.astype(BF16).reshape(N_HEADS, HEAD_DIM, tm)
        vt_ref[:, 0:HEAD_DIM, :] = fv.T.astype(BF16).reshape(N_HEADS, HEAD_DIM, tm)
        one_row = lax.broadcasted_iota(jnp.int32, (N_HEADS, BIAS_ROWS, tm), 1) == 0
        vt_ref[:, HEAD_DIM:, :] = jnp.where(one_row, 1.0, 0.0).astype(BF16)
        for h in range(N_HEADS):
            kh_ref[h] = fk[:, h * HEAD_DIM:(h + 1) * HEAD_DIM].astype(BF16)
    else:
        (fq_ref,) = q_refs
        fq_ref[...] = fq
    ff = _dot(u, wff_ref[...]) + bf_ref[...]
    lf_ref[...] = jnp.minimum(ff, 0.0) - jnp.log1p(jnp.exp(-jnp.abs(ff)))


def _inproj(h, g, w_main, w_ff, b_ff, cos_t, sin_t, head_major):
    t, d = h.shape
    tm = _pick_tile(t, 512, 8)
    act = BF16 if head_major else F32
    row = lambda i: (i, 0)
    fixed = lambda i: (0, 0)
    wide = pl.BlockSpec((tm, D_GROUP), row)
    out_shape = [jax.ShapeDtypeStruct((t, D_GROUP), act)] * 3 + [
        jax.ShapeDtypeStruct((t, D_GROUP), F32)] * 3 + [jax.ShapeDtypeStruct((t, LANES), F32)]
    out_specs = [wide] * 6 + [pl.BlockSpec((tm, LANES), row)]
    if head_major:
        transposed = lambda rows: jax.ShapeDtypeStruct((N_HEADS, rows, t), BF16)
        transposed_spec = lambda rows: pl.BlockSpec((N_HEADS, rows, tm), lambda i: (0, 0, i))
        out_shape += [transposed(HEAD_DIM), jax.ShapeDtypeStruct((N_HEADS, t, HEAD_DIM), BF16),
                      transposed(HEAD_DIM + BIAS_ROWS)]
        out_specs += [transposed_spec(HEAD_DIM), pl.BlockSpec((N_HEADS, tm, HEAD_DIM), lambda i: (0, i, 0)),
                      transposed_spec(HEAD_DIM + BIAS_ROWS)]
    else:
        out_shape += [jax.ShapeDtypeStruct((t, D_GROUP), F32)]
        out_specs += [wide]
    return pl.pallas_call(
        functools.partial(_inproj_kernel, head_major),
        grid=(t // tm,),
        in_specs=[
            pl.BlockSpec((tm, d), row),
            pl.BlockSpec((1, d), fixed),
            pl.BlockSpec(w_main.shape, fixed),
            pl.BlockSpec(w_ff.shape, fixed),
            pl.BlockSpec((1, LANES), fixed),
            pl.BlockSpec((tm, LANES), row),
            pl.BlockSpec((tm, LANES), row),
        ],
        out_specs=out_specs,
        out_shape=out_shape,
        compiler_params=_params("parallel"),
        name="inproj",
    )(h, g, w_main, w_ff, b_ff, cos_t, sin_t)


def _cumsum_kernel(seg, x_ref, c_ref, ct_ref, carry_ref):
    tb = x_ref.shape[0]
    row = lax.broadcasted_iota(jnp.int32, (tb, tb), 0)
    col = lax.broadcasted_iota(jnp.int32, (tb, tb), 1)
    keep = col <= row
    carried = seg > tb
    if not carried:
        keep = keep & ((row // seg) == (col // seg))
    tri = jnp.where(keep, 1.0, 0.0).astype(F32)
    c = jnp.dot(tri, x_ref[...], preferred_element_type=F32, precision=lax.Precision.HIGHEST)
    if carried:
        @pl.when(pl.program_id(0) == 0)
        def _():
            carry_ref[...] = jnp.zeros_like(carry_ref)

        c = c + carry_ref[...]
        carry_ref[...] = c[tb - 1:tb, :]
    c_ref[...] = c
    ct_ref[...] = c.T


def _cumsum(x, seg):
    t = x.shape[0]
    tb = _pick_tile(t, 512, LANES)
    assert seg % tb == 0 or tb % seg == 0
    return pl.pallas_call(
        functools.partial(_cumsum_kernel, seg),
        grid=(t // tb,),
        in_specs=[pl.BlockSpec((tb, LANES), lambda i: (i, 0))],
        out_specs=[pl.BlockSpec((tb, LANES), lambda i: (i, 0)),
                   pl.BlockSpec((LANES, tb), lambda i: (0, i))],
        out_shape=[jax.ShapeDtypeStruct((t, LANES), F32), jax.ShapeDtypeStruct((LANES, t), F32)],
        scratch_shapes=[pltpu.VMEM((1, LANES), F32)],
        compiler_params=_params("arbitrary"),
        name="forget_cumsum",
    )(x)


def _split3(x):
    hi = x.astype(BF16).astype(F32)
    r = x - hi
    mid = r.astype(BF16).astype(F32)
    lo = (r - mid).astype(BF16).astype(F32)
    return hi, mid, lo


def _cumsum_pack_kernel(x_ref, qt_ref, kh_ref, qa_ref, ka_ref, skip_ref, carry_ref, kn_ref, cmin_ref):
    i = pl.program_id(0)
    tb = x_ref.shape[0]
    row = lax.broadcasted_iota(jnp.int32, (tb, tb), 0)
    col = lax.broadcasted_iota(jnp.int32, (tb, tb), 1)
    tri = jnp.where(col <= row, 1.0, 0.0).astype(F32)

    @pl.when(i == 0)
    def _():
        carry_ref[...] = jnp.zeros_like(carry_ref)
        kn_ref[...] = jnp.zeros_like(kn_ref)
        cmin_ref[...] = jnp.zeros_like(cmin_ref)

    c_nat = jnp.dot(tri, x_ref[...], preferred_element_type=F32, precision=lax.Precision.HIGHEST) + carry_ref[...]
    carry_ref[...] = c_nat[tb - 1:tb, :]
    c = c_nat * LOG2E
    ct = c.T
    sub = lax.broadcasted_iota(jnp.int32, (BIAS_ROWS, tb), 0)
    lane = lax.broadcasted_iota(jnp.int32, (tb, HEAD_DIM), 1)
    head_lane = lax.broadcasted_iota(jnp.int32, (1, LANES), 1)
    kn = jnp.zeros((1, LANES), F32)
    qn = jnp.zeros((1, LANES), F32)
    for h in range(N_HEADS):
        hi, mid, lo = _split3(ct[h:h + 1, :])
        extra = jnp.where(sub == 0, hi, jnp.where(sub == 1, mid, jnp.where(sub == 2, lo,
                                                                          jnp.where(sub < 6, 1.0, 0.0))))
        qa_ref[h, 0:HEAD_DIM, :] = qt_ref[h]
        qa_ref[h, HEAD_DIM:HEAD_DIM + BIAS_ROWS, :] = extra.astype(BF16)
        qa_ref[h, HEAD_DIM + BIAS_ROWS:, :] = jnp.zeros((LANES - HEAD_DIM - BIAS_ROWS, tb), BF16)
        hi, mid, lo = _split3(c[:, h:h + 1])
        extra = jnp.where(lane < 3, 1.0, jnp.where(lane == 3, -hi, jnp.where(lane == 4, -mid,
                                                                             jnp.where(lane == 5, -lo, 0.0))))
        ka_ref[h, :, 0:HEAD_DIM] = kh_ref[h]
        ka_ref[h, :, HEAD_DIM:] = extra.astype(BF16)
        kf = kh_ref[h].astype(F32)
        qf = qt_ref[h].astype(F32)
        k_norm = jnp.sqrt(jnp.max(jnp.sum(kf * kf, axis=1, keepdims=True), axis=0, keepdims=True))
        q_norm = jnp.sqrt(jnp.max(jnp.sum(qf * qf, axis=0, keepdims=True), axis=1, keepdims=True))
        kn = jnp.where(head_lane == h, k_norm, kn)
        qn = jnp.where(head_lane == h, q_norm, qn)
    c_max = jnp.max(c, axis=0, keepdims=True)
    c_min = jnp.min(c, axis=0, keepdims=True)
    kn_ref[pl.ds(i, 1), :] = kn
    cmin_ref[pl.ds(i, 1), :] = c_min
    bound = kn_ref[...] * qn + kn * qn + c_max - cmin_ref[...]
    skip_ref[0] = jnp.where(bound <= SKIP_LOG2, 1, 0).astype(jnp.int32)


def _cumsum_pack(x, qt, kh):
    t = x.shape[0]
    tb = _pick_tile(t, 512, LANES)
    nb = t // tb
    nb_pad = -(-nb // 8) * 8
    return pl.pallas_call(
        _cumsum_pack_kernel,
        grid=(nb,),
        in_specs=[pl.BlockSpec((tb, LANES), lambda i: (i, 0)),
                  pl.BlockSpec((N_HEADS, HEAD_DIM, tb), lambda i: (0, 0, i)),
                  pl.BlockSpec((N_HEADS, tb, HEAD_DIM), lambda i: (0, i, 0))],
        out_specs=[pl.BlockSpec((N_HEADS, LANES, tb), lambda i: (0, 0, i)),
                   pl.BlockSpec((N_HEADS, tb, LANES), lambda i: (0, i, 0)),
                   pl.BlockSpec((1, nb_pad, LANES), lambda i: (i, 0, 0))],
        out_shape=[jax.ShapeDtypeStruct((N_HEADS, LANES, t), BF16),
                   jax.ShapeDtypeStruct((N_HEADS, t, LANES), BF16),
                   jax.ShapeDtypeStruct((nb, nb_pad, LANES), jnp.int32)],
        scratch_shapes=[pltpu.VMEM((1, LANES), F32), pltpu.VMEM((nb_pad, LANES), F32),
                        pltpu.VMEM((nb_pad, LANES), F32)],
        compiler_params=_params("arbitrary"),
        name="forget_cumsum_pack",
    )(x, qt, kh)


def _ret_kernel(q_ref, k_ref, v_ref, g_ref, s0_ref, intra_ref, qdec_ref, kdec_ref, cdec_ref,
                gnw_ref, gnb_ref, o_ref, sout_ref, s_ref):
    c = pl.program_id(1)

    @pl.when(c == 0)
    def _():
        s_ref[...] = s0_ref[0]

    mm = BF16 if q_ref.shape[0] >= 16 else F32
    q = q_ref[...].astype(F32)
    k = k_ref[...].astype(F32)
    qb = q.astype(mm)
    kb = k.astype(mm)
    vb = v_ref[...].astype(mm)
    qd = (q * qdec_ref[...]).astype(mm)
    kd = (k * kdec_ref[...]).astype(mm)
    outs = []
    for h in range(N_HEADS):
        sl = slice(h * HEAD_DIM, (h + 1) * HEAD_DIM)
        att = lax.dot_general(qb[:, sl], kb[:, sl], NT_DIMS, preferred_element_type=F32) * intra_ref[h]
        s_old = s_ref[h]
        o = _dot(att.astype(mm), vb[:, sl]) + _dot(qd[:, sl], s_old.astype(mm))
        s_ref[h] = s_old * cdec_ref[h] + lax.dot_general(kd[:, sl], vb[:, sl], TN_DIMS,
                                                         preferred_element_type=F32)
        mu = jnp.mean(o, axis=-1, keepdims=True)
        dev = o - mu
        var = jnp.mean(dev * dev, axis=-1, keepdims=True)
        outs.append(dev * lax.rsqrt(var + GN_EPS))
    on = jnp.concatenate(outs, axis=1)
    gate = g_ref[...]
    o_ref[...] = ((on * gnw_ref[...] + gnb_ref[...]) * (gate * jax.nn.sigmoid(gate))).astype(o_ref.dtype)

    @pl.when(c == pl.num_programs(1) - 1)
    def _():
        sout_ref[0] = s_ref[...]


def _ret_tables(chunk):
    lg = jnp.log1p(-(2.0 ** (-5.0 - jnp.arange(N_HEADS, dtype=F32))))
    i = jnp.arange(chunk, dtype=F32)
    diff = i[:, None] - i[None, :]
    intra = jnp.where(diff[None] >= 0, jnp.exp(jnp.maximum(diff, 0.0)[None] * lg[:, None, None]), 0.0)
    q_dec = jnp.exp((i[:, None] + 1.0) * lg[None, :])
    k_dec = jnp.exp((chunk - 1.0 - i)[:, None] * lg[None, :])
    c_dec = jnp.exp(chunk * lg)
    expand = lambda a: jnp.repeat(a, HEAD_DIM, axis=1)
    return intra, expand(q_dec), expand(k_dec), jnp.broadcast_to(c_dec[:, None, None], (N_HEADS, 1, HEAD_DIM))


def _retention(q, k, v, g, s0, gn_w, gn_b, seq, out_dtype):
    t = q.shape[0]
    batch = t // seq
    chunk = min(RET_CHUNK, seq)
    n = seq // chunk
    intra, q_dec, k_dec, c_dec = _ret_tables(chunk)
    blk = pl.BlockSpec((chunk, D_GROUP), lambda b, c: (b * n + c, 0))
    fixed2 = lambda b, c: (0, 0)
    fixed3 = lambda b, c: (0, 0, 0)
    state = pl.BlockSpec((1, N_HEADS, HEAD_DIM, HEAD_DIM), lambda b, c: (b, 0, 0, 0))
    return pl.pallas_call(
        _ret_kernel,
        grid=(batch, n),
        in_specs=[blk, blk, blk, blk, state,
                  pl.BlockSpec(intra.shape, fixed3),
                  pl.BlockSpec(q_dec.shape, fixed2),
                  pl.BlockSpec(k_dec.shape, fixed2),
                  pl.BlockSpec(c_dec.shape, fixed3),
                  pl.BlockSpec((1, D_GROUP), fixed2),
                  pl.BlockSpec((1, D_GROUP), fixed2)],
        out_specs=[blk, state],
        out_shape=[jax.ShapeDtypeStruct((t, D_GROUP), out_dtype),
                   jax.ShapeDtypeStruct((batch, N_HEADS, HEAD_DIM, HEAD_DIM), F32)],
        scratch_shapes=[pltpu.VMEM((N_HEADS, HEAD_DIM, HEAD_DIM), F32)],
        compiler_params=_params("parallel", "arbitrary"),
        name="retention",
    )(q, k, v, g, s0, intra, q_dec, k_dec, c_dec, gn_w, gn_b)


HEAD_GROUP = 2


def _fox_prompt_kernel(n_kblocks, qi_ref, kj_ref, skip_ref, qa_ref, ka_ref, vt_ref, o_ref, m_ref, acc_ref):
    step = pl.program_id(0)
    i = qi_ref[step]
    j = kj_ref[step]
    tk = ka_ref.shape[1]
    tq = qa_ref.shape[2]
    n_groups = N_HEADS // HEAD_GROUP

    @pl.when(j == 0)
    def _():
        m_ref[...] = jnp.full_like(m_ref, NEG_INF)
        acc_ref[...] = jnp.zeros_like(acc_ref)

    def update(h, causal):
        s = _dot(ka_ref[h], qa_ref[h])
        if causal is not None:
            s = jnp.where(causal, s, NEG_INF)
        m_prev = m_ref[h]
        m_new = jnp.maximum(m_prev, jnp.max(s, axis=0, keepdims=True))
        p = jnp.exp2(s - m_new).astype(BF16)
        acc_ref[h] = jnp.exp2(m_prev - m_new) * acc_ref[h] + _dot(vt_ref[h], p)
        m_ref[h] = m_new

    @pl.when(j < i)
    def _():
        flags = (i * n_kblocks + j) * N_HEADS
        for grp in range(n_groups):
            heads = range(grp * HEAD_GROUP, (grp + 1) * HEAD_GROUP)
            skippable = functools.reduce(jnp.minimum, [skip_ref[flags + h] for h in heads])

            @pl.when(skippable == 0)
            def _():
                for h in range(grp * HEAD_GROUP, (grp + 1) * HEAD_GROUP):
                    update(h, None)

    @pl.when(j == i)
    def _():
        causal = (lax.broadcasted_iota(jnp.int32, (tk, tq), 0)
                  <= lax.broadcasted_iota(jnp.int32, (tk, tq), 1))
        for h in range(N_HEADS):
            update(h, causal)
            acc = acc_ref[h]
            o_ref[h] = (acc[:HEAD_DIM] / acc[HEAD_DIM:HEAD_DIM + 1]).astype(o_ref.dtype)


def _fox_prompt(qa, ka, vt, skip):
    t = ka.shape[1]
    tq = _pick_tile(t, 512, LANES)
    nq = t // tq
    qi = np.array([i for i in range(nq) for _ in range(i + 1)], np.int32)
    kj = np.array([j for i in range(nq) for j in range(i + 1)], np.int32)
    v_rows = vt.shape[1]
    grid_spec = pltpu.PrefetchScalarGridSpec(
        num_scalar_prefetch=3,
        grid=(len(qi),),
        in_specs=[pl.BlockSpec((N_HEADS, LANES, tq), lambda s, qi, kj, sk: (0, 0, qi[s])),
                  pl.BlockSpec((N_HEADS, tq, LANES), lambda s, qi, kj, sk: (0, kj[s], 0)),
                  pl.BlockSpec((N_HEADS, v_rows, tq), lambda s, qi, kj, sk: (0, 0, kj[s]))],
        out_specs=pl.BlockSpec((N_HEADS, HEAD_DIM, tq), lambda s, qi, kj, sk: (0, 0, qi[s])),
        scratch_shapes=[pltpu.VMEM((N_HEADS, 1, tq), F32), pltpu.VMEM((N_HEADS, v_rows, tq), F32)],
    )
    return pl.pallas_call(
        functools.partial(_fox_prompt_kernel, skip.shape[1]),
        grid_spec=grid_spec,
        out_shape=jax.ShapeDtypeStruct((N_HEADS, HEAD_DIM, t), BF16),
        compiler_params=_params("arbitrary"),
        name="fox_prompt",
    )(jnp.asarray(qi), jnp.asarray(kj), skip[:, :, :N_HEADS].reshape(-1), qa, ka, vt)


def _fox_decode_kernel(n_group, pt_ref, q_ref, kn_ref, vn_ref, cn_ref, *refs):
    lt_refs = refs[:n_group]
    k_refs = refs[n_group:2 * n_group]
    v_refs = refs[2 * n_group:3 * n_group]
    o_ref, qbd_ref, m_ref, l_ref, acc_ref, carry_ref = refs[3 * n_group:]
    g = pl.program_id(1)
    n_q = q_ref.shape[0]
    rows = n_q * N_HEADS
    page = k_refs[0].shape[2]
    cn = cn_ref[0]

    def attend(s, v, v_dims):
        m_prev = m_ref[...]
        m_new = jnp.maximum(m_prev, jnp.max(s, axis=1, keepdims=True))
        alpha = jnp.exp(m_prev - m_new)
        p = jnp.exp(s - m_new)
        l_ref[...] = alpha * l_ref[...] + jnp.sum(p, axis=1, keepdims=True)
        acc_ref[...] = alpha * acc_ref[...] + lax.dot_general(p.astype(BF16), v, v_dims,
                                                              preferred_element_type=F32)
        m_ref[...] = m_new

    @pl.when(g == 0)
    def _():
        q = q_ref[...]
        rep = jnp.concatenate([jnp.broadcast_to(q[t:t + 1, :], (N_HEADS, D_GROUP)) for t in range(n_q)], axis=0)
        r_head = lax.broadcasted_iota(jnp.int32, (rows, D_GROUP), 0) % N_HEADS
        l_head = lax.broadcasted_iota(jnp.int32, (rows, D_GROUP), 1) // HEAD_DIM
        qbd_ref[...] = jnp.where(r_head == l_head, rep, 0.0).astype(BF16)
        m_ref[...] = jnp.full_like(m_ref, NEG_INF)
        l_ref[...] = jnp.zeros_like(l_ref)
        acc_ref[...] = jnp.zeros_like(acc_ref)
        carry_ref[...] = jnp.zeros_like(carry_ref)
        pad = jnp.zeros((page - n_q, D_GROUP), F32)
        k_new = jnp.concatenate([kn_ref[...], pad], axis=0).astype(BF16)
        v_new = jnp.concatenate([vn_ref[...], pad], axis=0).astype(BF16)
        s = lax.dot_general(qbd_ref[...], k_new, NT_DIMS, preferred_element_type=F32)
        key_pos = lax.broadcasted_iota(jnp.int32, (N_HEADS, page), 1)
        parts = []
        for t in range(n_q):
            s_t = s[t * N_HEADS:(t + 1) * N_HEADS, :] + cn[:, t:t + 1] - cn
            parts.append(jnp.where(key_pos <= t, s_t, NEG_INF))
        attend(jnp.concatenate(parts, axis=0), v_new, NN_DIMS)

    x = jnp.concatenate([r[0] for r in lt_refs], axis=0)
    later = (lax.broadcasted_iota(jnp.int32, (page, page), 0)
             > lax.broadcasted_iota(jnp.int32, (page, page), 1))
    d_loc = jnp.dot(x, jnp.where(later, 1.0, 0.0).astype(F32), preferred_element_type=F32,
                    precision=lax.Precision.HIGHEST)
    tot = jnp.sum(x, axis=1, keepdims=True)
    run = carry_ref[...]
    d_pages = [None] * n_group
    for r in reversed(range(n_group)):
        d_pages[r] = d_loc[r * N_HEADS:(r + 1) * N_HEADS, :] + run
        run = run + tot[r * N_HEADS:(r + 1) * N_HEADS, :]
    carry_ref[...] = run
    d_full = jnp.concatenate(d_pages, axis=1)
    kt_all = jnp.concatenate([r[0].astype(BF16) for r in k_refs], axis=1)
    vt_all = jnp.concatenate([r[0].astype(BF16) for r in v_refs], axis=1)
    s = _dot(qbd_ref[...], kt_all)
    parts = [s[t * N_HEADS:(t + 1) * N_HEADS, :] + d_full + cn[:, t:t + 1] for t in range(n_q)]
    attend(jnp.concatenate(parts, axis=0), vt_all, NT_DIMS)

    @pl.when(g == pl.num_programs(1) - 1)
    def _():
        o = acc_ref[...] / l_ref[...]
        r_head = lax.broadcasted_iota(jnp.int32, (rows, D_GROUP), 0) % N_HEADS
        l_head = lax.broadcasted_iota(jnp.int32, (rows, D_GROUP), 1) // HEAD_DIM
        o = jnp.where(r_head == l_head, o, 0.0)
        o_ref[...] = jnp.concatenate(
            [jnp.sum(o[t * N_HEADS:(t + 1) * N_HEADS, :], axis=0, keepdims=True) for t in range(n_q)], axis=0)


def _fox_decode(fq, fk, fv, cn, cache_kt, cache_vt, cache_lt, page_table, n_q):
    batch, n_pages = page_table.shape
    n_pool, _, page = cache_kt.shape
    n_group = _pick_tile(n_pages, 16, 1)
    n_steps = n_pages // n_group
    rows = n_q * N_HEADS

    def page_map(r):
        def index(b, g, pt):
            return (pt[b * n_pages + (n_steps - 1 - g) * n_group + r], 0, 0)
        return index

    per_seq = pl.BlockSpec((n_q, D_GROUP), lambda b, g, pt: (b, 0))
    in_specs = [per_seq, per_seq, per_seq, pl.BlockSpec((1, N_HEADS, LANES), lambda b, g, pt: (b, 0, 0))]
    in_specs += [pl.BlockSpec((1, N_HEADS, page), page_map(r)) for r in range(n_group)]
    in_specs += [pl.BlockSpec((1, D_GROUP, page), page_map(r)) for r in range(n_group)]
    in_specs += [pl.BlockSpec((1, D_GROUP, page), page_map(r)) for r in range(n_group)]
    grid_spec = pltpu.PrefetchScalarGridSpec(
        num_scalar_prefetch=1,
        grid=(batch, n_steps),
        in_specs=in_specs,
        out_specs=per_seq,
        scratch_shapes=[pltpu.VMEM((rows, D_GROUP), BF16), pltpu.VMEM((rows, 1), F32),
                        pltpu.VMEM((rows, 1), F32), pltpu.VMEM((rows, D_GROUP), F32),
                        pltpu.VMEM((N_HEADS, 1), F32)],
    )
    return pl.pallas_call(
        functools.partial(_fox_decode_kernel, n_group),
        grid_spec=grid_spec,
        out_shape=jax.ShapeDtypeStruct((batch * n_q, D_GROUP), F32),
        compiler_params=_params("parallel", "arbitrary"),
        name="fox_decode",
    )(page_table.reshape(-1), fq, fk, fv, cn,
      *([cache_lt] * n_group), *([cache_kt] * n_group), *([cache_vt] * n_group))


def _outproj_kernel(fox_transposed, h_ref, a_ref, b_ref, w_ref, o_ref):
    half = a_ref.shape[1]
    b = b_ref[...].astype(F32).T if fox_transposed else b_ref[...]
    mix = _dot(a_ref[...].astype(BF16), w_ref[:half, :]) + _dot(b.astype(BF16), w_ref[half:, :])
    o_ref[...] = h_ref[...] + mix


def _outproj(h, o_ret, o_fox, w_out, fox_transposed):
    t, d = h.shape
    tm = _pick_tile(t, 512, LANES if fox_transposed else 8)
    row = lambda i: (i, 0)
    fox_spec = pl.BlockSpec((D_GROUP, tm), lambda i: (0, i)) if fox_transposed else pl.BlockSpec((tm, D_GROUP), row)
    return pl.pallas_call(
        functools.partial(_outproj_kernel, fox_transposed),
        grid=(t // tm,),
        in_specs=[pl.BlockSpec((tm, d), row), pl.BlockSpec((tm, D_GROUP), row),
                  fox_spec, pl.BlockSpec(w_out.shape, lambda i: (0, 0))],
        out_specs=pl.BlockSpec((tm, d), row),
        out_shape=jax.ShapeDtypeStruct((t, d), F32),
        compiler_params=_params("parallel"),
        name="outproj",
    )(h, o_ret, o_fox, w_out)


def _ple_kernel(final, h_ref, p_ref, g_ref, wg_ref, bg_ref, wp_ref, gf_ref, o_ref):
    h = h_ref[...]
    gate = jax.nn.sigmoid(_dot(_rms(h, g_ref[...]).astype(BF16), wg_ref[...]) + bg_ref[...])
    h = h + gate * _dot(p_ref[...].astype(BF16), wp_ref[...])
    o_ref[...] = _rms(h, gf_ref[...]) if final else h


def _ple(h, p, g, w_gate, b_gate, w_proj, g_final, final):
    t, d = h.shape
    tm = _pick_tile(t, 512, 8)
    row = lambda i: (i, 0)
    fixed = lambda i: (0, 0)
    return pl.pallas_call(
        functools.partial(_ple_kernel, final),
        grid=(t // tm,),
        in_specs=[pl.BlockSpec((tm, d), row), pl.BlockSpec((tm, p.shape[1]), row),
                  pl.BlockSpec((1, d), fixed), pl.BlockSpec(w_gate.shape, fixed),
                  pl.BlockSpec((1, d), fixed), pl.BlockSpec(w_proj.shape, fixed),
                  pl.BlockSpec((1, d), fixed)],
        out_specs=pl.BlockSpec((tm, d), row),
        out_shape=jax.ShapeDtypeStruct((t, d), F32),
        compiler_params=_params("parallel"),
        name="ple_gate",
    )(h, p, g, w_gate, b_gate, w_proj, g_final)


def _rope_tables(pos):
    half = HEAD_DIM // 2
    inv = ROPE_BASE ** (-jnp.arange(half, dtype=F32) / half)
    ang = pos.astype(F32)[:, None] * inv[None, :]
    cos = jnp.cos(ang)
    sin = jnp.sin(ang)
    return jnp.tile(cos, (1, LANES // half)), jnp.tile(jnp.concatenate([-sin, sin], axis=1), (1, LANES // HEAD_DIM))


def _row(v):
    return v.reshape(1, -1)


def _layer(x, p, pos, seq, s0, past, lw, g_final, final):
    t = x.shape[0]
    batch = t // seq
    prompt = past is None
    cos_t, sin_t = _rope_tables(pos)
    h = _ffn(x, _row(lw["g_ffn1"]), lw["w1_ffn1"], lw["w3_ffn1"], lw["w2_ffn1"])
    outs = _inproj(h, _row(lw["g_mix"]), lw["w_main"], lw["w_ff"], lw["b_ff"], cos_t, sin_t, prompt)
    rq, rk, rv, rg, fk, fv, lf = outs[:7]
    o_ret, s_ret = _retention(rq, rk, rv, rg, s0, _row(lw["gn_w"]), _row(lw["gn_b"]), seq,
                              BF16 if prompt else F32)
    if prompt:
        assert batch == 1, "the prompt path handles one sequence"
        qt, kh, vt = outs[7:]
        qa, ka, skip = _cumsum_pack(lf, qt, kh)
        o_fox = _fox_prompt(qa, ka, vt, skip).reshape(D_GROUP, t)
    else:
        (fq,) = outs[7:]
        _, ct = _cumsum(lf, seq)
        cache_kt, cache_vt, cache_lt, page_table = past
        cn = ct[:N_HEADS].reshape(N_HEADS, batch, seq).transpose(1, 0, 2)
        cn = jnp.pad(cn, ((0, 0), (0, 0), (0, LANES - seq)))
        o_fox = _fox_decode(fq, fk, fv, cn, cache_kt, cache_vt, cache_lt, page_table, seq)
    h = _outproj(h, o_ret, o_fox, lw["w_out"], prompt)
    h = _ffn(h, _row(lw["g_ffn2"]), lw["w1_ffn2"], lw["w3_ffn2"], lw["w2_ffn2"])
    h = _ple(h, p, _row(lw["g_ple"]), lw["w_ple_gate"], _row(lw["b_ple_gate"]), lw["w_ple_proj"],
             _row(g_final), final)
    return h, fk, fv, lf[:, :N_HEADS], s_ret


def kernel(x_prompt, x_sample, cache_fox_k, cache_fox_v, cache_fox_logf, state_ret, page_table, p_prompt, p_sample, g_ffn1, w1_ffn1, w3_ffn1, w2_ffn1, g_mix, w_in, b_forget, gn_w, gn_b, w_out, g_ffn2, w1_ffn2, w3_ffn2, w2_ffn2, g_ple, w_ple_gate, b_ple_gate, w_ple_proj, g_final):
    depth = w_in.shape[0]
    batch_p, seq_p, d = x_prompt.shape
    batch_s, seq_s, _ = x_sample.shape
    n_pages = page_table.shape[1]
    page = cache_fox_k.shape[2]
    n_pool = cache_fox_k.shape[1]
    n_main = w_in.shape[2] - N_HEADS
    hp = x_prompt.reshape(batch_p * seq_p, d)
    hs = x_sample.reshape(batch_s * seq_s, d)
    pos_p = jnp.tile(jnp.arange(seq_p), batch_p)
    pos_s = jnp.tile(n_pages * page + jnp.arange(seq_s), batch_s)
    per_layer = [[] for _ in range(8)]
    for i in range(depth):
        lw = {
            "g_ffn1": g_ffn1[i], "w1_ffn1": w1_ffn1[i].astype(BF16), "w3_ffn1": w3_ffn1[i].astype(BF16),
            "w2_ffn1": w2_ffn1[i].astype(BF16), "g_mix": g_mix[i],
            "w_main": w_in[i, :, :n_main].astype(BF16),
            "w_ff": jnp.pad(w_in[i, :, n_main:], ((0, 0), (0, LANES - N_HEADS))).astype(BF16),
            "b_ff": jnp.pad(b_forget[i], (0, LANES - N_HEADS)).reshape(1, LANES),
            "gn_w": gn_w[i], "gn_b": gn_b[i], "w_out": w_out[i].astype(BF16),
            "g_ffn2": g_ffn2[i], "w1_ffn2": w1_ffn2[i].astype(BF16), "w3_ffn2": w3_ffn2[i].astype(BF16),
            "w2_ffn2": w2_ffn2[i].astype(BF16), "g_ple": g_ple[i], "w_ple_gate": w_ple_gate[i].astype(BF16),
            "b_ple_gate": b_ple_gate[i], "w_ple_proj": w_ple_proj[i].astype(BF16),
        }
        final = i == depth - 1
        s0_p = jnp.zeros((batch_p, N_HEADS, HEAD_DIM, HEAD_DIM), F32)
        hp, kp, vp, lp, sp = _layer(hp, p_prompt[i].reshape(batch_p * seq_p, -1), pos_p, seq_p, s0_p, None,
                                    lw, g_final, final)
        past = (cache_fox_k[i].transpose(0, 2, 3, 1).reshape(n_pool, D_GROUP, page),
                cache_fox_v[i].transpose(0, 2, 3, 1).reshape(n_pool, D_GROUP, page),
                cache_fox_logf[i].transpose(0, 2, 1), page_table)
        hs, ks, vs, ls, ss = _layer(hs, p_sample[i].reshape(batch_s * seq_s, -1), pos_s, seq_s, state_ret[i],
                                    past, lw, g_final, final)
        for lst, val in zip(per_layer, (
                kp.reshape(batch_p, seq_p, N_HEADS, HEAD_DIM), vp.reshape(batch_p, seq_p, N_HEADS, HEAD_DIM),
                lp.reshape(batch_p, seq_p, N_HEADS), sp,
                ks.reshape(batch_s, seq_s, N_HEADS, HEAD_DIM), vs.reshape(batch_s, seq_s, N_HEADS, HEAD_DIM),
                ls.reshape(batch_s, seq_s, N_HEADS), ss)):
            lst.append(val)
    stacked = [jnp.stack(lst) for lst in per_layer]
    return (hp.reshape(batch_p, seq_p, d), hs.reshape(batch_s, seq_s, d), *stacked)
```

```python
import functools

import numpy as np
import jax
import jax.numpy as jnp
from jax import lax
from jax.experimental import pallas as pl
from jax.experimental.pallas import tpu as pltpu

F32 = jnp.float32
BF16 = jnp.bfloat16

HEAD_DIM = 64
N_HEADS = 8
D_GROUP = N_HEADS * HEAD_DIM
RET_CHUNK = 256
ROPE_BASE = 10000.0
NORM_EPS = 1e-6
GN_EPS = 1e-5
NEG_INF = -1e30
LANES = 128
LOG2E = 1.4426950408889634
BIAS_ROWS = 16
DECODE_SUBGROUPS = 2
FFN_SUBCHUNK = 512
SKIP_LOG2 = -100.0
VMEM_LIMIT = 52 * 1024 * 1024

NN_DIMS = (((1,), (0,)), ((), ()))
NT_DIMS = (((1,), (1,)), ((), ()))
TN_DIMS = (((0,), (0,)), ((), ()))


def _pick_tile(n, pref, mult):
    best = None
    for t in range(mult, min(n, pref) + 1, mult):
        if n % t == 0:
            best = t
    return best if best is not None else n


def _params(*sem):
    return pltpu.CompilerParams(dimension_semantics=sem, vmem_limit_bytes=VMEM_LIMIT)


def _rms(x, g):
    r = lax.rsqrt(jnp.mean(x * x, axis=-1, keepdims=True) + NORM_EPS)
    return x * r * g


def _dot(a, b):
    return jnp.dot(a, b, preferred_element_type=F32)


def _ffn_kernel(x_ref, g_ref, w1_ref, w3_ref, w2_ref, o_ref):
    x = x_ref[...]
    xn = _rms(x, g_ref[...]).astype(BF16)
    tf = w1_ref.shape[1]
    bounds = [(lo, min(lo + FFN_SUBCHUNK, tf)) for lo in range(0, tf, FFN_SUBCHUNK)]

    def gate_up(lo, hi):
        return _dot(xn, w1_ref[:, lo:hi]), _dot(xn, w3_ref[:, lo:hi])

    nxt = gate_up(*bounds[0])
    total = None
    for n, (lo, hi) in enumerate(bounds):
        h1, h3 = nxt
        if n + 1 < len(bounds):
            nxt = gate_up(*bounds[n + 1])
        a = (h1 * jax.nn.sigmoid(h1) * h3).astype(BF16)
        part = _dot(a, w2_ref[lo:hi, :])
        total = part if total is None else total + part
    o_ref[...] = x + 0.5 * total


def _resident(shape):
    return pl.BlockSpec(shape, lambda i: (0,) * len(shape), pipeline_mode=pl.Buffered(1))


def _ffn(x, g, w1, w3, w2):
    t, d = x.shape
    tm = _pick_tile(t, 512, 8)
    row = lambda i: (i, 0)
    return pl.pallas_call(
        _ffn_kernel,
        grid=(t // tm,),
        in_specs=[pl.BlockSpec((tm, d), row), _resident((1, d)),
                  _resident(w1.shape), _resident(w3.shape), _resident(w2.shape)],
        out_specs=pl.BlockSpec((tm, d), row),
        out_shape=jax.ShapeDtypeStruct((t, d), F32),
        compiler_params=_params("parallel"),
        name="ffn",
    )(x, g, w1, w3, w2)


def _inproj_kernel(head_major, h_ref, g_ref, w_ref, wff_ref, bf_ref, cos_ref, sin_ref,
                   rq_ref, rk_ref, rv_ref, rg_ref, fk_ref, fv_ref, lf_ref, *q_refs):
    tm = h_ref.shape[0]
    u = _rms(h_ref[...], g_ref[...]).astype(BF16)

    def piece(p):
        return _dot(u, w_ref[:, p * D_GROUP:(p + 1) * D_GROUP])

    cos = cos_ref[...]
    sin = sin_ref[...]
    lane = lax.broadcasted_iota(jnp.int32, (tm, LANES), 1)
    first_half = (lane % HEAD_DIM) < (HEAD_DIM // 2)

    def rope(x):
        outs = []
        for cb in range(D_GROUP // LANES):
            xb = x[:, cb * LANES:(cb + 1) * LANES]
            partner = jnp.where(first_half,
                                pltpu.roll(xb, LANES - HEAD_DIM // 2, 1),
                                pltpu.roll(xb, HEAD_DIM // 2, 1))
            outs.append(xb * cos + partner * sin)
        return jnp.concatenate(outs, axis=1)

    scale = HEAD_DIM ** -0.5
    rq_ref[...] = rope(piece(0)).astype(rq_ref.dtype)
    rk_ref[...] = (rope(piece(1)) * scale).astype(rk_ref.dtype)
    rv_ref[...] = piece(2).astype(rv_ref.dtype)
    rg_ref[...] = piece(3)
    fq = piece(4) * (scale * LOG2E if head_major else scale)
    fk = piece(5)
    fv = piece(6)
    fk_ref[...] = fk
    fv_ref[...] = fv
    if head_major:
        qt_ref, kh_ref, vt_ref = q_refs
        qt_ref[...] = fq.T.astype(BF16).reshape(N_HEADS, HEAD_DIM, tm)
        vt_ref[:, 0:HEAD_DIM, :] = fv.T.astype(BF16).reshape(N_HEADS, HEAD_DIM, tm)
        one_row = lax.broadcasted_iota(jnp.int32, (N_HEADS, BIAS_ROWS, tm), 1) == 0
        vt_ref[:, HEAD_DIM:, :] = jnp.where(one_row, 1.0, 0.0).astype(BF16)
        for h in range(N_HEADS):
            kh_ref[h] = fk[:, h * HEAD_DIM:(h + 1) * HEAD_DIM].astype(BF16)
    else:
        (fq_ref,) = q_refs
        fq_ref[...] = fq
    ff = _dot(u, wff_ref[...]) + bf_ref[...]
    lf_ref[...] = jnp.minimum(ff, 0.0) - jnp.log1p(jnp.exp(-jnp.abs(ff)))


def _inproj(h, g, w_main, w_ff, b_ff, cos_t, sin_t, head_major):
    t, d = h.shape
    tm = _pick_tile(t, 512, 8)
    act = BF16 if head_major else F32
    row = lambda i: (i, 0)
    fixed = lambda i: (0, 0)
    wide = pl.BlockSpec((tm, D_GROUP), row)
    out_shape = [jax.ShapeDtypeStruct((t, D_GROUP), act)] * 3 + [
        jax.ShapeDtypeStruct((t, D_GROUP), F32)] * 3 + [jax.ShapeDtypeStruct((t, LANES), F32)]
    out_specs = [wide] * 6 + [pl.BlockSpec((tm, LANES), row)]
    if head_major:
        transposed = lambda rows: jax.ShapeDtypeStruct((N_HEADS, rows, t), BF16)
        transposed_spec = lambda rows: pl.BlockSpec((N_HEADS, rows, tm), lambda i: (0, 0, i))
        out_shape += [transposed(HEAD_DIM), jax.ShapeDtypeStruct((N_HEADS, t, HEAD_DIM), BF16),
                      transposed(HEAD_DIM + BIAS_ROWS)]
        out_specs += [transposed_spec(HEAD_DIM), pl.BlockSpec((N_HEADS, tm, HEAD_DIM), lambda i: (0, i, 0)),
                      transposed_spec(HEAD_DIM + BIAS_ROWS)]
    else:
        out_shape += [jax.ShapeDtypeStruct((t, D_GROUP), F32)]
        out_specs += [wide]
    return pl.pallas_call(
        functools.partial(_inproj_kernel, head_major),
        grid=(t // tm,),
        in_specs=[
            pl.BlockSpec((tm, d), row),
            pl.BlockSpec((1, d), fixed),
            pl.BlockSpec(w_main.shape, fixed),
            pl.BlockSpec(w_ff.shape, fixed),
            pl.BlockSpec((1, LANES), fixed),
            pl.BlockSpec((tm, LANES), row),
            pl.BlockSpec((tm, LANES), row),
        ],
        out_specs=out_specs,
        out_shape=out_shape,
        compiler_params=_params("parallel"),
        name="inproj",
    )(h, g, w_main, w_ff, b_ff, cos_t, sin_t)


def _cumsum_kernel(seg, x_ref, c_ref, ct_ref, carry_ref):
    tb = x_ref.shape[0]
    row = lax.broadcasted_iota(jnp.int32, (tb, tb), 0)
    col = lax.broadcasted_iota(jnp.int32, (tb, tb), 1)
    keep = col <= row
    carried = seg > tb
    if not carried:
        keep = keep & ((row // seg) == (col // seg))
    tri = jnp.where(keep, 1.0, 0.0).astype(F32)
    c = jnp.dot(tri, x_ref[...], preferred_element_type=F32, precision=lax.Precision.HIGHEST)
    if carried:
        @pl.when(pl.program_id(0) == 0)
        def _():
            carry_ref[...] = jnp.zeros_like(carry_ref)

        c = c + carry_ref[...]
        carry_ref[...] = c[tb - 1:tb, :]
    c_ref[...] = c
    ct_ref[...] = c.T


def _cumsum(x, seg):
    t = x.shape[0]
    tb = _pick_tile(t, 512, LANES)
    assert seg % tb == 0 or tb % seg == 0
    return pl.pallas_call(
        functools.partial(_cumsum_kernel, seg),
        grid=(t // tb,),
        in_specs=[pl.BlockSpec((tb, LANES), lambda i: (i, 0))],
        out_specs=[pl.BlockSpec((tb, LANES), lambda i: (i, 0)),
                   pl.BlockSpec((LANES, tb), lambda i: (0, i))],
        out_shape=[jax.ShapeDtypeStruct((t, LANES), F32), jax.ShapeDtypeStruct((LANES, t), F32)],
        scratch_shapes=[pltpu.VMEM((1, LANES), F32)],
        compiler_params=_params("arbitrary"),
        name="forget_cumsum",
    )(x)


def _split3(x):
    hi = x.astype(BF16).astype(F32)
    r = x - hi
    mid = r.astype(BF16).astype(F32)
    lo = (r - mid).astype(BF16).astype(F32)
    return hi, mid, lo


def _cumsum_pack_kernel(x_ref, qt_ref, kh_ref, qa_ref, ka_ref, skip_ref, carry_ref, kn_ref, cmin_ref):
    i = pl.program_id(0)
    tb = x_ref.shape[0]
    row = lax.broadcasted_iota(jnp.int32, (tb, tb), 0)
    col = lax.broadcasted_iota(jnp.int32, (tb, tb), 1)
    tri = jnp.where(col <= row, 1.0, 0.0).astype(F32)

    @pl.when(i == 0)
    def _():
        carry_ref[...] = jnp.zeros_like(carry_ref)
        kn_ref[...] = jnp.zeros_like(kn_ref)
        cmin_ref[...] = jnp.zeros_like(cmin_ref)

    c_nat = jnp.dot(tri, x_ref[...], preferred_element_type=F32, precision=lax.Precision.HIGHEST) + carry_ref[...]
    carry_ref[...] = c_nat[tb - 1:tb, :]
    c = c_nat * LOG2E
    ct = c.T
    sub = lax.broadcasted_iota(jnp.int32, (BIAS_ROWS, tb), 0)
    lane = lax.broadcasted_iota(jnp.int32, (tb, HEAD_DIM), 1)
    head_lane = lax.broadcasted_iota(jnp.int32, (1, LANES), 1)
    kn = jnp.zeros((1, LANES), F32)
    qn = jnp.zeros((1, LANES), F32)
    for h in range(N_HEADS):
        hi, mid, lo = _split3(ct[h:h + 1, :])
        extra = jnp.where(sub == 0, hi, jnp.where(sub == 1, mid, jnp.where(sub == 2, lo,
                                                                          jnp.where(sub < 6, 1.0, 0.0))))
        qa_ref[h, 0:HEAD_DIM, :] = qt_ref[h]
        qa_ref[h, HEAD_DIM:HEAD_DIM + BIAS_ROWS, :] = extra.astype(BF16)
        qa_ref[h, HEAD_DIM + BIAS_ROWS:, :] = jnp.zeros((LANES - HEAD_DIM - BIAS_ROWS, tb), BF16)
        hi, mid, lo = _split3(c[:, h:h + 1])
        extra = jnp.where(lane < 3, 1.0, jnp.where(lane == 3, -hi, jnp.where(lane == 4, -mid,
                                                                             jnp.where(lane == 5, -lo, 0.0))))
        ka_ref[h, :, 0:HEAD_DIM] = kh_ref[h]
        ka_ref[h, :, HEAD_DIM:] = extra.astype(BF16)
        kf = kh_ref[h].astype(F32)
        qf = qt_ref[h].astype(F32)
        k_norm = jnp.sqrt(jnp.max(jnp.sum(kf * kf, axis=1, keepdims=True), axis=0, keepdims=True))
        q_norm = jnp.sqrt(jnp.max(jnp.sum(qf * qf, axis=0, keepdims=True), axis=1, keepdims=True))
        kn = jnp.where(head_lane == h, k_norm, kn)
        qn = jnp.where(head_lane == h, q_norm, qn)
    c_max = jnp.max(c, axis=0, keepdims=True)
    c_min = jnp.min(c, axis=0, keepdims=True)
    kn_ref[pl.ds(i, 1), :] = kn
    cmin_ref[pl.ds(i, 1), :] = c_min
    bound = kn_ref[...] * qn + kn * qn + c_max - cmin_ref[...]
    skip_ref[0] = jnp.where(bound <= SKIP_LOG2, 1, 0).astype(jnp.int32)


def _cumsum_pack(x, qt, kh):
    t = x.shape[0]
    tb = _pick_tile(t, 512, LANES)
    nb = t // tb
    nb_pad = -(-nb // 8) * 8
    return pl.pallas_call(
        _cumsum_pack_kernel,
        grid=(nb,),
        in_specs=[pl.BlockSpec((tb, LANES), lambda i: (i, 0)),
                  pl.BlockSpec((N_HEADS, HEAD_DIM, tb), lambda i: (0, 0, i)),
                  pl.BlockSpec((N_HEADS, tb, HEAD_DIM), lambda i: (0, i, 0))],
        out_specs=[pl.BlockSpec((N_HEADS, LANES, tb), lambda i: (0, 0, i)),
                   pl.BlockSpec((N_HEADS, tb, LANES), lambda i: (0, i, 0)),
                   pl.BlockSpec((1, nb_pad, LANES), lambda i: (i, 0, 0))],
        out_shape=[jax.ShapeDtypeStruct((N_HEADS, LANES, t), BF16),
                   jax.ShapeDtypeStruct((N_HEADS, t, LANES), BF16),
                   jax.ShapeDtypeStruct((nb, nb_pad, LANES), jnp.int32)],
        scratch_shapes=[pltpu.VMEM((1, LANES), F32), pltpu.VMEM((nb_pad, LANES), F32),
                        pltpu.VMEM((nb_pad, LANES), F32)],
        compiler_params=_params("arbitrary"),
        name="forget_cumsum_pack",
    )(x, qt, kh)


def _ret_kernel(q_ref, k_ref, v_ref, g_ref, s0_ref, intra_ref, qdec_ref, kdec_ref, cdec_ref,
                gnw_ref, gnb_ref, o_ref, sout_ref, s_ref):
    c = pl.program_id(1)

    @pl.when(c == 0)
    def _():
        s_ref[...] = s0_ref[0]

    mm = BF16 if q_ref.shape[0] >= 16 else F32
    q = q_ref[...].astype(F32)
    k = k_ref[...].astype(F32)
    qb = q.astype(mm)
    kb = k.astype(mm)
    vb = v_ref[...].astype(mm)
    qd = (q * qdec_ref[...]).astype(mm)
    kd = (k * kdec_ref[...]).astype(mm)
    heads = [slice(h * HEAD_DIM, (h + 1) * HEAD_DIM) for h in range(N_HEADS)]
    atts = [lax.dot_general(qb[:, sl], kb[:, sl], NT_DIMS, preferred_element_type=F32) for sl in heads]
    inters = [_dot(qd[:, sl], s_ref[h].astype(mm)) for h, sl in enumerate(heads)]
    for h, sl in enumerate(heads):
        s_ref[h] = s_ref[h] * cdec_ref[h] + lax.dot_general(kd[:, sl], vb[:, sl], TN_DIMS,
                                                            preferred_element_type=F32)
    outs = []
    for h, sl in enumerate(heads):
        o = _dot((atts[h] * intra_ref[h]).astype(mm), vb[:, sl]) + inters[h]
        mu = jnp.mean(o, axis=-1, keepdims=True)
        dev = o - mu
        var = jnp.mean(dev * dev, axis=-1, keepdims=True)
        outs.append(dev * lax.rsqrt(var + GN_EPS))
    on = jnp.concatenate(outs, axis=1)
    gate = g_ref[...]
    o_ref[...] = ((on * gnw_ref[...] + gnb_ref[...]) * (gate * jax.nn.sigmoid(gate))).astype(o_ref.dtype)

    @pl.when(c == pl.num_programs(1) - 1)
    def _():
        sout_ref[0] = s_ref[...]


def _ret_tables(chunk):
    lg = jnp.log1p(-(2.0 ** (-5.0 - jnp.arange(N_HEADS, dtype=F32))))
    i = jnp.arange(chunk, dtype=F32)
    diff = i[:, None] - i[None, :]
    intra = jnp.where(diff[None] >= 0, jnp.exp(jnp.maximum(diff, 0.0)[None] * lg[:, None, None]), 0.0)
    q_dec = jnp.exp((i[:, None] + 1.0) * lg[None, :])
    k_dec = jnp.exp((chunk - 1.0 - i)[:, None] * lg[None, :])
    c_dec = jnp.exp(chunk * lg)
    expand = lambda a: jnp.repeat(a, HEAD_DIM, axis=1)
    return intra, expand(q_dec), expand(k_dec), jnp.broadcast_to(c_dec[:, None, None], (N_HEADS, 1, HEAD_DIM))


def _retention(q, k, v, g, s0, gn_w, gn_b, seq, out_dtype):
    t = q.shape[0]
    batch = t // seq
    chunk = min(RET_CHUNK, seq)
    n = seq // chunk
    intra, q_dec, k_dec, c_dec = _ret_tables(chunk)
    blk = pl.BlockSpec((chunk, D_GROUP), lambda b, c: (b * n + c, 0))
    fixed2 = lambda b, c: (0, 0)
    fixed3 = lambda b, c: (0, 0, 0)
    state = pl.BlockSpec((1, N_HEADS, HEAD_DIM, HEAD_DIM), lambda b, c: (b, 0, 0, 0))
    return pl.pallas_call(
        _ret_kernel,
        grid=(batch, n),
        in_specs=[blk, blk, blk, blk, state,
                  pl.BlockSpec(intra.shape, fixed3),
                  pl.BlockSpec(q_dec.shape, fixed2),
                  pl.BlockSpec(k_dec.shape, fixed2),
                  pl.BlockSpec(c_dec.shape, fixed3),
                  pl.BlockSpec((1, D_GROUP), fixed2),
                  pl.BlockSpec((1, D_GROUP), fixed2)],
        out_specs=[blk, state],
        out_shape=[jax.ShapeDtypeStruct((t, D_GROUP), out_dtype),
                   jax.ShapeDtypeStruct((batch, N_HEADS, HEAD_DIM, HEAD_DIM), F32)],
        scratch_shapes=[pltpu.VMEM((N_HEADS, HEAD_DIM, HEAD_DIM), F32)],
        compiler_params=_params("parallel", "arbitrary"),
        name="retention",
    )(q, k, v, g, s0, intra, q_dec, k_dec, c_dec, gn_w, gn_b)


HEAD_GROUP = 2


def _fox_prompt_kernel(n_kblocks, qi_ref, kj_ref, skip_ref, qa_ref, ka_ref, vt_ref, o_ref, m_ref, acc_ref):
    step = pl.program_id(0)
    i = qi_ref[step]
    j = kj_ref[step]
    tk = ka_ref.shape[1]
    tq = qa_ref.shape[2]
    n_groups = N_HEADS // HEAD_GROUP

    @pl.when(j == 0)
    def _():
        m_ref[...] = jnp.full_like(m_ref, NEG_INF)
        acc_ref[...] = jnp.zeros_like(acc_ref)

    def scores(h):
        return _dot(ka_ref[h], qa_ref[h])

    def update(heads, causal):
        s_next = scores(heads[0])
        for n, h in enumerate(heads):
            s = s_next
            if n + 1 < len(heads):
                s_next = scores(heads[n + 1])
            if causal is not None:
                s = jnp.where(causal, s, NEG_INF)
            m_prev = m_ref[h]
            m_new = jnp.maximum(m_prev, jnp.max(s, axis=0, keepdims=True))
            p = jnp.exp2(s - m_new).astype(BF16)
            acc_ref[h] = jnp.exp2(m_prev - m_new) * acc_ref[h] + _dot(vt_ref[h], p)
            m_ref[h] = m_new

    @pl.when(j < i)
    def _():
        flags = (i * n_kblocks + j) * N_HEADS
        for grp in range(n_groups):
            heads = list(range(grp * HEAD_GROUP, (grp + 1) * HEAD_GROUP))
            skippable = functools.reduce(jnp.minimum, [skip_ref[flags + h] for h in heads])

            @pl.when(skippable == 0)
            def _():
                update(heads, None)

    @pl.when(j == i)
    def _():
        causal = (lax.broadcasted_iota(jnp.int32, (tk, tq), 0)
                  <= lax.broadcasted_iota(jnp.int32, (tk, tq), 1))
        update(list(range(N_HEADS)), causal)
        for h in range(N_HEADS):
            acc = acc_ref[h]
            o_ref[h] = (acc[:HEAD_DIM] / acc[HEAD_DIM:HEAD_DIM + 1]).astype(o_ref.dtype)


def _fox_prompt(qa, ka, vt, skip):
    t = ka.shape[1]
    tq = _pick_tile(t, 512, LANES)
    nq = t // tq
    qi = np.array([i for i in range(nq) for _ in range(i + 1)], np.int32)
    kj = np.array([j for i in range(nq) for j in range(i + 1)], np.int32)
    v_rows = vt.shape[1]
    grid_spec = pltpu.PrefetchScalarGridSpec(
        num_scalar_prefetch=3,
        grid=(len(qi),),
        in_specs=[pl.BlockSpec((N_HEADS, LANES, tq), lambda s, qi, kj, sk: (0, 0, qi[s])),
                  pl.BlockSpec((N_HEADS, tq, LANES), lambda s, qi, kj, sk: (0, kj[s], 0)),
                  pl.BlockSpec((N_HEADS, v_rows, tq), lambda s, qi, kj, sk: (0, 0, kj[s]))],
        out_specs=pl.BlockSpec((N_HEADS, HEAD_DIM, tq), lambda s, qi, kj, sk: (0, 0, qi[s])),
        scratch_shapes=[pltpu.VMEM((N_HEADS, 1, tq), F32), pltpu.VMEM((N_HEADS, v_rows, tq), F32)],
    )
    return pl.pallas_call(
        functools.partial(_fox_prompt_kernel, skip.shape[1]),
        grid_spec=grid_spec,
        out_shape=jax.ShapeDtypeStruct((N_HEADS, HEAD_DIM, t), BF16),
        compiler_params=_params("arbitrary"),
        name="fox_prompt",
    )(jnp.asarray(qi), jnp.asarray(kj), skip[:, :, :N_HEADS].reshape(-1), qa, ka, vt)


def _fox_decode_kernel(n_group, pt_ref, q_ref, kn_ref, vn_ref, cn_ref, *refs):
    lt_refs = refs[:n_group]
    k_refs = refs[n_group:2 * n_group]
    v_refs = refs[2 * n_group:3 * n_group]
    o_ref, qbd_ref, m_ref, l_ref, acc_ref, carry_ref = refs[3 * n_group:]
    g = pl.program_id(1)
    n_q = q_ref.shape[0]
    rows = n_q * N_HEADS
    page = k_refs[0].shape[2]
    cn = cn_ref[0]

    def attend(s, v, v_dims):
        m_prev = m_ref[...]
        m_new = jnp.maximum(m_prev, jnp.max(s, axis=1, keepdims=True))
        alpha = jnp.exp(m_prev - m_new)
        p = jnp.exp(s - m_new)
        l_ref[...] = alpha * l_ref[...] + jnp.sum(p, axis=1, keepdims=True)
        acc_ref[...] = alpha * acc_ref[...] + lax.dot_general(p.astype(BF16), v, v_dims,
                                                              preferred_element_type=F32)
        m_ref[...] = m_new

    @pl.when(g == 0)
    def _():
        q = q_ref[...]
        rep = jnp.concatenate([jnp.broadcast_to(q[t:t + 1, :], (N_HEADS, D_GROUP)) for t in range(n_q)], axis=0)
        r_head = lax.broadcasted_iota(jnp.int32, (rows, D_GROUP), 0) % N_HEADS
        l_head = lax.broadcasted_iota(jnp.int32, (rows, D_GROUP), 1) // HEAD_DIM
        qbd_ref[...] = jnp.where(r_head == l_head, rep, 0.0).astype(BF16)
        m_ref[...] = jnp.full_like(m_ref, NEG_INF)
        l_ref[...] = jnp.zeros_like(l_ref)
        acc_ref[...] = jnp.zeros_like(acc_ref)
        carry_ref[...] = jnp.zeros_like(carry_ref)
        pad = jnp.zeros((page - n_q, D_GROUP), F32)
        k_new = jnp.concatenate([kn_ref[...], pad], axis=0).astype(BF16)
        v_new = jnp.concatenate([vn_ref[...], pad], axis=0).astype(BF16)
        s = lax.dot_general(qbd_ref[...], k_new, NT_DIMS, preferred_element_type=F32)
        key_pos = lax.broadcasted_iota(jnp.int32, (N_HEADS, page), 1)
        parts = []
        for t in range(n_q):
            s_t = s[t * N_HEADS:(t + 1) * N_HEADS, :] + cn[:, t:t + 1] - cn
            parts.append(jnp.where(key_pos <= t, s_t, NEG_INF))
        attend(jnp.concatenate(parts, axis=0), v_new, NN_DIMS)

    x = jnp.concatenate([r[0] for r in lt_refs], axis=0)
    later = (lax.broadcasted_iota(jnp.int32, (page, page), 0)
             > lax.broadcasted_iota(jnp.int32, (page, page), 1))
    d_loc = jnp.dot(x, jnp.where(later, 1.0, 0.0).astype(F32), preferred_element_type=F32,
                    precision=lax.Precision.HIGHEST)
    tot = jnp.sum(x, axis=1, keepdims=True)
    run = carry_ref[...]
    d_pages = [None] * n_group
    for r in reversed(range(n_group)):
        d_pages[r] = d_loc[r * N_HEADS:(r + 1) * N_HEADS, :] + run
        run = run + tot[r * N_HEADS:(r + 1) * N_HEADS, :]
    carry_ref[...] = run
    n_sub = DECODE_SUBGROUPS if n_group % DECODE_SUBGROUPS == 0 else 1
    per_sub = n_group // n_sub
    subs = [list(range(a * per_sub, (a + 1) * per_sub)) for a in range(n_sub)]
    qbd = qbd_ref[...]

    def sub_scores(pages):
        return _dot(qbd, jnp.concatenate([k_refs[r][0].astype(BF16) for r in pages], axis=1))

    s_next = sub_scores(subs[0])
    for n, pages in enumerate(subs):
        s = s_next
        if n + 1 < n_sub:
            s_next = sub_scores(subs[n + 1])
        d_sub = jnp.concatenate([d_pages[r] for r in pages], axis=1)
        parts = [s[t * N_HEADS:(t + 1) * N_HEADS, :] + d_sub + cn[:, t:t + 1] for t in range(n_q)]
        vt_sub = jnp.concatenate([v_refs[r][0].astype(BF16) for r in pages], axis=1)
        attend(jnp.concatenate(parts, axis=0), vt_sub, NT_DIMS)

    @pl.when(g == pl.num_programs(1) - 1)
    def _():
        o = acc_ref[...] / l_ref[...]
        r_head = lax.broadcasted_iota(jnp.int32, (rows, D_GROUP), 0) % N_HEADS
        l_head = lax.broadcasted_iota(jnp.int32, (rows, D_GROUP), 1) // HEAD_DIM
        o = jnp.where(r_head == l_head, o, 0.0)
        o_ref[...] = jnp.concatenate(
            [jnp.sum(o[t * N_HEADS:(t + 1) * N_HEADS, :], axis=0, keepdims=True) for t in range(n_q)], axis=0)


def _fox_decode(fq, fk, fv, cn, cache_kt, cache_vt, cache_lt, page_table, n_q):
    batch, n_pages = page_table.shape
    n_pool, _, page = cache_kt.shape
    n_group = _pick_tile(n_pages, 16, 1)
    n_steps = n_pages // n_group
    rows = n_q * N_HEADS

    def page_map(r):
        def index(b, g, pt):
            return (pt[b * n_pages + (n_steps - 1 - g) * n_group + r], 0, 0)
        return index

    per_seq = pl.BlockSpec((n_q, D_GROUP), lambda b, g, pt: (b, 0))
    in_specs = [per_seq, per_seq, per_seq, pl.BlockSpec((1, N_HEADS, LANES), lambda b, g, pt: (b, 0, 0))]
    in_specs += [pl.BlockSpec((1, N_HEADS, page), page_map(r)) for r in range(n_group)]
    in_specs += [pl.BlockSpec((1, D_GROUP, page), page_map(r)) for r in range(n_group)]
    in_specs += [pl.BlockSpec((1, D_GROUP, page), page_map(r)) for r in range(n_group)]
    grid_spec = pltpu.PrefetchScalarGridSpec(
        num_scalar_prefetch=1,
        grid=(batch, n_steps),
        in_specs=in_specs,
        out_specs=per_seq,
        scratch_shapes=[pltpu.VMEM((rows, D_GROUP), BF16), pltpu.VMEM((rows, 1), F32),
                        pltpu.VMEM((rows, 1), F32), pltpu.VMEM((rows, D_GROUP), F32),
                        pltpu.VMEM((N_HEADS, 1), F32)],
    )
    return pl.pallas_call(
        functools.partial(_fox_decode_kernel, n_group),
        grid_spec=grid_spec,
        out_shape=jax.ShapeDtypeStruct((batch * n_q, D_GROUP), F32),
        compiler_params=_params("parallel", "arbitrary"),
        name="fox_decode",
    )(page_table.reshape(-1), fq, fk, fv, cn,
      *([cache_lt] * n_group), *([cache_kt] * n_group), *([cache_vt] * n_group))


def _outproj_kernel(fox_transposed, h_ref, a_ref, b_ref, w_ref, o_ref):
    half = a_ref.shape[1]
    b = b_ref[...].astype(F32).T if fox_transposed else b_ref[...]
    mix = _dot(a_ref[...].astype(BF16), w_ref[:half, :]) + _dot(b.astype(BF16), w_ref[half:, :])
    o_ref[...] = h_ref[...] + mix


def _outproj(h, o_ret, o_fox, w_out, fox_transposed):
    t, d = h.shape
    tm = _pick_tile(t, 512, LANES if fox_transposed else 8)
    row = lambda i: (i, 0)
    fox_spec = pl.BlockSpec((D_GROUP, tm), lambda i: (0, i)) if fox_transposed else pl.BlockSpec((tm, D_GROUP), row)
    return pl.pallas_call(
        functools.partial(_outproj_kernel, fox_transposed),
        grid=(t // tm,),
        in_specs=[pl.BlockSpec((tm, d), row), pl.BlockSpec((tm, D_GROUP), row),
                  fox_spec, pl.BlockSpec(w_out.shape, lambda i: (0, 0))],
        out_specs=pl.BlockSpec((tm, d), row),
        out_shape=jax.ShapeDtypeStruct((t, d), F32),
        compiler_params=_params("parallel"),
        name="outproj",
    )(h, o_ret, o_fox, w_out)


def _ple_kernel(final, h_ref, p_ref, g_ref, wg_ref, bg_ref, wp_ref, gf_ref, o_ref):
    h = h_ref[...]
    gate = jax.nn.sigmoid(_dot(_rms(h, g_ref[...]).astype(BF16), wg_ref[...]) + bg_ref[...])
    h = h + gate * _dot(p_ref[...].astype(BF16), wp_ref[...])
    o_ref[...] = _rms(h, gf_ref[...]) if final else h


def _ple(h, p, g, w_gate, b_gate, w_proj, g_final, final):
    t, d = h.shape
    tm = _pick_tile(t, 512, 8)
    row = lambda i: (i, 0)
    fixed = lambda i: (0, 0)
    return pl.pallas_call(
        functools.partial(_ple_kernel, final),
        grid=(t // tm,),
        in_specs=[pl.BlockSpec((tm, d), row), pl.BlockSpec((tm, p.shape[1]), row),
                  pl.BlockSpec((1, d), fixed), pl.BlockSpec(w_gate.shape, fixed),
                  pl.BlockSpec((1, d), fixed), pl.BlockSpec(w_proj.shape, fixed),
                  pl.BlockSpec((1, d), fixed)],
        out_specs=pl.BlockSpec((tm, d), row),
        out_shape=jax.ShapeDtypeStruct((t, d), F32),
        compiler_params=_params("parallel"),
        name="ple_gate",
    )(h, p, g, w_gate, b_gate, w_proj, g_final)


def _rope_tables(pos):
    half = HEAD_DIM // 2
    inv = ROPE_BASE ** (-jnp.arange(half, dtype=F32) / half)
    ang = pos.astype(F32)[:, None] * inv[None, :]
    cos = jnp.cos(ang)
    sin = jnp.sin(ang)
    return jnp.tile(cos, (1, LANES // half)), jnp.tile(jnp.concatenate([-sin, sin], axis=1), (1, LANES // HEAD_DIM))


def _row(v):
    return v.reshape(1, -1)


def _layer(x, p, pos, seq, s0, past, lw, g_final, final):
    t = x.shape[0]
    batch = t // seq
    prompt = past is None
    cos_t, sin_t = _rope_tables(pos)
    h = _ffn(x, _row(lw["g_ffn1"]), lw["w1_ffn1"], lw["w3_ffn1"], lw["w2_ffn1"])
    outs = _inproj(h, _row(lw["g_mix"]), lw["w_main"], lw["w_ff"], lw["b_ff"], cos_t, sin_t, prompt)
    rq, rk, rv, rg, fk, fv, lf = outs[:7]
    o_ret, s_ret = _retention(rq, rk, rv, rg, s0, _row(lw["gn_w"]), _row(lw["gn_b"]), seq,
                              BF16 if prompt else F32)
    if prompt:
        assert batch == 1, "the prompt path handles one sequence"
        qt, kh, vt = outs[7:]
        qa, ka, skip = _cumsum_pack(lf, qt, kh)
        o_fox = _fox_prompt(qa, ka, vt, skip).reshape(D_GROUP, t)
    else:
        (fq,) = outs[7:]
        _, ct = _cumsum(lf, seq)
        cache_kt, cache_vt, cache_lt, page_table = past
        cn = ct[:N_HEADS].reshape(N_HEADS, batch, seq).transpose(1, 0, 2)
        cn = jnp.pad(cn, ((0, 0), (0, 0), (0, LANES - seq)))
        o_fox = _fox_decode(fq, fk, fv, cn, cache_kt, cache_vt, cache_lt, page_table, seq)
    h = _outproj(h, o_ret, o_fox, lw["w_out"], prompt)
    h = _ffn(h, _row(lw["g_ffn2"]), lw["w1_ffn2"], lw["w3_ffn2"], lw["w2_ffn2"])
    h = _ple(h, p, _row(lw["g_ple"]), lw["w_ple_gate"], _row(lw["b_ple_gate"]), lw["w_ple_proj"],
             _row(g_final), final)
    return h, fk, fv, lf[:, :N_HEADS], s_ret


def kernel(x_prompt, x_sample, cache_fox_k, cache_fox_v, cache_fox_logf, state_ret, page_table, p_prompt, p_sample, g_ffn1, w1_ffn1, w3_ffn1, w2_ffn1, g_mix, w_in, b_forget, gn_w, gn_b, w_out, g_ffn2, w1_ffn2, w3_ffn2, w2_ffn2, g_ple, w_ple_gate, b_ple_gate, w_ple_proj, g_final):
    depth = w_in.shape[0]
    batch_p, seq_p, d = x_prompt.shape
    batch_s, seq_s, _ = x_sample.shape
    n_pages = page_table.shape[1]
    page = cache_fox_k.shape[2]
    n_pool = cache_fox_k.shape[1]
    n_main = w_in.shape[2] - N_HEADS
    hp = x_prompt.reshape(batch_p * seq_p, d)
    hs = x_sample.reshape(batch_s * seq_s, d)
    pos_p = jnp.tile(jnp.arange(seq_p), batch_p)
    pos_s = jnp.tile(n_pages * page + jnp.arange(seq_s), batch_s)
    per_layer = [[] for _ in range(8)]
    for i in range(depth):
        lw = {
            "g_ffn1": g_ffn1[i], "w1_ffn1": w1_ffn1[i].astype(BF16), "w3_ffn1": w3_ffn1[i].astype(BF16),
            "w2_ffn1": w2_ffn1[i].astype(BF16), "g_mix": g_mix[i],
            "w_main": w_in[i, :, :n_main].astype(BF16),
            "w_ff": jnp.pad(w_in[i, :, n_main:], ((0, 0), (0, LANES - N_HEADS))).astype(BF16),
            "b_ff": jnp.pad(b_forget[i], (0, LANES - N_HEADS)).reshape(1, LANES),
            "gn_w": gn_w[i], "gn_b": gn_b[i], "w_out": w_out[i].astype(BF16),
            "g_ffn2": g_ffn2[i], "w1_ffn2": w1_ffn2[i].astype(BF16), "w3_ffn2": w3_ffn2[i].astype(BF16),
            "w2_ffn2": w2_ffn2[i].astype(BF16), "g_ple": g_ple[i], "w_ple_gate": w_ple_gate[i].astype(BF16),
            "b_ple_gate": b_ple_gate[i], "w_ple_proj": w_ple_proj[i].astype(BF16),
        }
        final = i == depth - 1
        s0_p = jnp.zeros((batch_p, N_HEADS, HEAD_DIM, HEAD_DIM), F32)
        hp, kp, vp, lp, sp = _layer(hp, p_prompt[i].reshape(batch_p * seq_p, -1), pos_p, seq_p, s0_p, None,
                                    lw, g_final, final)
        past = (cache_fox_k[i].transpose(0, 2, 3, 1).reshape(n_pool, D_GROUP, page),
                cache_fox_v[i].transpose(0, 2, 3, 1).reshape(n_pool, D_GROUP, page),
                cache_fox_logf[i].transpose(0, 2, 1), page_table)
        hs, ks, vs, ls, ss = _layer(hs, p_sample[i].reshape(batch_s * seq_s, -1), pos_s, seq_s, state_ret[i],
                                    past, lw, g_final, final)
        for lst, val in zip(per_layer, (
                kp.reshape(batch_p, seq_p, N_HEADS, HEAD_DIM), vp.reshape(batch_p, seq_p, N_HEADS, HEAD_DIM),
                lp.reshape(batch_p, seq_p, N_HEADS), sp,
                ks.reshape(batch_s, seq_s, N_HEADS, HEAD_DIM), vs.reshape(batch_s, seq_s, N_HEADS, HEAD_DIM),
                ls.reshape(batch_s, seq_s, N_HEADS), ss)):
            lst.append(val)
    stacked = [jnp.stack(lst) for lst in per_layer]
    return (hp.reshape(batch_p, seq_p, d), hs.reshape(batch_s, seq_s, d), *stacked)
```

```python
import functools

import numpy as np
import jax
import jax.numpy as jnp
from jax import lax
from jax.experimental import pallas as pl
from jax.experimental.pallas import tpu as pltpu

F32 = jnp.float32
BF16 = jnp.bfloat16

HEAD_DIM = 64
N_HEADS = 8
D_GROUP = N_HEADS * HEAD_DIM
RET_CHUNK = 256
ROPE_BASE = 10000.0
NORM_EPS = 1e-6
GN_EPS = 1e-5
NEG_INF = -1e30
LANES = 128
LOG2E = 1.4426950408889634
BIAS_ROWS = 16
DECODE_SUBGROUPS = 2
FFN_SUBCHUNK = 512
SKIP_LOG2 = -100.0
VMEM_LIMIT = 52 * 1024 * 1024

NN_DIMS = (((1,), (0,)), ((), ()))
NT_DIMS = (((1,), (1,)), ((), ()))
TN_DIMS = (((0,), (0,)), ((), ()))


def _pick_tile(n, pref, mult):
    best = None
    for t in range(mult, min(n, pref) + 1, mult):
        if n % t == 0:
            best = t
    return best if best is not None else n


def _params(*sem):
    return pltpu.CompilerParams(dimension_semantics=sem, vmem_limit_bytes=VMEM_LIMIT)


def _rms(x, g):
    r = lax.rsqrt(jnp.mean(x * x, axis=-1, keepdims=True) + NORM_EPS)
    return x * r * g


def _dot(a, b):
    return jnp.dot(a, b, preferred_element_type=F32)


def _ffn_kernel(x_ref, g_ref, w1_ref, w3_ref, w2_ref, o_ref):
    x = x_ref[...]
    xn = _rms(x, g_ref[...]).astype(BF16)
    tf = w1_ref.shape[1]
    bounds = [(lo, min(lo + FFN_SUBCHUNK, tf)) for lo in range(0, tf, FFN_SUBCHUNK)]

    def gate_up(lo, hi):
        return _dot(xn, w1_ref[:, lo:hi]), _dot(xn, w3_ref[:, lo:hi])

    nxt = gate_up(*bounds[0])
    total = None
    for n, (lo, hi) in enumerate(bounds):
        h1, h3 = nxt
        if n + 1 < len(bounds):
            nxt = gate_up(*bounds[n + 1])
        a = (h1 * jax.nn.sigmoid(h1) * h3).astype(BF16)
        part = _dot(a, w2_ref[lo:hi, :])
        total = part if total is None else total + part
    o_ref[...] = x + 0.5 * total


def _resident(shape):
    return pl.BlockSpec(shape, lambda i: (0,) * len(shape), pipeline_mode=pl.Buffered(1))


def _ffn(x, g, w1, w3, w2):
    t, d = x.shape
    tm = _pick_tile(t, 512, 8)
    row = lambda i: (i, 0)
    return pl.pallas_call(
        _ffn_kernel,
        grid=(t // tm,),
        in_specs=[pl.BlockSpec((tm, d), row), _resident((1, d)),
                  _resident(w1.shape), _resident(w3.shape), _resident(w2.shape)],
        out_specs=pl.BlockSpec((tm, d), row),
        out_shape=jax.ShapeDtypeStruct((t, d), F32),
        compiler_params=_params("parallel"),
        name="ffn",
    )(x, g, w1, w3, w2)


def _inproj_kernel(head_major, h_ref, g_ref, w_ref, wff_ref, bf_ref, cos_ref, sin_ref,
                   rq_ref, rk_ref, rv_ref, rg_ref, fk_ref, fv_ref, lf_ref, *q_refs):
    tm = h_ref.shape[0]
    u = _rms(h_ref[...], g_ref[...]).astype(BF16)

    def piece(p):
        return _dot(u, w_ref[:, p * D_GROUP:(p + 1) * D_GROUP])

    cos = cos_ref[...]
    sin = sin_ref[...]
    lane = lax.broadcasted_iota(jnp.int32, (tm, LANES), 1)
    first_half = (lane % HEAD_DIM) < (HEAD_DIM // 2)

    def rope(x):
        outs = []
        for cb in range(D_GROUP // LANES):
            xb = x[:, cb * LANES:(cb + 1) * LANES]
            partner = jnp.where(first_half,
                                pltpu.roll(xb, LANES - HEAD_DIM // 2, 1),
                                pltpu.roll(xb, HEAD_DIM // 2, 1))
            outs.append(xb * cos + partner * sin)
        return jnp.concatenate(outs, axis=1)

    scale = HEAD_DIM ** -0.5
    rq_ref[...] = rope(piece(0)).astype(rq_ref.dtype)
    rk_ref[...] = (rope(piece(1)) * scale).astype(rk_ref.dtype)
    rv_ref[...] = piece(2).astype(rv_ref.dtype)
    rg_ref[...] = piece(3)
    fq = piece(4) * (scale * LOG2E if head_major else scale)
    fk = piece(5)
    fv = piece(6)
    fk_ref[...] = fk
    fv_ref[...] = fv
    if head_major:
        qt_ref, kh_ref, vt_ref = q_refs
        qt_ref[...] = fq.T.astype(BF16).reshape(N_HEADS, HEAD_DIM, tm)
        vt_ref[:, 0:HEAD_DIM, :] = fv.T.astype(BF16).reshape(N_HEADS, HEAD_DIM, tm)
        one_row = lax.broadcasted_iota(jnp.int32, (N_HEADS, BIAS_ROWS, tm), 1) == 0
        vt_ref[:, HEAD_DIM:, :] = jnp.where(one_row, 1.0, 0.0).astype(BF16)
        for h in range(N_HEADS):
            kh_ref[h] = fk[:, h * HEAD_DIM:(h + 1) * HEAD_DIM].astype(BF16)
    else:
        (fq_ref,) = q_refs
        fq_ref[...] = fq
    ff = _dot(u, wff_ref[...]) + bf_ref[...]
    lf_ref[...] = jnp.minimum(ff, 0.0) - jnp.log1p(jnp.exp(-jnp.abs(ff)))


def _inproj(h, g, w_main, w_ff, b_ff, cos_t, sin_t, head_major):
    t, d = h.shape
    tm = _pick_tile(t, 512, 8)
    act = BF16 if head_major else F32
    row = lambda i: (i, 0)
    fixed = lambda i: (0, 0)
    wide = pl.BlockSpec((tm, D_GROUP), row)
    out_shape = [jax.ShapeDtypeStruct((t, D_GROUP), act)] * 3 + [
        jax.ShapeDtypeStruct((t, D_GROUP), F32)] * 3 + [jax.ShapeDtypeStruct((t, LANES), F32)]
    out_specs = [wide] * 6 + [pl.BlockSpec((tm, LANES), row)]
    if head_major:
        transposed = lambda rows: jax.ShapeDtypeStruct((N_HEADS, rows, t), BF16)
        transposed_spec = lambda rows: pl.BlockSpec((N_HEADS, rows, tm), lambda i: (0, 0, i))
        out_shape += [transposed(HEAD_DIM), jax.ShapeDtypeStruct((N_HEADS, t, HEAD_DIM), BF16),
                      transposed(HEAD_DIM + BIAS_ROWS)]
        out_specs += [transposed_spec(HEAD_DIM), pl.BlockSpec((N_HEADS, tm, HEAD_DIM), lambda i: (0, i, 0)),
                      transposed_spec(HEAD_DIM + BIAS_ROWS)]
    else:
        out_shape += [jax.ShapeDtypeStruct((t, D_GROUP), F32)]
        out_specs += [wide]
    return pl.pallas_call(
        functools.partial(_inproj_kernel, head_major),
        grid=(t // tm,),
        in_specs=[
            pl.BlockSpec((tm, d), row),
            pl.BlockSpec((1, d), fixed),
            pl.BlockSpec(w_main.shape, fixed),
            pl.BlockSpec(w_ff.shape, fixed),
            pl.BlockSpec((1, LANES), fixed),
            pl.BlockSpec((tm, LANES), row),
            pl.BlockSpec((tm, LANES), row),
        ],
        out_specs=out_specs,
        out_shape=out_shape,
        compiler_params=_params("parallel"),
        name="inproj",
    )(h, g, w_main, w_ff, b_ff, cos_t, sin_t)


def _cumsum_kernel(seg, x_ref, c_ref, ct_ref, carry_ref):
    tb = x_ref.shape[0]
    row = lax.broadcasted_iota(jnp.int32, (tb, tb), 0)
    col = lax.broadcasted_iota(jnp.int32, (tb, tb), 1)
    keep = col <= row
    carried = seg > tb
    if not carried:
        keep = keep & ((row // seg) == (col // seg))
    tri = jnp.where(keep, 1.0, 0.0).astype(F32)
    c = jnp.dot(tri, x_ref[...], preferred_element_type=F32, precision=lax.Precision.HIGHEST)
    if carried:
        @pl.when(pl.program_id(0) == 0)
        def _():
            carry_ref[...] = jnp.zeros_like(carry_ref)

        c = c + carry_ref[...]
        carry_ref[...] = c[tb - 1:tb, :]
    c_ref[...] = c
    ct_ref[...] = c.T


def _cumsum(x, seg):
    t = x.shape[0]
    tb = _pick_tile(t, 512, LANES)
    assert seg % tb == 0 or tb % seg == 0
    return pl.pallas_call(
        functools.partial(_cumsum_kernel, seg),
        grid=(t // tb,),
        in_specs=[pl.BlockSpec((tb, LANES), lambda i: (i, 0))],
        out_specs=[pl.BlockSpec((tb, LANES), lambda i: (i, 0)),
                   pl.BlockSpec((LANES, tb), lambda i: (0, i))],
        out_shape=[jax.ShapeDtypeStruct((t, LANES), F32), jax.ShapeDtypeStruct((LANES, t), F32)],
        scratch_shapes=[pltpu.VMEM((1, LANES), F32)],
        compiler_params=_params("arbitrary"),
        name="forget_cumsum",
    )(x)


def _split3(x):
    hi = x.astype(BF16).astype(F32)
    r = x - hi
    mid = r.astype(BF16).astype(F32)
    lo = (r - mid).astype(BF16).astype(F32)
    return hi, mid, lo


def _cumsum_pack_kernel(x_ref, qt_ref, kh_ref, qa_ref, ka_ref, skip_ref, carry_ref, kn_ref, cmin_ref):
    i = pl.program_id(0)
    tb = x_ref.shape[0]
    row = lax.broadcasted_iota(jnp.int32, (tb, tb), 0)
    col = lax.broadcasted_iota(jnp.int32, (tb, tb), 1)
    tri = jnp.where(col <= row, 1.0, 0.0).astype(F32)

    @pl.when(i == 0)
    def _():
        carry_ref[...] = jnp.zeros_like(carry_ref)
        kn_ref[...] = jnp.zeros_like(kn_ref)
        cmin_ref[...] = jnp.zeros_like(cmin_ref)

    c_nat = jnp.dot(tri, x_ref[...], preferred_element_type=F32, precision=lax.Precision.HIGHEST) + carry_ref[...]
    carry_ref[...] = c_nat[tb - 1:tb, :]
    c = c_nat * LOG2E
    ct = c.T
    sub = lax.broadcasted_iota(jnp.int32, (BIAS_ROWS, tb), 0)
    lane = lax.broadcasted_iota(jnp.int32, (tb, HEAD_DIM), 1)
    head_lane = lax.broadcasted_iota(jnp.int32, (1, LANES), 1)
    kn = jnp.zeros((1, LANES), F32)
    qn = jnp.zeros((1, LANES), F32)
    for h in range(N_HEADS):
        hi, mid, lo = _split3(ct[h:h + 1, :])
        extra = jnp.where(sub == 0, hi, jnp.where(sub == 1, mid, jnp.where(sub == 2, lo,
                                                                          jnp.where(sub < 6, 1.0, 0.0))))
        qa_ref[h, 0:HEAD_DIM, :] = qt_ref[h]
        qa_ref[h, HEAD_DIM:HEAD_DIM + BIAS_ROWS, :] = extra.astype(BF16)
        qa_ref[h, HEAD_DIM + BIAS_ROWS:, :] = jnp.zeros((LANES - HEAD_DIM - BIAS_ROWS, tb), BF16)
        hi, mid, lo = _split3(c[:, h:h + 1])
        extra = jnp.where(lane < 3, 1.0, jnp.where(lane == 3, -hi, jnp.where(lane == 4, -mid,
                                                                             jnp.where(lane == 5, -lo, 0.0))))
        ka_ref[h, :, 0:HEAD_DIM] = kh_ref[h]
        ka_ref[h, :, HEAD_DIM:] = extra.astype(BF16)
        kf = kh_ref[h].astype(F32)
        qf = qt_ref[h].astype(F32)
        k_norm = jnp.sqrt(jnp.max(jnp.sum(kf * kf, axis=1, keepdims=True), axis=0, keepdims=True))
        q_norm = jnp.sqrt(jnp.max(jnp.sum(qf * qf, axis=0, keepdims=True), axis=1, keepdims=True))
        kn = jnp.where(head_lane == h, k_norm, kn)
        qn = jnp.where(head_lane == h, q_norm, qn)
    c_max = jnp.max(c, axis=0, keepdims=True)
    c_min = jnp.min(c, axis=0, keepdims=True)
    kn_ref[pl.ds(i, 1), :] = kn
    cmin_ref[pl.ds(i, 1), :] = c_min
    bound = kn_ref[...] * qn + kn * qn + c_max - cmin_ref[...]
    skip_ref[0] = jnp.where(bound <= SKIP_LOG2, 1, 0).astype(jnp.int32)


def _cumsum_pack(x, qt, kh):
    t = x.shape[0]
    tb = _pick_tile(t, 512, LANES)
    nb = t // tb
    nb_pad = -(-nb // 8) * 8
    return pl.pallas_call(
        _cumsum_pack_kernel,
        grid=(nb,),
        in_specs=[pl.BlockSpec((tb, LANES), lambda i: (i, 0)),
                  pl.BlockSpec((N_HEADS, HEAD_DIM, tb), lambda i: (0, 0, i)),
                  pl.BlockSpec((N_HEADS, tb, HEAD_DIM), lambda i: (0, i, 0))],
        out_specs=[pl.BlockSpec((N_HEADS, LANES, tb), lambda i: (0, 0, i)),
                   pl.BlockSpec((N_HEADS, tb, LANES), lambda i: (0, i, 0)),
                   pl.BlockSpec((1, nb_pad, LANES), lambda i: (i, 0, 0))],
        out_shape=[jax.ShapeDtypeStruct((N_HEADS, LANES, t), BF16),
                   jax.ShapeDtypeStruct((N_HEADS, t, LANES), BF16),
                   jax.ShapeDtypeStruct((nb, nb_pad, LANES), jnp.int32)],
        scratch_shapes=[pltpu.VMEM((1, LANES), F32), pltpu.VMEM((nb_pad, LANES), F32),
                        pltpu.VMEM((nb_pad, LANES), F32)],
        compiler_params=_params("arbitrary"),
        name="forget_cumsum_pack",
    )(x, qt, kh)


def _ret_kernel(q_ref, k_ref, v_ref, g_ref, s0_ref, intra_ref, qdec_ref, kdec_ref, cdec_ref,
                gnw_ref, gnb_ref, o_ref, sout_ref, s_ref):
    c = pl.program_id(1)

    @pl.when(c == 0)
    def _():
        s_ref[...] = s0_ref[0]

    mm = BF16 if q_ref.shape[0] >= 16 else F32
    q = q_ref[...].astype(F32)
    k = k_ref[...].astype(F32)
    qb = q.astype(mm)
    kb = k.astype(mm)
    vb = v_ref[...].astype(mm)
    qd = (q * qdec_ref[...]).astype(mm)
    kd = (k * kdec_ref[...]).astype(mm)
    heads = [slice(h * HEAD_DIM, (h + 1) * HEAD_DIM) for h in range(N_HEADS)]
    atts = [lax.dot_general(qb[:, sl], kb[:, sl], NT_DIMS, preferred_element_type=F32) for sl in heads]
    inters = [_dot(qd[:, sl], s_ref[h].astype(mm)) for h, sl in enumerate(heads)]
    for h, sl in enumerate(heads):
        s_ref[h] = s_ref[h] * cdec_ref[h] + lax.dot_general(kd[:, sl], vb[:, sl], TN_DIMS,
                                                            preferred_element_type=F32)
    outs = []
    for h, sl in enumerate(heads):
        o = _dot((atts[h] * intra_ref[h]).astype(mm), vb[:, sl]) + inters[h]
        mu = jnp.mean(o, axis=-1, keepdims=True)
        dev = o - mu
        var = jnp.mean(dev * dev, axis=-1, keepdims=True)
        outs.append(dev * lax.rsqrt(var + GN_EPS))
    on = jnp.concatenate(outs, axis=1)
    gate = g_ref[...]
    o_ref[...] = ((on * gnw_ref[...] + gnb_ref[...]) * (gate * jax.nn.sigmoid(gate))).astype(o_ref.dtype)

    @pl.when(c == pl.num_programs(1) - 1)
    def _():
        sout_ref[0] = s_ref[...]


def _ret_tables(chunk):
    lg = jnp.log1p(-(2.0 ** (-5.0 - jnp.arange(N_HEADS, dtype=F32))))
    i = jnp.arange(chunk, dtype=F32)
    diff = i[:, None] - i[None, :]
    intra = jnp.where(diff[None] >= 0, jnp.exp(jnp.maximum(diff, 0.0)[None] * lg[:, None, None]), 0.0)
    q_dec = jnp.exp((i[:, None] + 1.0) * lg[None, :])
    k_dec = jnp.exp((chunk - 1.0 - i)[:, None] * lg[None, :])
    c_dec = jnp.exp(chunk * lg)
    expand = lambda a: jnp.repeat(a, HEAD_DIM, axis=1)
    return intra, expand(q_dec), expand(k_dec), jnp.broadcast_to(c_dec[:, None, None], (N_HEADS, 1, HEAD_DIM))


def _retention(q, k, v, g, s0, gn_w, gn_b, seq, out_dtype):
    t = q.shape[0]
    batch = t // seq
    chunk = min(RET_CHUNK, seq)
    n = seq // chunk
    intra, q_dec, k_dec, c_dec = _ret_tables(chunk)
    blk = pl.BlockSpec((chunk, D_GROUP), lambda b, c: (b * n + c, 0))
    fixed2 = lambda b, c: (0, 0)
    fixed3 = lambda b, c: (0, 0, 0)
    state = pl.BlockSpec((1, N_HEADS, HEAD_DIM, HEAD_DIM), lambda b, c: (b, 0, 0, 0))
    return pl.pallas_call(
        _ret_kernel,
        grid=(batch, n),
        in_specs=[blk, blk, blk, blk, state,
                  pl.BlockSpec(intra.shape, fixed3),
                  pl.BlockSpec(q_dec.shape, fixed2),
                  pl.BlockSpec(k_dec.shape, fixed2),
                  pl.BlockSpec(c_dec.shape, fixed3),
                  pl.BlockSpec((1, D_GROUP), fixed2),
                  pl.BlockSpec((1, D_GROUP), fixed2)],
        out_specs=[blk, state],
        out_shape=[jax.ShapeDtypeStruct((t, D_GROUP), out_dtype),
                   jax.ShapeDtypeStruct((batch, N_HEADS, HEAD_DIM, HEAD_DIM), F32)],
        scratch_shapes=[pltpu.VMEM((N_HEADS, HEAD_DIM, HEAD_DIM), F32)],
        compiler_params=_params("parallel", "arbitrary"),
        name="retention",
    )(q, k, v, g, s0, intra, q_dec, k_dec, c_dec, gn_w, gn_b)


HEAD_GROUP = 2


def _fox_prompt_kernel(n_kblocks, qi_ref, kj_ref, skip_ref, qa_ref, ka_ref, vt_ref, o_ref, m_ref, acc_ref):
    step = pl.program_id(0)
    i = qi_ref[step]
    j = kj_ref[step]
    tk = ka_ref.shape[1]
    tq = qa_ref.shape[2]
    n_groups = N_HEADS // HEAD_GROUP

    @pl.when(j == 0)
    def _():
        m_ref[...] = jnp.full_like(m_ref, NEG_INF)
        acc_ref[...] = jnp.zeros_like(acc_ref)

    def scores(h):
        return _dot(ka_ref[h], qa_ref[h])

    def update(heads, causal):
        s_next = scores(heads[0])
        for n, h in enumerate(heads):
            s = s_next
            if n + 1 < len(heads):
                s_next = scores(heads[n + 1])
            if causal is not None:
                s = jnp.where(causal, s, NEG_INF)
            m_prev = m_ref[h]
            m_new = jnp.maximum(m_prev, jnp.max(s, axis=0, keepdims=True))
            p = jnp.exp2(s - m_new).astype(BF16)
            acc_ref[h] = jnp.exp2(m_prev - m_new) * acc_ref[h] + _dot(vt_ref[h], p)
            m_ref[h] = m_new

    @pl.when(j < i)
    def _():
        flags = (i * n_kblocks + j) * N_HEADS
        for grp in range(n_groups):
            heads = list(range(grp * HEAD_GROUP, (grp + 1) * HEAD_GROUP))
            skippable = functools.reduce(jnp.minimum, [skip_ref[flags + h] for h in heads])

            @pl.when(skippable == 0)
            def _():
                update(heads, None)

    @pl.when(j == i)
    def _():
        causal = (lax.broadcasted_iota(jnp.int32, (tk, tq), 0)
                  <= lax.broadcasted_iota(jnp.int32, (tk, tq), 1))
        update(list(range(N_HEADS)), causal)
        for h in range(N_HEADS):
            acc = acc_ref[h]
            o_ref[h] = (acc[:HEAD_DIM] / acc[HEAD_DIM:HEAD_DIM + 1]).astype(o_ref.dtype)


def _fox_prompt(qa, ka, vt, skip):
    t = ka.shape[1]
    tq = _pick_tile(t, 512, LANES)
    nq = t // tq
    qi = np.array([i for i in range(nq) for _ in range(i + 1)], np.int32)
    kj = np.array([j for i in range(nq) for j in range(i + 1)], np.int32)
    v_rows = vt.shape[1]
    grid_spec = pltpu.PrefetchScalarGridSpec(
        num_scalar_prefetch=3,
        grid=(len(qi),),
        in_specs=[pl.BlockSpec((N_HEADS, LANES, tq), lambda s, qi, kj, sk: (0, 0, qi[s])),
                  pl.BlockSpec((N_HEADS, tq, LANES), lambda s, qi, kj, sk: (0, kj[s], 0)),
                  pl.BlockSpec((N_HEADS, v_rows, tq), lambda s, qi, kj, sk: (0, 0, kj[s]))],
        out_specs=pl.BlockSpec((N_HEADS, HEAD_DIM, tq), lambda s, qi, kj, sk: (0, 0, qi[s])),
        scratch_shapes=[pltpu.VMEM((N_HEADS, 1, tq), F32), pltpu.VMEM((N_HEADS, v_rows, tq), F32)],
    )
    return pl.pallas_call(
        functools.partial(_fox_prompt_kernel, skip.shape[1]),
        grid_spec=grid_spec,
        out_shape=jax.ShapeDtypeStruct((N_HEADS, HEAD_DIM, t), BF16),
        compiler_params=_params("arbitrary"),
        name="fox_prompt",
    )(jnp.asarray(qi), jnp.asarray(kj), skip[:, :, :N_HEADS].reshape(-1), qa, ka, vt)


def _fox_decode_kernel(n_group, n_pages, pt_ref, q_ref, kn_ref, vn_ref, cn_ref, lt_hbm, kt_hbm, vt_hbm,
                       o_ref, lt_buf, kt_buf, vt_buf, sem, qbd_ref, m_ref, l_ref, acc_ref, carry_ref):
    b = pl.program_id(0)
    g = pl.program_id(1)
    n_steps = pl.num_programs(1)
    step = b * n_steps + g
    slot = step % 2
    n_q = q_ref.shape[0]
    rows = n_q * N_HEADS
    page = kt_buf.shape[3]
    cn = cn_ref[0]

    def page_copies(of_step, into_slot):
        seq = of_step // n_steps
        first = seq * n_pages + (n_steps - 1 - of_step % n_steps) * n_group
        copies = []
        for r in range(n_group):
            pid = pt_ref[first + r]
            copies.append(pltpu.make_async_copy(lt_hbm.at[pid], lt_buf.at[into_slot, r], sem.at[0, into_slot]))
            copies.append(pltpu.make_async_copy(kt_hbm.at[pid], kt_buf.at[into_slot, r], sem.at[1, into_slot]))
            copies.append(pltpu.make_async_copy(vt_hbm.at[pid], vt_buf.at[into_slot, r], sem.at[2, into_slot]))
        return copies

    @pl.when(step == 0)
    def _():
        for c in page_copies(step, slot):
            c.start()

    @pl.when(step + 1 < pl.num_programs(0) * n_steps)
    def _():
        for c in page_copies(step + 1, 1 - slot):
            c.start()

    for c in page_copies(step, slot):
        c.wait()
    lt_pages = [lt_buf[slot, r] for r in range(n_group)]

    def attend(s, v, v_dims):
        m_prev = m_ref[...]
        m_new = jnp.maximum(m_prev, jnp.max(s, axis=1, keepdims=True))
        alpha = jnp.exp(m_prev - m_new)
        p = jnp.exp(s - m_new)
        l_ref[...] = alpha * l_ref[...] + jnp.sum(p, axis=1, keepdims=True)
        acc_ref[...] = alpha * acc_ref[...] + lax.dot_general(p.astype(BF16), v, v_dims,
                                                              preferred_element_type=F32)
        m_ref[...] = m_new

    @pl.when(g == 0)
    def _():
        q = q_ref[...]
        rep = jnp.concatenate([jnp.broadcast_to(q[t:t + 1, :], (N_HEADS, D_GROUP)) for t in range(n_q)], axis=0)
        r_head = lax.broadcasted_iota(jnp.int32, (rows, D_GROUP), 0) % N_HEADS
        l_head = lax.broadcasted_iota(jnp.int32, (rows, D_GROUP), 1) // HEAD_DIM
        qbd_ref[...] = jnp.where(r_head == l_head, rep, 0.0).astype(BF16)
        m_ref[...] = jnp.full_like(m_ref, NEG_INF)
        l_ref[...] = jnp.zeros_like(l_ref)
        acc_ref[...] = jnp.zeros_like(acc_ref)
        carry_ref[...] = jnp.zeros_like(carry_ref)
        pad = jnp.zeros((page - n_q, D_GROUP), F32)
        k_new = jnp.concatenate([kn_ref[...], pad], axis=0).astype(BF16)
        v_new = jnp.concatenate([vn_ref[...], pad], axis=0).astype(BF16)
        s = lax.dot_general(qbd_ref[...], k_new, NT_DIMS, preferred_element_type=F32)
        key_pos = lax.broadcasted_iota(jnp.int32, (N_HEADS, page), 1)
        parts = []
        for t in range(n_q):
            s_t = s[t * N_HEADS:(t + 1) * N_HEADS, :] + cn[:, t:t + 1] - cn
            parts.append(jnp.where(key_pos <= t, s_t, NEG_INF))
        attend(jnp.concatenate(parts, axis=0), v_new, NN_DIMS)

    x = jnp.concatenate(lt_pages, axis=0)
    later = (lax.broadcasted_iota(jnp.int32, (page, page), 0)
             > lax.broadcasted_iota(jnp.int32, (page, page), 1))
    d_loc = jnp.dot(x, jnp.where(later, 1.0, 0.0).astype(F32), preferred_element_type=F32,
                    precision=lax.Precision.HIGHEST)
    tot = jnp.sum(x, axis=1, keepdims=True)
    run = carry_ref[...]
    d_pages = [None] * n_group
    for r in reversed(range(n_group)):
        d_pages[r] = d_loc[r * N_HEADS:(r + 1) * N_HEADS, :] + run
        run = run + tot[r * N_HEADS:(r + 1) * N_HEADS, :]
    carry_ref[...] = run
    n_sub = DECODE_SUBGROUPS if n_group % DECODE_SUBGROUPS == 0 else 1
    per_sub = n_group // n_sub
    subs = [list(range(a * per_sub, (a + 1) * per_sub)) for a in range(n_sub)]
    qbd = qbd_ref[...]

    def sub_scores(pages):
        return _dot(qbd, jnp.concatenate([kt_buf[slot, r].astype(BF16) for r in pages], axis=1))

    s_next = sub_scores(subs[0])
    for n, pages in enumerate(subs):
        s = s_next
        if n + 1 < n_sub:
            s_next = sub_scores(subs[n + 1])
        d_sub = jnp.concatenate([d_pages[r] for r in pages], axis=1)
        parts = [s[t * N_HEADS:(t + 1) * N_HEADS, :] + d_sub + cn[:, t:t + 1] for t in range(n_q)]
        vt_sub = jnp.concatenate([vt_buf[slot, r].astype(BF16) for r in pages], axis=1)
        attend(jnp.concatenate(parts, axis=0), vt_sub, NT_DIMS)

    @pl.when(g == pl.num_programs(1) - 1)
    def _():
        o = acc_ref[...] / l_ref[...]
        r_head = lax.broadcasted_iota(jnp.int32, (rows, D_GROUP), 0) % N_HEADS
        l_head = lax.broadcasted_iota(jnp.int32, (rows, D_GROUP), 1) // HEAD_DIM
        o = jnp.where(r_head == l_head, o, 0.0)
        o_ref[...] = jnp.concatenate(
            [jnp.sum(o[t * N_HEADS:(t + 1) * N_HEADS, :], axis=0, keepdims=True) for t in range(n_q)], axis=0)


def _fox_decode(fq, fk, fv, cn, cache_kt, cache_vt, cache_lt, page_table, n_q):
    batch, n_pages = page_table.shape
    n_pool, _, page = cache_kt.shape
    n_group = _pick_tile(n_pages, 16, 1)
    n_steps = n_pages // n_group
    rows = n_q * N_HEADS

    per_seq = pl.BlockSpec((n_q, D_GROUP), lambda b, g, pt: (b, 0))
    in_hbm = pl.BlockSpec(memory_space=pl.ANY)
    grid_spec = pltpu.PrefetchScalarGridSpec(
        num_scalar_prefetch=1,
        grid=(batch, n_steps),
        in_specs=[per_seq, per_seq, per_seq, pl.BlockSpec((1, N_HEADS, LANES), lambda b, g, pt: (b, 0, 0)),
                  in_hbm, in_hbm, in_hbm],
        out_specs=per_seq,
        scratch_shapes=[pltpu.VMEM((2, n_group, N_HEADS, page), F32),
                        pltpu.VMEM((2, n_group, D_GROUP, page), F32),
                        pltpu.VMEM((2, n_group, D_GROUP, page), F32),
                        pltpu.SemaphoreType.DMA((3, 2)),
                        pltpu.VMEM((rows, D_GROUP), BF16), pltpu.VMEM((rows, 1), F32),
                        pltpu.VMEM((rows, 1), F32), pltpu.VMEM((rows, D_GROUP), F32),
                        pltpu.VMEM((N_HEADS, 1), F32)],
    )
    return pl.pallas_call(
        functools.partial(_fox_decode_kernel, n_group, n_pages),
        grid_spec=grid_spec,
        out_shape=jax.ShapeDtypeStruct((batch * n_q, D_GROUP), F32),
        compiler_params=_params("arbitrary", "arbitrary"),
        name="fox_decode",
    )(page_table.reshape(-1), fq, fk, fv, cn, cache_lt, cache_kt, cache_vt)


def _outproj_kernel(fox_transposed, h_ref, a_ref, b_ref, w_ref, o_ref):
    half = a_ref.shape[1]
    b = b_ref[...].astype(F32).T if fox_transposed else b_ref[...]
    mix = _dot(a_ref[...].astype(BF16), w_ref[:half, :]) + _dot(b.astype(BF16), w_ref[half:, :])
    o_ref[...] = h_ref[...] + mix


def _outproj(h, o_ret, o_fox, w_out, fox_transposed):
    t, d = h.shape
    tm = _pick_tile(t, 512, LANES if fox_transposed else 8)
    row = lambda i: (i, 0)
    fox_spec = pl.BlockSpec((D_GROUP, tm), lambda i: (0, i)) if fox_transposed else pl.BlockSpec((tm, D_GROUP), row)
    return pl.pallas_call(
        functools.partial(_outproj_kernel, fox_transposed),
        grid=(t // tm,),
        in_specs=[pl.BlockSpec((tm, d), row), pl.BlockSpec((tm, D_GROUP), row),
                  fox_spec, pl.BlockSpec(w_out.shape, lambda i: (0, 0))],
        out_specs=pl.BlockSpec((tm, d), row),
        out_shape=jax.ShapeDtypeStruct((t, d), F32),
        compiler_params=_params("parallel"),
        name="outproj",
    )(h, o_ret, o_fox, w_out)


def _ple_kernel(final, h_ref, p_ref, g_ref, wg_ref, bg_ref, wp_ref, gf_ref, o_ref):
    h = h_ref[...]
    gate = jax.nn.sigmoid(_dot(_rms(h, g_ref[...]).astype(BF16), wg_ref[...]) + bg_ref[...])
    h = h + gate * _dot(p_ref[...].astype(BF16), wp_ref[...])
    o_ref[...] = _rms(h, gf_ref[...]) if final else h


def _ple(h, p, g, w_gate, b_gate, w_proj, g_final, final):
    t, d = h.shape
    tm = _pick_tile(t, 512, 8)
    row = lambda i: (i, 0)
    fixed = lambda i: (0, 0)
    return pl.pallas_call(
        functools.partial(_ple_kernel, final),
        grid=(t // tm,),
        in_specs=[pl.BlockSpec((tm, d), row), pl.BlockSpec((tm, p.shape[1]), row),
                  pl.BlockSpec((1, d), fixed), pl.BlockSpec(w_gate.shape, fixed),
                  pl.BlockSpec((1, d), fixed), pl.BlockSpec(w_proj.shape, fixed),
                  pl.BlockSpec((1, d), fixed)],
        out_specs=pl.BlockSpec((tm, d), row),
        out_shape=jax.ShapeDtypeStruct((t, d), F32),
        compiler_params=_params("parallel"),
        name="ple_gate",
    )(h, p, g, w_gate, b_gate, w_proj, g_final)


def _rope_tables(pos):
    half = HEAD_DIM // 2
    inv = ROPE_BASE ** (-jnp.arange(half, dtype=F32) / half)
    ang = pos.astype(F32)[:, None] * inv[None, :]
    cos = jnp.cos(ang)
    sin = jnp.sin(ang)
    return jnp.tile(cos, (1, LANES // half)), jnp.tile(jnp.concatenate([-sin, sin], axis=1), (1, LANES // HEAD_DIM))


def _row(v):
    return v.reshape(1, -1)


def _layer(x, p, pos, seq, s0, past, lw, g_final, final):
    t = x.shape[0]
    batch = t // seq
    prompt = past is None
    cos_t, sin_t = _rope_tables(pos)
    h = _ffn(x, _row(lw["g_ffn1"]), lw["w1_ffn1"], lw["w3_ffn1"], lw["w2_ffn1"])
    outs = _inproj(h, _row(lw["g_mix"]), lw["w_main"], lw["w_ff"], lw["b_ff"], cos_t, sin_t, prompt)
    rq, rk, rv, rg, fk, fv, lf = outs[:7]
    o_ret, s_ret = _retention(rq, rk, rv, rg, s0, _row(lw["gn_w"]), _row(lw["gn_b"]), seq,
                              BF16 if prompt else F32)
    if prompt:
        assert batch == 1, "the prompt path handles one sequence"
        qt, kh, vt = outs[7:]
        qa, ka, skip = _cumsum_pack(lf, qt, kh)
        o_fox = _fox_prompt(qa, ka, vt, skip).reshape(D_GROUP, t)
    else:
        (fq,) = outs[7:]
        _, ct = _cumsum(lf, seq)
        cache_kt, cache_vt, cache_lt, page_table = past
        cn = ct[:N_HEADS].reshape(N_HEADS, batch, seq).transpose(1, 0, 2)
        cn = jnp.pad(cn, ((0, 0), (0, 0), (0, LANES - seq)))
        o_fox = _fox_decode(fq, fk, fv, cn, cache_kt, cache_vt, cache_lt, page_table, seq)
    h = _outproj(h, o_ret, o_fox, lw["w_out"], prompt)
    h = _ffn(h, _row(lw["g_ffn2"]), lw["w1_ffn2"], lw["w3_ffn2"], lw["w2_ffn2"])
    h = _ple(h, p, _row(lw["g_ple"]), lw["w_ple_gate"], _row(lw["b_ple_gate"]), lw["w_ple_proj"],
             _row(g_final), final)
    return h, fk, fv, lf[:, :N_HEADS], s_ret


def kernel(x_prompt, x_sample, cache_fox_k, cache_fox_v, cache_fox_logf, state_ret, page_table, p_prompt, p_sample, g_ffn1, w1_ffn1, w3_ffn1, w2_ffn1, g_mix, w_in, b_forget, gn_w, gn_b, w_out, g_ffn2, w1_ffn2, w3_ffn2, w2_ffn2, g_ple, w_ple_gate, b_ple_gate, w_ple_proj, g_final):
    depth = w_in.shape[0]
    batch_p, seq_p, d = x_prompt.shape
    batch_s, seq_s, _ = x_sample.shape
    n_pages = page_table.shape[1]
    page = cache_fox_k.shape[2]
    n_pool = cache_fox_k.shape[1]
    n_main = w_in.shape[2] - N_HEADS
    hp = x_prompt.reshape(batch_p * seq_p, d)
    hs = x_sample.reshape(batch_s * seq_s, d)
    pos_p = jnp.tile(jnp.arange(seq_p), batch_p)
    pos_s = jnp.tile(n_pages * page + jnp.arange(seq_s), batch_s)
    per_layer = [[] for _ in range(8)]
    for i in range(depth):
        lw = {
            "g_ffn1": g_ffn1[i], "w1_ffn1": w1_ffn1[i].astype(BF16), "w3_ffn1": w3_ffn1[i].astype(BF16),
            "w2_ffn1": w2_ffn1[i].astype(BF16), "g_mix": g_mix[i],
            "w_main": w_in[i, :, :n_main].astype(BF16),
            "w_ff": jnp.pad(w_in[i, :, n_main:], ((0, 0), (0, LANES - N_HEADS))).astype(BF16),
            "b_ff": jnp.pad(b_forget[i], (0, LANES - N_HEADS)).reshape(1, LANES),
            "gn_w": gn_w[i], "gn_b": gn_b[i], "w_out": w_out[i].astype(BF16),
            "g_ffn2": g_ffn2[i], "w1_ffn2": w1_ffn2[i].astype(BF16), "w3_ffn2": w3_ffn2[i].astype(BF16),
            "w2_ffn2": w2_ffn2[i].astype(BF16), "g_ple": g_ple[i], "w_ple_gate": w_ple_gate[i].astype(BF16),
            "b_ple_gate": b_ple_gate[i], "w_ple_proj": w_ple_proj[i].astype(BF16),
        }
        final = i == depth - 1
        s0_p = jnp.zeros((batch_p, N_HEADS, HEAD_DIM, HEAD_DIM), F32)
        hp, kp, vp, lp, sp = _layer(hp, p_prompt[i].reshape(batch_p * seq_p, -1), pos_p, seq_p, s0_p, None,
                                    lw, g_final, final)
        past = (cache_fox_k[i].transpose(0, 2, 3, 1).reshape(n_pool, D_GROUP, page),
                cache_fox_v[i].transpose(0, 2, 3, 1).reshape(n_pool, D_GROUP, page),
                cache_fox_logf[i].transpose(0, 2, 1), page_table)
        hs, ks, vs, ls, ss = _layer(hs, p_sample[i].reshape(batch_s * seq_s, -1), pos_s, seq_s, state_ret[i],
                                    past, lw, g_final, final)
        for lst, val in zip(per_layer, (
                kp.reshape(batch_p, seq_p, N_HEADS, HEAD_DIM), vp.reshape(batch_p, seq_p, N_HEADS, HEAD_DIM),
                lp.reshape(batch_p, seq_p, N_HEADS), sp,
                ks.reshape(batch_s, seq_s, N_HEADS, HEAD_DIM), vs.reshape(batch_s, seq_s, N_HEADS, HEAD_DIM),
                ls.reshape(batch_s, seq_s, N_HEADS), ss)):
            lst.append(val)
    stacked = [jnp.stack(lst) for lst in per_layer]
    return (hp.reshape(batch_p, seq_p, d), hs.reshape(batch_s, seq_s, d), *stacked)
```

```python
import functools

import numpy as np
import jax
import jax.numpy as jnp
from jax import lax
from jax.experimental import pallas as pl
from jax.experimental.pallas import tpu as pltpu

F32 = jnp.float32
BF16 = jnp.bfloat16

HEAD_DIM = 64
N_HEADS = 8
D_GROUP = N_HEADS * HEAD_DIM
RET_CHUNK = 256
ROPE_BASE = 10000.0
NORM_EPS = 1e-6
GN_EPS = 1e-5
NEG_INF = -1e30
LANES = 128
LOG2E = 1.4426950408889634
BIAS_ROWS = 16
DECODE_SUBGROUPS = 2
FFN_SUBCHUNK = 512
SKIP_LOG2 = -75.0
VMEM_LIMIT = 52 * 1024 * 1024

NN_DIMS = (((1,), (0,)), ((), ()))
NT_DIMS = (((1,), (1,)), ((), ()))
TN_DIMS = (((0,), (0,)), ((), ()))


def _pick_tile(n, pref, mult):
    best = None
    for t in range(mult, min(n, pref) + 1, mult):
        if n % t == 0:
            best = t
    return best if best is not None else n


def _params(*sem):
    return pltpu.CompilerParams(dimension_semantics=sem, vmem_limit_bytes=VMEM_LIMIT)


def _rms(x, g):
    r = lax.rsqrt(jnp.mean(x * x, axis=-1, keepdims=True) + NORM_EPS)
    return x * r * g


def _dot(a, b):
    return jnp.dot(a, b, preferred_element_type=F32)


def _split3(x):
    hi = x.astype(BF16).astype(F32)
    r = x - hi
    mid = r.astype(BF16).astype(F32)
    lo = (r - mid).astype(BF16).astype(F32)
    return hi, mid, lo


def _select_sum(mask, x, dims, mask_first=True):
    total = None
    for term in _split3(x):
        term = term.astype(BF16)
        part = lax.dot_general(*((mask, term) if mask_first else (term, mask)), dims, preferred_element_type=F32)
        total = part if total is None else total + part
    return total


def _ffn_kernel(x_ref, g_ref, w1_ref, w3_ref, w2_ref, o_ref):
    x = x_ref[...]
    xn = _rms(x, g_ref[...]).astype(BF16)
    tf = w1_ref.shape[1]
    bounds = [(lo, min(lo + FFN_SUBCHUNK, tf)) for lo in range(0, tf, FFN_SUBCHUNK)]

    def gate_up(lo, hi):
        return _dot(xn, w1_ref[:, lo:hi]), _dot(xn, w3_ref[:, lo:hi])

    nxt = gate_up(*bounds[0])
    total = None
    for n, (lo, hi) in enumerate(bounds):
        h1, h3 = nxt
        if n + 1 < len(bounds):
            nxt = gate_up(*bounds[n + 1])
        a = (h1 * jax.nn.sigmoid(h1) * h3).astype(BF16)
        part = _dot(a, w2_ref[lo:hi, :])
        total = part if total is None else total + part
    o_ref[...] = x + 0.5 * total


def _resident(shape):
    return pl.BlockSpec(shape, lambda i: (0,) * len(shape), pipeline_mode=pl.Buffered(1))


def _ffn(x, g, w1, w3, w2):
    t, d = x.shape
    tm = _pick_tile(t, 512, 8)
    row = lambda i: (i, 0)
    return pl.pallas_call(
        _ffn_kernel,
        grid=(t // tm,),
        in_specs=[pl.BlockSpec((tm, d), row), _resident((1, d)),
                  _resident(w1.shape), _resident(w3.shape), _resident(w2.shape)],
        out_specs=pl.BlockSpec((tm, d), row),
        out_shape=jax.ShapeDtypeStruct((t, d), F32),
        compiler_params=_params("parallel"),
        name="ffn",
    )(x, g, w1, w3, w2)


def _inproj_kernel(head_major, h_ref, g_ref, w_ref, wff_ref, bf_ref, cos_ref, sin_ref,
                   rq_ref, rk_ref, rv_ref, rg_ref, fk_ref, fv_ref, lf_ref, *q_refs):
    tm = h_ref.shape[0]
    u = _rms(h_ref[...], g_ref[...]).astype(BF16)

    def piece(p):
        return _dot(u, w_ref[:, p * D_GROUP:(p + 1) * D_GROUP])

    cos = cos_ref[...]
    sin = sin_ref[...]
    lane = lax.broadcasted_iota(jnp.int32, (tm, LANES), 1)
    first_half = (lane % HEAD_DIM) < (HEAD_DIM // 2)

    def rope(x):
        outs = []
        for cb in range(D_GROUP // LANES):
            xb = x[:, cb * LANES:(cb + 1) * LANES]
            partner = jnp.where(first_half,
                                pltpu.roll(xb, LANES - HEAD_DIM // 2, 1),
                                pltpu.roll(xb, HEAD_DIM // 2, 1))
            outs.append(xb * cos + partner * sin)
        return jnp.concatenate(outs, axis=1)

    scale = HEAD_DIM ** -0.5
    cur, nxt = piece(0), piece(1)
    rq_ref[...] = rope(cur).astype(rq_ref.dtype)
    cur, nxt = nxt, piece(2)
    rk_ref[...] = (rope(cur) * scale).astype(rk_ref.dtype)
    cur, nxt = nxt, piece(3)
    rv_ref[...] = cur.astype(rv_ref.dtype)
    cur, nxt = nxt, piece(4)
    rg_ref[...] = cur
    cur, nxt = nxt, piece(5)
    fq = cur * (scale * LOG2E if head_major else scale)
    if head_major:
        qt_ref, kh_ref, vt_ref = q_refs
        qt_ref[...] = fq.T.astype(BF16).reshape(N_HEADS, HEAD_DIM, tm)
    else:
        (fq_ref,) = q_refs
        fq_ref[...] = fq
    fk, nxt = nxt, piece(6)
    fk_ref[...] = fk
    if head_major:
        for h in range(N_HEADS):
            kh_ref[h] = fk[:, h * HEAD_DIM:(h + 1) * HEAD_DIM].astype(BF16)
    fv, ff = nxt, _dot(u, wff_ref[...])
    fv_ref[...] = fv
    if head_major:
        vt_ref[:, 0:HEAD_DIM, :] = fv.T.astype(BF16).reshape(N_HEADS, HEAD_DIM, tm)
        one_row = lax.broadcasted_iota(jnp.int32, (N_HEADS, BIAS_ROWS, tm), 1) == 0
        vt_ref[:, HEAD_DIM:, :] = jnp.where(one_row, 1.0, 0.0).astype(BF16)
    ff = ff + bf_ref[...]
    lf_ref[...] = jnp.minimum(ff, 0.0) - jnp.log1p(jnp.exp(-jnp.abs(ff)))


def _inproj(h, g, w_main, w_ff, b_ff, cos_t, sin_t, head_major):
    t, d = h.shape
    tm = _pick_tile(t, 512, 8)
    act = BF16 if head_major else F32
    row = lambda i: (i, 0)
    fixed = lambda i: (0, 0)
    wide = pl.BlockSpec((tm, D_GROUP), row)
    out_shape = [jax.ShapeDtypeStruct((t, D_GROUP), act)] * 3 + [
        jax.ShapeDtypeStruct((t, D_GROUP), F32)] * 3 + [jax.ShapeDtypeStruct((t, LANES), F32)]
    out_specs = [wide] * 6 + [pl.BlockSpec((tm, LANES), row)]
    if head_major:
        transposed = lambda rows: jax.ShapeDtypeStruct((N_HEADS, rows, t), BF16)
        transposed_spec = lambda rows: pl.BlockSpec((N_HEADS, rows, tm), lambda i: (0, 0, i))
        out_shape += [transposed(HEAD_DIM), jax.ShapeDtypeStruct((N_HEADS, t, HEAD_DIM), BF16),
                      transposed(HEAD_DIM + BIAS_ROWS)]
        out_specs += [transposed_spec(HEAD_DIM), pl.BlockSpec((N_HEADS, tm, HEAD_DIM), lambda i: (0, i, 0)),
                      transposed_spec(HEAD_DIM + BIAS_ROWS)]
    else:
        out_shape += [jax.ShapeDtypeStruct((t, D_GROUP), F32)]
        out_specs += [wide]
    return pl.pallas_call(
        functools.partial(_inproj_kernel, head_major),
        grid=(t // tm,),
        in_specs=[
            pl.BlockSpec((tm, d), row),
            pl.BlockSpec((1, d), fixed),
            pl.BlockSpec(w_main.shape, fixed),
            pl.BlockSpec(w_ff.shape, fixed),
            pl.BlockSpec((1, LANES), fixed),
            pl.BlockSpec((tm, LANES), row),
            pl.BlockSpec((tm, LANES), row),
        ],
        out_specs=out_specs,
        out_shape=out_shape,
        compiler_params=_params("parallel"),
        name="inproj",
    )(h, g, w_main, w_ff, b_ff, cos_t, sin_t)


def _cumsum_kernel(seg, x_ref, c_ref, ct_ref, carry_ref):
    tb = x_ref.shape[0]
    row = lax.broadcasted_iota(jnp.int32, (tb, tb), 0)
    col = lax.broadcasted_iota(jnp.int32, (tb, tb), 1)
    keep = col <= row
    carried = seg > tb
    if not carried:
        keep = keep & ((row // seg) == (col // seg))
    tri = jnp.where(keep, 1.0, 0.0).astype(BF16)
    c = _select_sum(tri, x_ref[...], NN_DIMS)
    if carried:
        @pl.when(pl.program_id(0) == 0)
        def _():
            carry_ref[...] = jnp.zeros_like(carry_ref)

        c = c + carry_ref[...]
        carry_ref[...] = c[tb - 1:tb, :]
    c_ref[...] = c
    ct_ref[...] = c.T


def _cumsum(x, seg):
    t = x.shape[0]
    tb = _pick_tile(t, 512, LANES)
    assert seg % tb == 0 or tb % seg == 0
    return pl.pallas_call(
        functools.partial(_cumsum_kernel, seg),
        grid=(t // tb,),
        in_specs=[pl.BlockSpec((tb, LANES), lambda i: (i, 0))],
        out_specs=[pl.BlockSpec((tb, LANES), lambda i: (i, 0)),
                   pl.BlockSpec((LANES, tb), lambda i: (0, i))],
        out_shape=[jax.ShapeDtypeStruct((t, LANES), F32), jax.ShapeDtypeStruct((LANES, t), F32)],
        scratch_shapes=[pltpu.VMEM((1, LANES), F32)],
        compiler_params=_params("arbitrary"),
        name="forget_cumsum",
    )(x)


def _cumsum_pack_kernel(x_ref, qt_ref, kh_ref, qa_ref, ka_ref, skip_ref, carry_ref, kn_ref, cmin_ref):
    i = pl.program_id(0)
    tb = x_ref.shape[0]
    row = lax.broadcasted_iota(jnp.int32, (tb, tb), 0)
    col = lax.broadcasted_iota(jnp.int32, (tb, tb), 1)
    tri = jnp.where(col <= row, 1.0, 0.0).astype(BF16)

    @pl.when(i == 0)
    def _():
        carry_ref[...] = jnp.zeros_like(carry_ref)
        kn_ref[...] = jnp.zeros_like(kn_ref)
        cmin_ref[...] = jnp.zeros_like(cmin_ref)

    c_nat = _select_sum(tri, x_ref[...], NN_DIMS) + carry_ref[...]
    carry_ref[...] = c_nat[tb - 1:tb, :]
    c = c_nat * LOG2E
    ct = c.T
    sub = lax.broadcasted_iota(jnp.int32, (BIAS_ROWS, tb), 0)
    lane = lax.broadcasted_iota(jnp.int32, (tb, HEAD_DIM), 1)
    head_lane = lax.broadcasted_iota(jnp.int32, (1, LANES), 1)
    kn = jnp.zeros((1, LANES), F32)
    qn = jnp.zeros((1, LANES), F32)
    for h in range(N_HEADS):
        hi, mid, lo = _split3(ct[h:h + 1, :])
        extra = jnp.where(sub == 0, hi, jnp.where(sub == 1, mid, jnp.where(sub == 2, lo,
                                                                          jnp.where(sub < 6, 1.0, 0.0))))
        qa_ref[h, 0:HEAD_DIM, :] = qt_ref[h]
        qa_ref[h, HEAD_DIM:HEAD_DIM + BIAS_ROWS, :] = extra.astype(BF16)
        qa_ref[h, HEAD_DIM + BIAS_ROWS:, :] = jnp.zeros((LANES - HEAD_DIM - BIAS_ROWS, tb), BF16)
        hi, mid, lo = _split3(c[:, h:h + 1])
        extra = jnp.where(lane < 3, 1.0, jnp.where(lane == 3, -hi, jnp.where(lane == 4, -mid,
                                                                             jnp.where(lane == 5, -lo, 0.0))))
        ka_ref[h, :, 0:HEAD_DIM] = kh_ref[h]
        ka_ref[h, :, HEAD_DIM:] = extra.astype(BF16)
        kf = kh_ref[h].astype(F32)
        qf = qt_ref[h].astype(F32)
        k_norm = jnp.sqrt(jnp.max(jnp.sum(kf * kf, axis=1, keepdims=True), axis=0, keepdims=True))
        q_norm = jnp.sqrt(jnp.max(jnp.sum(qf * qf, axis=0, keepdims=True), axis=1, keepdims=True))
        kn = jnp.where(head_lane == h, k_norm, kn)
        qn = jnp.where(head_lane == h, q_norm, qn)
    c_max = jnp.max(c, axis=0, keepdims=True)
    c_min = jnp.min(c, axis=0, keepdims=True)
    kn_ref[pl.ds(i, 1), :] = kn
    cmin_ref[pl.ds(i, 1), :] = c_min
    bound = kn_ref[...] * qn + kn * qn + c_max - cmin_ref[...]
    skip_ref[0] = jnp.where(bound <= SKIP_LOG2, 1, 0).astype(jnp.int32)


def _cumsum_pack(x, qt, kh):
    t = x.shape[0]
    tb = _pick_tile(t, 512, LANES)
    nb = t // tb
    nb_pad = -(-nb // 8) * 8
    return pl.pallas_call(
        _cumsum_pack_kernel,
        grid=(nb,),
        in_specs=[pl.BlockSpec((tb, LANES), lambda i: (i, 0)),
                  pl.BlockSpec((N_HEADS, HEAD_DIM, tb), lambda i: (0, 0, i)),
                  pl.BlockSpec((N_HEADS, tb, HEAD_DIM), lambda i: (0, i, 0))],
        out_specs=[pl.BlockSpec((N_HEADS, LANES, tb), lambda i: (0, 0, i)),
                   pl.BlockSpec((N_HEADS, tb, LANES), lambda i: (0, i, 0)),
                   pl.BlockSpec((1, nb_pad, LANES), lambda i: (i, 0, 0))],
        out_shape=[jax.ShapeDtypeStruct((N_HEADS, LANES, t), BF16),
                   jax.ShapeDtypeStruct((N_HEADS, t, LANES), BF16),
                   jax.ShapeDtypeStruct((nb, nb_pad, LANES), jnp.int32)],
        scratch_shapes=[pltpu.VMEM((1, LANES), F32), pltpu.VMEM((nb_pad, LANES), F32),
                        pltpu.VMEM((nb_pad, LANES), F32)],
        compiler_params=_params("arbitrary"),
        name="forget_cumsum_pack",
    )(x, qt, kh)


def _ret_kernel(q_ref, k_ref, v_ref, g_ref, s0_ref, intra_ref, qdec_ref, kdec_ref, cdec_ref,
                gnw_ref, gnb_ref, o_ref, sout_ref, s_ref):
    c = pl.program_id(1)

    @pl.when(c == 0)
    def _():
        s_ref[...] = s0_ref[0]

    mm = BF16 if q_ref.shape[0] >= 16 else F32
    q = q_ref[...].astype(F32)
    k = k_ref[...].astype(F32)
    qb = q.astype(mm)
    kb = k.astype(mm)
    vb = v_ref[...].astype(mm)
    qd = (q * qdec_ref[...]).astype(mm)
    kd = (k * kdec_ref[...]).astype(mm)
    heads = [slice(h * HEAD_DIM, (h + 1) * HEAD_DIM) for h in range(N_HEADS)]
    atts = [lax.dot_general(qb[:, sl], kb[:, sl], NT_DIMS, preferred_element_type=F32) for sl in heads]
    inters = [_dot(qd[:, sl], s_ref[h].astype(mm)) for h, sl in enumerate(heads)]
    for h, sl in enumerate(heads):
        s_ref[h] = s_ref[h] * cdec_ref[h] + lax.dot_general(kd[:, sl], vb[:, sl], TN_DIMS,
                                                            preferred_element_type=F32)
    outs = []
    for h, sl in enumerate(heads):
        o = _dot((atts[h] * intra_ref[h]).astype(mm), vb[:, sl]) + inters[h]
        mu = jnp.mean(o, axis=-1, keepdims=True)
        dev = o - mu
        var = jnp.mean(dev * dev, axis=-1, keepdims=True)
        outs.append(dev * lax.rsqrt(var + GN_EPS))
    on = jnp.concatenate(outs, axis=1)
    gate = g_ref[...]
    o_ref[...] = ((on * gnw_ref[...] + gnb_ref[...]) * (gate * jax.nn.sigmoid(gate))).astype(o_ref.dtype)

    @pl.when(c == pl.num_programs(1) - 1)
    def _():
        sout_ref[0] = s_ref[...]


def _ret_tables(chunk):
    lg = jnp.log1p(-(2.0 ** (-5.0 - jnp.arange(N_HEADS, dtype=F32))))
    i = jnp.arange(chunk, dtype=F32)
    diff = i[:, None] - i[None, :]
    intra = jnp.where(diff[None] >= 0, jnp.exp(jnp.maximum(diff, 0.0)[None] * lg[:, None, None]), 0.0)
    q_dec = jnp.exp((i[:, None] + 1.0) * lg[None, :])
    k_dec = jnp.exp((chunk - 1.0 - i)[:, None] * lg[None, :])
    c_dec = jnp.exp(chunk * lg)
    expand = lambda a: jnp.repeat(a, HEAD_DIM, axis=1)
    return intra, expand(q_dec), expand(k_dec), jnp.broadcast_to(c_dec[:, None, None], (N_HEADS, 1, HEAD_DIM))


def _retention(q, k, v, g, s0, gn_w, gn_b, seq, out_dtype):
    t = q.shape[0]
    batch = t // seq
    chunk = min(RET_CHUNK, seq)
    n = seq // chunk
    intra, q_dec, k_dec, c_dec = _ret_tables(chunk)
    blk = pl.BlockSpec((chunk, D_GROUP), lambda b, c: (b * n + c, 0))
    fixed2 = lambda b, c: (0, 0)
    fixed3 = lambda b, c: (0, 0, 0)
    state = pl.BlockSpec((1, N_HEADS, HEAD_DIM, HEAD_DIM), lambda b, c: (b, 0, 0, 0))
    return pl.pallas_call(
        _ret_kernel,
        grid=(batch, n),
        in_specs=[blk, blk, blk, blk, state,
                  pl.BlockSpec(intra.shape, fixed3),
                  pl.BlockSpec(q_dec.shape, fixed2),
                  pl.BlockSpec(k_dec.shape, fixed2),
                  pl.BlockSpec(c_dec.shape, fixed3),
                  pl.BlockSpec((1, D_GROUP), fixed2),
                  pl.BlockSpec((1, D_GROUP), fixed2)],
        out_specs=[blk, state],
        out_shape=[jax.ShapeDtypeStruct((t, D_GROUP), out_dtype),
                   jax.ShapeDtypeStruct((batch, N_HEADS, HEAD_DIM, HEAD_DIM), F32)],
        scratch_shapes=[pltpu.VMEM((N_HEADS, HEAD_DIM, HEAD_DIM), F32)],
        compiler_params=_params("parallel", "arbitrary"),
        name="retention",
    )(q, k, v, g, s0, intra, q_dec, k_dec, c_dec, gn_w, gn_b)


HEAD_GROUP = 2


def _fox_prompt_kernel(n_kblocks, qi_ref, kj_ref, skip_ref, qa_ref, ka_ref, vt_ref, o_ref, m_ref, acc_ref):
    step = pl.program_id(0)
    i = qi_ref[step]
    j = kj_ref[step]
    tk = ka_ref.shape[1]
    tq = qa_ref.shape[2]
    n_groups = N_HEADS // HEAD_GROUP

    @pl.when(j == 0)
    def _():
        m_ref[...] = jnp.full_like(m_ref, NEG_INF)
        acc_ref[...] = jnp.zeros_like(acc_ref)

    def scores(h):
        return _dot(ka_ref[h], qa_ref[h])

    def update(heads, causal):
        s_next = scores(heads[0])
        for n, h in enumerate(heads):
            s = s_next
            if n + 1 < len(heads):
                s_next = scores(heads[n + 1])
            if causal is not None:
                s = jnp.where(causal, s, NEG_INF)
            m_prev = m_ref[h]
            m_new = jnp.maximum(m_prev, jnp.max(s, axis=0, keepdims=True))
            p = jnp.exp2(s - m_new).astype(BF16)
            acc_ref[h] = jnp.exp2(m_prev - m_new) * acc_ref[h] + _dot(vt_ref[h], p)
            m_ref[h] = m_new

    @pl.when(j < i)
    def _():
        flags = (i * n_kblocks + j) * N_HEADS
        for grp in range(n_groups):
            heads = list(range(grp * HEAD_GROUP, (grp + 1) * HEAD_GROUP))
            skippable = functools.reduce(jnp.minimum, [skip_ref[flags + h] for h in heads])

            @pl.when(skippable == 0)
            def _():
                update(heads, None)

    @pl.when(j == i)
    def _():
        causal = (lax.broadcasted_iota(jnp.int32, (tk, tq), 0)
                  <= lax.broadcasted_iota(jnp.int32, (tk, tq), 1))
        update(list(range(N_HEADS)), causal)
        for h in range(N_HEADS):
            acc = acc_ref[h]
            o_ref[h] = (acc[:HEAD_DIM] / acc[HEAD_DIM:HEAD_DIM + 1]).astype(o_ref.dtype)


def _fox_prompt(qa, ka, vt, skip):
    t = ka.shape[1]
    tq = _pick_tile(t, 512, LANES)
    nq = t // tq
    qi = np.array([i for i in range(nq) for _ in range(i + 1)], np.int32)
    kj = np.array([j for i in range(nq) for j in range(i + 1)], np.int32)
    v_rows = vt.shape[1]
    grid_spec = pltpu.PrefetchScalarGridSpec(
        num_scalar_prefetch=3,
        grid=(len(qi),),
        in_specs=[pl.BlockSpec((N_HEADS, LANES, tq), lambda s, qi, kj, sk: (0, 0, qi[s])),
                  pl.BlockSpec((N_HEADS, tq, LANES), lambda s, qi, kj, sk: (0, kj[s], 0)),
                  pl.BlockSpec((N_HEADS, v_rows, tq), lambda s, qi, kj, sk: (0, 0, kj[s]))],
        out_specs=pl.BlockSpec((N_HEADS, HEAD_DIM, tq), lambda s, qi, kj, sk: (0, 0, qi[s])),
        scratch_shapes=[pltpu.VMEM((N_HEADS, 1, tq), F32), pltpu.VMEM((N_HEADS, v_rows, tq), F32)],
    )
    return pl.pallas_call(
        functools.partial(_fox_prompt_kernel, skip.shape[1]),
        grid_spec=grid_spec,
        out_shape=jax.ShapeDtypeStruct((N_HEADS, HEAD_DIM, t), BF16),
        compiler_params=_params("arbitrary"),
        name="fox_prompt",
    )(jnp.asarray(qi), jnp.asarray(kj), skip[:, :, :N_HEADS].reshape(-1), qa, ka, vt)


def _fox_decode_kernel(n_group, n_pages, pt_ref, q_ref, kn_ref, vn_ref, cn_ref, lt_hbm, kt_hbm, vt_hbm,
                       o_ref, lt_buf, kt_buf, vt_buf, sem, qbd_ref, m_ref, l_ref, acc_ref, carry_ref):
    b = pl.program_id(0)
    g = pl.program_id(1)
    n_steps = pl.num_programs(1)
    step = b * n_steps + g
    slot = step % 2
    n_q = q_ref.shape[0]
    rows = n_q * N_HEADS
    page = kt_buf.shape[3]
    cn = cn_ref[0]

    def page_copies(of_step, into_slot):
        seq = of_step // n_steps
        first = seq * n_pages + (n_steps - 1 - of_step % n_steps) * n_group
        copies = []
        for r in range(n_group):
            pid = pt_ref[first + r]
            copies.append(pltpu.make_async_copy(lt_hbm.at[pid], lt_buf.at[into_slot, r], sem.at[0, into_slot]))
            copies.append(pltpu.make_async_copy(kt_hbm.at[pid], kt_buf.at[into_slot, r], sem.at[1, into_slot]))
            copies.append(pltpu.make_async_copy(vt_hbm.at[pid], vt_buf.at[into_slot, r], sem.at[2, into_slot]))
        return copies

    @pl.when(step == 0)
    def _():
        for c in page_copies(step, slot):
            c.start()

    @pl.when(step + 1 < pl.num_programs(0) * n_steps)
    def _():
        for c in page_copies(step + 1, 1 - slot):
            c.start()

    for c in page_copies(step, slot):
        c.wait()
    lt_pages = [lt_buf[slot, r] for r in range(n_group)]

    def attend(s, v, v_dims):
        m_prev = m_ref[...]
        m_new = jnp.maximum(m_prev, jnp.max(s, axis=1, keepdims=True))
        alpha = jnp.exp(m_prev - m_new)
        p = jnp.exp(s - m_new)
        l_ref[...] = alpha * l_ref[...] + jnp.sum(p, axis=1, keepdims=True)
        acc_ref[...] = alpha * acc_ref[...] + lax.dot_general(p.astype(BF16), v, v_dims,
                                                              preferred_element_type=F32)
        m_ref[...] = m_new

    @pl.when(g == 0)
    def _():
        q = q_ref[...]
        rep = jnp.concatenate([jnp.broadcast_to(q[t:t + 1, :], (N_HEADS, D_GROUP)) for t in range(n_q)], axis=0)
        r_head = lax.broadcasted_iota(jnp.int32, (rows, D_GROUP), 0) % N_HEADS
        l_head = lax.broadcasted_iota(jnp.int32, (rows, D_GROUP), 1) // HEAD_DIM
        qbd_ref[...] = jnp.where(r_head == l_head, rep, 0.0).astype(BF16)
        m_ref[...] = jnp.full_like(m_ref, NEG_INF)
        l_ref[...] = jnp.zeros_like(l_ref)
        acc_ref[...] = jnp.zeros_like(acc_ref)
        carry_ref[...] = jnp.zeros_like(carry_ref)
        pad = jnp.zeros((page - n_q, D_GROUP), F32)
        k_new = jnp.concatenate([kn_ref[...], pad], axis=0).astype(BF16)
        v_new = jnp.concatenate([vn_ref[...], pad], axis=0).astype(BF16)
        s = lax.dot_general(qbd_ref[...], k_new, NT_DIMS, preferred_element_type=F32)
        key_pos = lax.broadcasted_iota(jnp.int32, (N_HEADS, page), 1)
        parts = []
        for t in range(n_q):
            s_t = s[t * N_HEADS:(t + 1) * N_HEADS, :] + cn[:, t:t + 1] - cn
            parts.append(jnp.where(key_pos <= t, s_t, NEG_INF))
        attend(jnp.concatenate(parts, axis=0), v_new, NN_DIMS)

    x = jnp.concatenate(lt_pages, axis=0)
    later = (lax.broadcasted_iota(jnp.int32, (page, page), 0)
             > lax.broadcasted_iota(jnp.int32, (page, page), 1))
    d_loc = _select_sum(jnp.where(later, 1.0, 0.0).astype(BF16), x, NN_DIMS, mask_first=False)
    tot = jnp.sum(x, axis=1, keepdims=True)
    run = carry_ref[...]
    d_pages = [None] * n_group
    for r in reversed(range(n_group)):
        d_pages[r] = d_loc[r * N_HEADS:(r + 1) * N_HEADS, :] + run
        run = run + tot[r * N_HEADS:(r + 1) * N_HEADS, :]
    carry_ref[...] = run
    n_sub = DECODE_SUBGROUPS if n_group % DECODE_SUBGROUPS == 0 else 1
    per_sub = n_group // n_sub
    subs = [list(range(a * per_sub, (a + 1) * per_sub)) for a in range(n_sub)]
    qbd = qbd_ref[...]

    def sub_scores(pages):
        return _dot(qbd, jnp.concatenate([kt_buf[slot, r].astype(BF16) for r in pages], axis=1))

    s_next = sub_scores(subs[0])
    for n, pages in enumerate(subs):
        s = s_next
        if n + 1 < n_sub:
            s_next = sub_scores(subs[n + 1])
        d_sub = jnp.concatenate([d_pages[r] for r in pages], axis=1)
        parts = [s[t * N_HEADS:(t + 1) * N_HEADS, :] + d_sub + cn[:, t:t + 1] for t in range(n_q)]
        vt_sub = jnp.concatenate([vt_buf[slot, r].astype(BF16) for r in pages], axis=1)
        attend(jnp.concatenate(parts, axis=0), vt_sub, NT_DIMS)

    @pl.when(g == pl.num_programs(1) - 1)
    def _():
        o = acc_ref[...] / l_ref[...]
        r_head = lax.broadcasted_iota(jnp.int32, (rows, D_GROUP), 0) % N_HEADS
        l_head = lax.broadcasted_iota(jnp.int32, (rows, D_GROUP), 1) // HEAD_DIM
        o = jnp.where(r_head == l_head, o, 0.0)
        o_ref[...] = jnp.concatenate(
            [jnp.sum(o[t * N_HEADS:(t + 1) * N_HEADS, :], axis=0, keepdims=True) for t in range(n_q)], axis=0)


def _fox_decode(fq, fk, fv, cn, cache_kt, cache_vt, cache_lt, page_table, n_q):
    batch, n_pages = page_table.shape
    n_pool, _, page = cache_kt.shape
    n_group = _pick_tile(n_pages, 16, 1)
    n_steps = n_pages // n_group
    rows = n_q * N_HEADS

    per_seq = pl.BlockSpec((n_q, D_GROUP), lambda b, g, pt: (b, 0))
    in_hbm = pl.BlockSpec(memory_space=pl.ANY)
    grid_spec = pltpu.PrefetchScalarGridSpec(
        num_scalar_prefetch=1,
        grid=(batch, n_steps),
        in_specs=[per_seq, per_seq, per_seq, pl.BlockSpec((1, N_HEADS, LANES), lambda b, g, pt: (b, 0, 0)),
                  in_hbm, in_hbm, in_hbm],
        out_specs=per_seq,
        scratch_shapes=[pltpu.VMEM((2, n_group, N_HEADS, page), F32),
                        pltpu.VMEM((2, n_group, D_GROUP, page), F32),
                        pltpu.VMEM((2, n_group, D_GROUP, page), F32),
                        pltpu.SemaphoreType.DMA((3, 2)),
                        pltpu.VMEM((rows, D_GROUP), BF16), pltpu.VMEM((rows, 1), F32),
                        pltpu.VMEM((rows, 1), F32), pltpu.VMEM((rows, D_GROUP), F32),
                        pltpu.VMEM((N_HEADS, 1), F32)],
    )
    return pl.pallas_call(
        functools.partial(_fox_decode_kernel, n_group, n_pages),
        grid_spec=grid_spec,
        out_shape=jax.ShapeDtypeStruct((batch * n_q, D_GROUP), F32),
        compiler_params=_params("arbitrary", "arbitrary"),
        name="fox_decode",
    )(page_table.reshape(-1), fq, fk, fv, cn, cache_lt, cache_kt, cache_vt)


def _outproj_kernel(fox_transposed, h_ref, a_ref, b_ref, w_ref, o_ref):
    half = a_ref.shape[1]
    b = b_ref[...].astype(F32).T if fox_transposed else b_ref[...]
    mix = _dot(a_ref[...].astype(BF16), w_ref[:half, :]) + _dot(b.astype(BF16), w_ref[half:, :])
    o_ref[...] = h_ref[...] + mix


def _outproj(h, o_ret, o_fox, w_out, fox_transposed):
    t, d = h.shape
    tm = _pick_tile(t, 512, LANES if fox_transposed else 8)
    row = lambda i: (i, 0)
    fox_spec = pl.BlockSpec((D_GROUP, tm), lambda i: (0, i)) if fox_transposed else pl.BlockSpec((tm, D_GROUP), row)
    return pl.pallas_call(
        functools.partial(_outproj_kernel, fox_transposed),
        grid=(t // tm,),
        in_specs=[pl.BlockSpec((tm, d), row), pl.BlockSpec((tm, D_GROUP), row),
                  fox_spec, pl.BlockSpec(w_out.shape, lambda i: (0, 0))],
        out_specs=pl.BlockSpec((tm, d), row),
        out_shape=jax.ShapeDtypeStruct((t, d), F32),
        compiler_params=_params("parallel"),
        name="outproj",
    )(h, o_ret, o_fox, w_out)


def _ple_kernel(final, h_ref, p_ref, g_ref, wg_ref, bg_ref, wp_ref, gf_ref, o_ref):
    h = h_ref[...]
    gate = jax.nn.sigmoid(_dot(_rms(h, g_ref[...]).astype(BF16), wg_ref[...]) + bg_ref[...])
    h = h + gate * _dot(p_ref[...].astype(BF16), wp_ref[...])
    o_ref[...] = _rms(h, gf_ref[...]) if final else h


def _ple(h, p, g, w_gate, b_gate, w_proj, g_final, final):
    t, d = h.shape
    tm = _pick_tile(t, 512, 8)
    row = lambda i: (i, 0)
    fixed = lambda i: (0, 0)
    return pl.pallas_call(
        functools.partial(_ple_kernel, final),
        grid=(t // tm,),
        in_specs=[pl.BlockSpec((tm, d), row), pl.BlockSpec((tm, p.shape[1]), row),
                  pl.BlockSpec((1, d), fixed), pl.BlockSpec(w_gate.shape, fixed),
                  pl.BlockSpec((1, d), fixed), pl.BlockSpec(w_proj.shape, fixed),
                  pl.BlockSpec((1, d), fixed)],
        out_specs=pl.BlockSpec((tm, d), row),
        out_shape=jax.ShapeDtypeStruct((t, d), F32),
        compiler_params=_params("parallel"),
        name="ple_gate",
    )(h, p, g, w_gate, b_gate, w_proj, g_final)


def _rope_tables(pos):
    half = HEAD_DIM // 2
    inv = ROPE_BASE ** (-jnp.arange(half, dtype=F32) / half)
    ang = pos.astype(F32)[:, None] * inv[None, :]
    cos = jnp.cos(ang)
    sin = jnp.sin(ang)
    return jnp.tile(cos, (1, LANES // half)), jnp.tile(jnp.concatenate([-sin, sin], axis=1), (1, LANES // HEAD_DIM))


def _row(v):
    return v.reshape(1, -1)


def _layer(x, p, pos, seq, s0, past, lw, g_final, final):
    t = x.shape[0]
    batch = t // seq
    prompt = past is None
    cos_t, sin_t = _rope_tables(pos)
    h = _ffn(x, _row(lw["g_ffn1"]), lw["w1_ffn1"], lw["w3_ffn1"], lw["w2_ffn1"])
    outs = _inproj(h, _row(lw["g_mix"]), lw["w_main"], lw["w_ff"], lw["b_ff"], cos_t, sin_t, prompt)
    rq, rk, rv, rg, fk, fv, lf = outs[:7]
    o_ret, s_ret = _retention(rq, rk, rv, rg, s0, _row(lw["gn_w"]), _row(lw["gn_b"]), seq,
                              BF16 if prompt else F32)
    if prompt:
        assert batch == 1, "the prompt path handles one sequence"
        qt, kh, vt = outs[7:]
        qa, ka, skip = _cumsum_pack(lf, qt, kh)
        o_fox = _fox_prompt(qa, ka, vt, skip).reshape(D_GROUP, t)
    else:
        (fq,) = outs[7:]
        _, ct = _cumsum(lf, seq)
        cache_kt, cache_vt, cache_lt, page_table = past
        cn = ct[:N_HEADS].reshape(N_HEADS, batch, seq).transpose(1, 0, 2)
        cn = jnp.pad(cn, ((0, 0), (0, 0), (0, LANES - seq)))
        o_fox = _fox_decode(fq, fk, fv, cn, cache_kt, cache_vt, cache_lt, page_table, seq)
    h = _outproj(h, o_ret, o_fox, lw["w_out"], prompt)
    h = _ffn(h, _row(lw["g_ffn2"]), lw["w1_ffn2"], lw["w3_ffn2"], lw["w2_ffn2"])
    h = _ple(h, p, _row(lw["g_ple"]), lw["w_ple_gate"], _row(lw["b_ple_gate"]), lw["w_ple_proj"],
             _row(g_final), final)
    return h, fk, fv, lf[:, :N_HEADS], s_ret


def kernel(x_prompt, x_sample, cache_fox_k, cache_fox_v, cache_fox_logf, state_ret, page_table, p_prompt, p_sample, g_ffn1, w1_ffn1, w3_ffn1, w2_ffn1, g_mix, w_in, b_forget, gn_w, gn_b, w_out, g_ffn2, w1_ffn2, w3_ffn2, w2_ffn2, g_ple, w_ple_gate, b_ple_gate, w_ple_proj, g_final):
    depth = w_in.shape[0]
    batch_p, seq_p, d = x_prompt.shape
    batch_s, seq_s, _ = x_sample.shape
    n_pages = page_table.shape[1]
    page = cache_fox_k.shape[2]
    n_pool = cache_fox_k.shape[1]
    n_main = w_in.shape[2] - N_HEADS
    hp = x_prompt.reshape(batch_p * seq_p, d)
    hs = x_sample.reshape(batch_s * seq_s, d)
    pos_p = jnp.tile(jnp.arange(seq_p), batch_p)
    pos_s = jnp.tile(n_pages * page + jnp.arange(seq_s), batch_s)
    per_layer = [[] for _ in range(8)]
    for i in range(depth):
        lw = {
            "g_ffn1": g_ffn1[i], "w1_ffn1": w1_ffn1[i].astype(BF16), "w3_ffn1": w3_ffn1[i].astype(BF16),
            "w2_ffn1": w2_ffn1[i].astype(BF16), "g_mix": g_mix[i],
            "w_main": w_in[i, :, :n_main].astype(BF16),
            "w_ff": jnp.pad(w_in[i, :, n_main:], ((0, 0), (0, LANES - N_HEADS))).astype(BF16),
            "b_ff": jnp.pad(b_forget[i], (0, LANES - N_HEADS)).reshape(1, LANES),
            "gn_w": gn_w[i], "gn_b": gn_b[i], "w_out": w_out[i].astype(BF16),
            "g_ffn2": g_ffn2[i], "w1_ffn2": w1_ffn2[i].astype(BF16), "w3_ffn2": w3_ffn2[i].astype(BF16),
            "w2_ffn2": w2_ffn2[i].astype(BF16), "g_ple": g_ple[i], "w_ple_gate": w_ple_gate[i].astype(BF16),
            "b_ple_gate": b_ple_gate[i], "w_ple_proj": w_ple_proj[i].astype(BF16),
        }
        final = i == depth - 1
        s0_p = jnp.zeros((batch_p, N_HEADS, HEAD_DIM, HEAD_DIM), F32)
        hp, kp, vp, lp, sp = _layer(hp, p_prompt[i].reshape(batch_p * seq_p, -1), pos_p, seq_p, s0_p, None,
                                    lw, g_final, final)
        past = (cache_fox_k[i].transpose(0, 2, 3, 1).reshape(n_pool, D_GROUP, page),
                cache_fox_v[i].transpose(0, 2, 3, 1).reshape(n_pool, D_GROUP, page),
                cache_fox_logf[i].transpose(0, 2, 1), page_table)
        hs, ks, vs, ls, ss = _layer(hs, p_sample[i].reshape(batch_s * seq_s, -1), pos_s, seq_s, state_ret[i],
                                    past, lw, g_final, final)
        for lst, val in zip(per_layer, (
                kp.reshape(batch_p, seq_p, N_HEADS, HEAD_DIM), vp.reshape(batch_p, seq_p, N_HEADS, HEAD_DIM),
                lp.reshape(batch_p, seq_p, N_HEADS), sp,
                ks.reshape(batch_s, seq_s, N_HEADS, HEAD_DIM), vs.reshape(batch_s, seq_s, N_HEADS, HEAD_DIM),
                ls.reshape(batch_s, seq_s, N_HEADS), ss)):
            lst.append(val)
    stacked = [jnp.stack(lst) for lst in per_layer]
    return (hp.reshape(batch_p, seq_p, d), hs.reshape(batch_s, seq_s, d), *stacked)
```

```python
import functools

import numpy as np
import jax
import jax.numpy as jnp
from jax import lax
from jax.experimental import pallas as pl
from jax.experimental.pallas import tpu as pltpu

F32 = jnp.float32
BF16 = jnp.bfloat16

HEAD_DIM = 64
N_HEADS = 8
D_GROUP = N_HEADS * HEAD_DIM
RET_CHUNK = 256
ROPE_BASE = 10000.0
NORM_EPS = 1e-6
GN_EPS = 1e-5
NEG_INF = -1e30
LANES = 128
LOG2E = 1.4426950408889634
BIAS_ROWS = 16
DECODE_SUBGROUPS = 2
FFN_SUBCHUNK = 512
NORM_MARGIN = 1.01
DIAG_SLACK = 2.0 ** -8
SKIP_LOG2 = -75.0
VMEM_LIMIT = 52 * 1024 * 1024

NN_DIMS = (((1,), (0,)), ((), ()))
NT_DIMS = (((1,), (1,)), ((), ()))
TN_DIMS = (((0,), (0,)), ((), ()))


def _pick_tile(n, pref, mult):
    best = None
    for t in range(mult, min(n, pref) + 1, mult):
        if n % t == 0:
            best = t
    return best if best is not None else n


def _params(*sem):
    return pltpu.CompilerParams(dimension_semantics=sem, vmem_limit_bytes=VMEM_LIMIT)


def _rms(x, g):
    r = lax.rsqrt(jnp.mean(x * x, axis=-1, keepdims=True) + NORM_EPS)
    return x * r * g


def _dot(a, b):
    return jnp.dot(a, b, preferred_element_type=F32)


def _split3(x):
    hi = x.astype(BF16).astype(F32)
    r = x - hi
    mid = r.astype(BF16).astype(F32)
    lo = (r - mid).astype(BF16).astype(F32)
    return hi, mid, lo


def _select_sum(mask, x, dims, mask_first=True):
    total = None
    for term in _split3(x):
        term = term.astype(BF16)
        part = lax.dot_general(*((mask, term) if mask_first else (term, mask)), dims, preferred_element_type=F32)
        total = part if total is None else total + part
    return total


def _ffn_kernel(x_ref, g_ref, w1_ref, w3_ref, w2_ref, o_ref):
    x = x_ref[...]
    xn = _rms(x, g_ref[...]).astype(BF16)
    tf = w1_ref.shape[1]
    bounds = [(lo, min(lo + FFN_SUBCHUNK, tf)) for lo in range(0, tf, FFN_SUBCHUNK)]

    def gate_up(lo, hi):
        return _dot(xn, w1_ref[:, lo:hi]), _dot(xn, w3_ref[:, lo:hi])

    nxt = gate_up(*bounds[0])
    total = None
    for n, (lo, hi) in enumerate(bounds):
        h1, h3 = nxt
        if n + 1 < len(bounds):
            nxt = gate_up(*bounds[n + 1])
        a = (h1 * jax.nn.sigmoid(h1) * h3).astype(BF16)
        part = _dot(a, w2_ref[lo:hi, :])
        total = part if total is None else total + part
    o_ref[...] = x + 0.5 * total


def _resident(shape):
    return pl.BlockSpec(shape, lambda i: (0,) * len(shape), pipeline_mode=pl.Buffered(1))


def _ffn(x, g, w1, w3, w2):
    t, d = x.shape
    tm = _pick_tile(t, 512, 8)
    row = lambda i: (i, 0)
    return pl.pallas_call(
        _ffn_kernel,
        grid=(t // tm,),
        in_specs=[pl.BlockSpec((tm, d), row), _resident((1, d)),
                  _resident(w1.shape), _resident(w3.shape), _resident(w2.shape)],
        out_specs=pl.BlockSpec((tm, d), row),
        out_shape=jax.ShapeDtypeStruct((t, d), F32),
        compiler_params=_params("parallel"),
        name="ffn",
    )(x, g, w1, w3, w2)


def _inproj_kernel(head_major, h_ref, g_ref, w_ref, wff_ref, bf_ref, cos_ref, sin_ref,
                   rq_ref, rk_ref, rv_ref, rg_ref, fk_ref, fv_ref, lf_ref, *q_refs):
    tm = h_ref.shape[0]
    u = _rms(h_ref[...], g_ref[...]).astype(BF16)

    def piece(p):
        return _dot(u, w_ref[:, p * D_GROUP:(p + 1) * D_GROUP])

    cos = cos_ref[...]
    sin = sin_ref[...]
    lane = lax.broadcasted_iota(jnp.int32, (tm, LANES), 1)
    first_half = (lane % HEAD_DIM) < (HEAD_DIM // 2)

    def rope(x):
        outs = []
        for cb in range(D_GROUP // LANES):
            xb = x[:, cb * LANES:(cb + 1) * LANES]
            partner = jnp.where(first_half,
                                pltpu.roll(xb, LANES - HEAD_DIM // 2, 1),
                                pltpu.roll(xb, HEAD_DIM // 2, 1))
            outs.append(xb * cos + partner * sin)
        return jnp.concatenate(outs, axis=1)

    scale = HEAD_DIM ** -0.5
    cur, nxt = piece(0), piece(1)
    rq_ref[...] = rope(cur).astype(rq_ref.dtype)
    cur, nxt = nxt, piece(2)
    rk_ref[...] = (rope(cur) * scale).astype(rk_ref.dtype)
    cur, nxt = nxt, piece(3)
    rv_ref[...] = cur.astype(rv_ref.dtype)
    cur, nxt = nxt, piece(4)
    rg_ref[...] = cur
    cur, nxt = nxt, piece(5)
    fq = cur * (scale * LOG2E if head_major else scale)
    if head_major:
        qt_ref, kh_ref, vt_ref, stats_ref = q_refs
        qt_ref[...] = fq.T.astype(BF16).reshape(N_HEADS, HEAD_DIM, tm)
    else:
        (fq_ref,) = q_refs
        fq_ref[...] = fq
    fk, nxt = nxt, piece(6)
    fk_ref[...] = fk
    if head_major:
        for h in range(N_HEADS):
            kh_ref[h] = fk[:, h * HEAD_DIM:(h + 1) * HEAD_DIM].astype(BF16)
        qb = fq.astype(BF16).astype(F32)
        kb = fk.astype(BF16).astype(F32)
        selector = (lax.broadcasted_iota(jnp.int32, (D_GROUP, LANES), 0) // HEAD_DIM
                    == lax.broadcasted_iota(jnp.int32, (D_GROUP, LANES), 1)).astype(BF16)
        per_head = lambda x: _dot(x.astype(BF16), selector)
        kn = jnp.sqrt(jnp.max(per_head(kb * kb), axis=0, keepdims=True)) * NORM_MARGIN
        qn = jnp.sqrt(jnp.max(per_head(qb * qb), axis=0, keepdims=True)) * NORM_MARGIN
        d_min = jnp.min(per_head(qb * kb), axis=0, keepdims=True) - DIAG_SLACK * kn * qn
        sub = lax.broadcasted_iota(jnp.int32, (N_HEADS, LANES), 0)
        stats_ref[0] = jnp.where(sub == 0, d_min, jnp.where(sub == 1, kn, qn))
    fv, ff = nxt, _dot(u, wff_ref[...])
    fv_ref[...] = fv
    if head_major:
        vt_ref[:, 0:HEAD_DIM, :] = fv.T.astype(BF16).reshape(N_HEADS, HEAD_DIM, tm)
        one_row = lax.broadcasted_iota(jnp.int32, (N_HEADS, BIAS_ROWS, tm), 1) == 0
        vt_ref[:, HEAD_DIM:, :] = jnp.where(one_row, 1.0, 0.0).astype(BF16)
    ff = ff + bf_ref[...]
    lf_ref[...] = jnp.minimum(ff, 0.0) - jnp.log1p(jnp.exp(-jnp.abs(ff)))


def _inproj(h, g, w_main, w_ff, b_ff, cos_t, sin_t, head_major):
    t, d = h.shape
    tm = _pick_tile(t, 512, 8)
    act = BF16 if head_major else F32
    row = lambda i: (i, 0)
    fixed = lambda i: (0, 0)
    wide = pl.BlockSpec((tm, D_GROUP), row)
    out_shape = [jax.ShapeDtypeStruct((t, D_GROUP), act)] * 3 + [
        jax.ShapeDtypeStruct((t, D_GROUP), F32)] * 3 + [jax.ShapeDtypeStruct((t, LANES), F32)]
    out_specs = [wide] * 6 + [pl.BlockSpec((tm, LANES), row)]
    if head_major:
        transposed = lambda rows: jax.ShapeDtypeStruct((N_HEADS, rows, t), BF16)
        transposed_spec = lambda rows: pl.BlockSpec((N_HEADS, rows, tm), lambda i: (0, 0, i))
        out_shape += [transposed(HEAD_DIM), jax.ShapeDtypeStruct((N_HEADS, t, HEAD_DIM), BF16),
                      transposed(HEAD_DIM + BIAS_ROWS)]
        out_specs += [transposed_spec(HEAD_DIM), pl.BlockSpec((N_HEADS, tm, HEAD_DIM), lambda i: (0, i, 0)),
                      transposed_spec(HEAD_DIM + BIAS_ROWS)]
        out_shape += [jax.ShapeDtypeStruct((t // tm, N_HEADS, LANES), F32)]
        out_specs += [pl.BlockSpec((1, N_HEADS, LANES), lambda i: (i, 0, 0))]
    else:
        out_shape += [jax.ShapeDtypeStruct((t, D_GROUP), F32)]
        out_specs += [wide]
    return pl.pallas_call(
        functools.partial(_inproj_kernel, head_major),
        grid=(t // tm,),
        in_specs=[
            pl.BlockSpec((tm, d), row),
            pl.BlockSpec((1, d), fixed),
            pl.BlockSpec(w_main.shape, fixed),
            pl.BlockSpec(w_ff.shape, fixed),
            pl.BlockSpec((1, LANES), fixed),
            pl.BlockSpec((tm, LANES), row),
            pl.BlockSpec((tm, LANES), row),
        ],
        out_specs=out_specs,
        out_shape=out_shape,
        compiler_params=_params("parallel"),
        name="inproj",
    )(h, g, w_main, w_ff, b_ff, cos_t, sin_t)


def _cumsum_kernel(seg, x_ref, c_ref, ct_ref, carry_ref):
    tb = x_ref.shape[0]
    row = lax.broadcasted_iota(jnp.int32, (tb, tb), 0)
    col = lax.broadcasted_iota(jnp.int32, (tb, tb), 1)
    keep = col <= row
    carried = seg > tb
    if not carried:
        keep = keep & ((row // seg) == (col // seg))
    tri = jnp.where(keep, 1.0, 0.0).astype(BF16)
    c = _select_sum(tri, x_ref[...], NN_DIMS)
    if carried:
        @pl.when(pl.program_id(0) == 0)
        def _():
            carry_ref[...] = jnp.zeros_like(carry_ref)

        c = c + carry_ref[...]
        carry_ref[...] = c[tb - 1:tb, :]
    c_ref[...] = c
    ct_ref[...] = c.T


def _cumsum(x, seg):
    t = x.shape[0]
    tb = _pick_tile(t, 512, LANES)
    assert seg % tb == 0 or tb % seg == 0
    return pl.pallas_call(
        functools.partial(_cumsum_kernel, seg),
        grid=(t // tb,),
        in_specs=[pl.BlockSpec((tb, LANES), lambda i: (i, 0))],
        out_specs=[pl.BlockSpec((tb, LANES), lambda i: (i, 0)),
                   pl.BlockSpec((LANES, tb), lambda i: (0, i))],
        out_shape=[jax.ShapeDtypeStruct((t, LANES), F32), jax.ShapeDtypeStruct((LANES, t), F32)],
        scratch_shapes=[pltpu.VMEM((1, LANES), F32)],
        compiler_params=_params("arbitrary"),
        name="forget_cumsum",
    )(x)


def _cumsum_pack_kernel(x_ref, qt_ref, kh_ref, stats_ref, qa_ref, ka_ref, skip_ref, carry_ref, kn_ref, cmin_ref):
    i = pl.program_id(0)
    tb = x_ref.shape[0]
    row = lax.broadcasted_iota(jnp.int32, (tb, tb), 0)
    col = lax.broadcasted_iota(jnp.int32, (tb, tb), 1)
    tri = jnp.where(col <= row, 1.0, 0.0).astype(BF16)

    @pl.when(i == 0)
    def _():
        carry_ref[...] = jnp.zeros_like(carry_ref)
        kn_ref[...] = jnp.zeros_like(kn_ref)
        cmin_ref[...] = jnp.zeros_like(cmin_ref)

    c_nat = _select_sum(tri, x_ref[...], NN_DIMS) + carry_ref[...]
    carry_ref[...] = c_nat[tb - 1:tb, :]
    c = c_nat * LOG2E
    ct = c.T
    sub = lax.broadcasted_iota(jnp.int32, (BIAS_ROWS, tb), 0)
    lane = lax.broadcasted_iota(jnp.int32, (tb, HEAD_DIM), 1)
    for h in range(N_HEADS):
        hi, mid, lo = _split3(ct[h:h + 1, :])
        extra = jnp.where(sub == 0, hi, jnp.where(sub == 1, mid, jnp.where(sub == 2, lo,
                                                                          jnp.where(sub < 6, 1.0, 0.0))))
        qa_ref[h, 0:HEAD_DIM, :] = qt_ref[h]
        qa_ref[h, HEAD_DIM:HEAD_DIM + BIAS_ROWS, :] = extra.astype(BF16)
        qa_ref[h, HEAD_DIM + BIAS_ROWS:, :] = jnp.zeros((LANES - HEAD_DIM - BIAS_ROWS, tb), BF16)
        hi, mid, lo = _split3(c[:, h:h + 1])
        extra = jnp.where(lane < 3, 1.0, jnp.where(lane == 3, -hi, jnp.where(lane == 4, -mid,
                                                                             jnp.where(lane == 5, -lo, 0.0))))
        ka_ref[h, :, 0:HEAD_DIM] = kh_ref[h]
        ka_ref[h, :, HEAD_DIM:] = extra.astype(BF16)
    stats = stats_ref[0]
    d_min, kn, qn = stats[0:1, :], stats[1:2, :], stats[2:3, :]
    c_max = jnp.max(c, axis=0, keepdims=True)
    c_min = jnp.min(c, axis=0, keepdims=True)
    kn_ref[pl.ds(i, 1), :] = kn
    cmin_ref[pl.ds(i, 1), :] = c_min
    bound = kn_ref[...] * qn - d_min + c_max - cmin_ref[...]
    skip_ref[0] = jnp.where(bound <= SKIP_LOG2, 1, 0).astype(jnp.int32)


def _cumsum_pack(x, qt, kh, stats):
    t = x.shape[0]
    nb = stats.shape[0]
    tb = t // nb
    nb_pad = -(-nb // 8) * 8
    return pl.pallas_call(
        _cumsum_pack_kernel,
        grid=(nb,),
        in_specs=[pl.BlockSpec((tb, LANES), lambda i: (i, 0)),
                  pl.BlockSpec((N_HEADS, HEAD_DIM, tb), lambda i: (0, 0, i)),
                  pl.BlockSpec((N_HEADS, tb, HEAD_DIM), lambda i: (0, i, 0)),
                  pl.BlockSpec((1, N_HEADS, LANES), lambda i: (i, 0, 0))],
        out_specs=[pl.BlockSpec((N_HEADS, LANES, tb), lambda i: (0, 0, i)),
                   pl.BlockSpec((N_HEADS, tb, LANES), lambda i: (0, i, 0)),
                   pl.BlockSpec((1, nb_pad, LANES), lambda i: (i, 0, 0))],
        out_shape=[jax.ShapeDtypeStruct((N_HEADS, LANES, t), BF16),
                   jax.ShapeDtypeStruct((N_HEADS, t, LANES), BF16),
                   jax.ShapeDtypeStruct((nb, nb_pad, LANES), jnp.int32)],
        scratch_shapes=[pltpu.VMEM((1, LANES), F32), pltpu.VMEM((nb_pad, LANES), F32),
                        pltpu.VMEM((nb_pad, LANES), F32)],
        compiler_params=_params("arbitrary"),
        name="forget_cumsum_pack",
    )(x, qt, kh, stats)


def _ret_kernel(q_ref, k_ref, v_ref, g_ref, s0_ref, intra_ref, qdec_ref, kdec_ref, cdec_ref,
                gnw_ref, gnb_ref, o_ref, sout_ref, s_ref):
    c = pl.program_id(1)

    @pl.when(c == 0)
    def _():
        s_ref[...] = s0_ref[0]

    mm = BF16 if q_ref.shape[0] >= 16 else F32
    q = q_ref[...].astype(F32)
    k = k_ref[...].astype(F32)
    qb = q.astype(mm)
    kb = k.astype(mm)
    vb = v_ref[...].astype(mm)
    qd = (q * qdec_ref[...]).astype(mm)
    kd = (k * kdec_ref[...]).astype(mm)
    heads = [slice(h * HEAD_DIM, (h + 1) * HEAD_DIM) for h in range(N_HEADS)]
    atts = [lax.dot_general(qb[:, sl], kb[:, sl], NT_DIMS, preferred_element_type=F32) for sl in heads]
    inters = [_dot(qd[:, sl], s_ref[h].astype(mm)) for h, sl in enumerate(heads)]
    for h, sl in enumerate(heads):
        s_ref[h] = s_ref[h] * cdec_ref[h] + lax.dot_general(kd[:, sl], vb[:, sl], TN_DIMS,
                                                            preferred_element_type=F32)
    outs = []
    for h, sl in enumerate(heads):
        o = _dot((atts[h] * intra_ref[h]).astype(mm), vb[:, sl]) + inters[h]
        mu = jnp.mean(o, axis=-1, keepdims=True)
        dev = o - mu
        var = jnp.mean(dev * dev, axis=-1, keepdims=True)
        outs.append(dev * lax.rsqrt(var + GN_EPS))
    on = jnp.concatenate(outs, axis=1)
    gate = g_ref[...]
    o_ref[...] = ((on * gnw_ref[...] + gnb_ref[...]) * (gate * jax.nn.sigmoid(gate))).astype(o_ref.dtype)

    @pl.when(c == pl.num_programs(1) - 1)
    def _():
        sout_ref[0] = s_ref[...]


def _ret_tables(chunk):
    lg = jnp.log1p(-(2.0 ** (-5.0 - jnp.arange(N_HEADS, dtype=F32))))
    i = jnp.arange(chunk, dtype=F32)
    diff = i[:, None] - i[None, :]
    intra = jnp.where(diff[None] >= 0, jnp.exp(jnp.maximum(diff, 0.0)[None] * lg[:, None, None]), 0.0)
    q_dec = jnp.exp((i[:, None] + 1.0) * lg[None, :])
    k_dec = jnp.exp((chunk - 1.0 - i)[:, None] * lg[None, :])
    c_dec = jnp.exp(chunk * lg)
    expand = lambda a: jnp.repeat(a, HEAD_DIM, axis=1)
    return intra, expand(q_dec), expand(k_dec), jnp.broadcast_to(c_dec[:, None, None], (N_HEADS, 1, HEAD_DIM))


def _retention(q, k, v, g, s0, gn_w, gn_b, seq, out_dtype):
    t = q.shape[0]
    batch = t // seq
    chunk = min(RET_CHUNK, seq)
    n = seq // chunk
    intra, q_dec, k_dec, c_dec = _ret_tables(chunk)
    blk = pl.BlockSpec((chunk, D_GROUP), lambda b, c: (b * n + c, 0))
    fixed2 = lambda b, c: (0, 0)
    fixed3 = lambda b, c: (0, 0, 0)
    state = pl.BlockSpec((1, N_HEADS, HEAD_DIM, HEAD_DIM), lambda b, c: (b, 0, 0, 0))
    return pl.pallas_call(
        _ret_kernel,
        grid=(batch, n),
        in_specs=[blk, blk, blk, blk, state,
                  pl.BlockSpec(intra.shape, fixed3),
                  pl.BlockSpec(q_dec.shape, fixed2),
                  pl.BlockSpec(k_dec.shape, fixed2),
                  pl.BlockSpec(c_dec.shape, fixed3),
                  pl.BlockSpec((1, D_GROUP), fixed2),
                  pl.BlockSpec((1, D_GROUP), fixed2)],
        out_specs=[blk, state],
        out_shape=[jax.ShapeDtypeStruct((t, D_GROUP), out_dtype),
                   jax.ShapeDtypeStruct((batch, N_HEADS, HEAD_DIM, HEAD_DIM), F32)],
        scratch_shapes=[pltpu.VMEM((N_HEADS, HEAD_DIM, HEAD_DIM), F32)],
        compiler_params=_params("parallel", "arbitrary"),
        name="retention",
    )(q, k, v, g, s0, intra, q_dec, k_dec, c_dec, gn_w, gn_b)


HEAD_GROUP = 2


def _fox_prompt_kernel(n_kblocks, qi_ref, kj_ref, skip_ref, qa_ref, ka_ref, vt_ref, o_ref, m_ref, acc_ref):
    step = pl.program_id(0)
    i = qi_ref[step]
    j = kj_ref[step]
    tk = ka_ref.shape[1]
    tq = qa_ref.shape[2]
    n_groups = N_HEADS // HEAD_GROUP

    @pl.when(j == 0)
    def _():
        m_ref[...] = jnp.full_like(m_ref, NEG_INF)
        acc_ref[...] = jnp.zeros_like(acc_ref)

    def scores(h):
        return _dot(ka_ref[h], qa_ref[h])

    def update(heads, causal):
        s_next = scores(heads[0])
        for n, h in enumerate(heads):
            s = s_next
            if n + 1 < len(heads):
                s_next = scores(heads[n + 1])
            if causal is not None:
                s = jnp.where(causal, s, NEG_INF)
            m_prev = m_ref[h]
            m_new = jnp.maximum(m_prev, jnp.max(s, axis=0, keepdims=True))
            p = jnp.exp2(s - m_new).astype(BF16)
            acc_ref[h] = jnp.exp2(m_prev - m_new) * acc_ref[h] + _dot(vt_ref[h], p)
            m_ref[h] = m_new

    @pl.when(j < i)
    def _():
        flags = (i * n_kblocks + j) * N_HEADS
        for grp in range(n_groups):
            heads = list(range(grp * HEAD_GROUP, (grp + 1) * HEAD_GROUP))
            skippable = functools.reduce(jnp.minimum, [skip_ref[flags + h] for h in heads])

            @pl.when(skippable == 0)
            def _():
                update(heads, None)

    @pl.when(j == i)
    def _():
        causal = (lax.broadcasted_iota(jnp.int32, (tk, tq), 0)
                  <= lax.broadcasted_iota(jnp.int32, (tk, tq), 1))
        update(list(range(N_HEADS)), causal)
        for h in range(N_HEADS):
            acc = acc_ref[h]
            o_ref[h] = (acc[:HEAD_DIM] / acc[HEAD_DIM:HEAD_DIM + 1]).astype(o_ref.dtype)


def _fox_prompt(qa, ka, vt, skip):
    t = ka.shape[1]
    nq = skip.shape[0]
    tq = t // nq
    qi = np.array([i for i in range(nq) for _ in range(i + 1)], np.int32)
    kj = np.array([j for i in range(nq) for j in range(i + 1)], np.int32)
    v_rows = vt.shape[1]
    grid_spec = pltpu.PrefetchScalarGridSpec(
        num_scalar_prefetch=3,
        grid=(len(qi),),
        in_specs=[pl.BlockSpec((N_HEADS, LANES, tq), lambda s, qi, kj, sk: (0, 0, qi[s])),
                  pl.BlockSpec((N_HEADS, tq, LANES), lambda s, qi, kj, sk: (0, kj[s], 0)),
                  pl.BlockSpec((N_HEADS, v_rows, tq), lambda s, qi, kj, sk: (0, 0, kj[s]))],
        out_specs=pl.BlockSpec((N_HEADS, HEAD_DIM, tq), lambda s, qi, kj, sk: (0, 0, qi[s])),
        scratch_shapes=[pltpu.VMEM((N_HEADS, 1, tq), F32), pltpu.VMEM((N_HEADS, v_rows, tq), F32)],
    )
    return pl.pallas_call(
        functools.partial(_fox_prompt_kernel, skip.shape[1]),
        grid_spec=grid_spec,
        out_shape=jax.ShapeDtypeStruct((N_HEADS, HEAD_DIM, t), BF16),
        compiler_params=_params("arbitrary"),
        name="fox_prompt",
    )(jnp.asarray(qi), jnp.asarray(kj), skip[:, :, :N_HEADS].reshape(-1), qa, ka, vt)


def _fox_decode_kernel(n_group, n_pages, pt_ref, q_ref, kn_ref, vn_ref, cn_ref, lt_hbm, kt_hbm, vt_hbm,
                       o_ref, lt_buf, kt_buf, vt_buf, sem, qbd_ref, m_ref, l_ref, acc_ref, carry_ref):
    b = pl.program_id(0)
    g = pl.program_id(1)
    n_steps = pl.num_programs(1)
    step = b * n_steps + g
    slot = step % 2
    n_q = q_ref.shape[0]
    rows = n_q * N_HEADS
    page = kt_buf.shape[3]
    cn = cn_ref[0]

    def page_copies(of_step, into_slot):
        seq = of_step // n_steps
        first = seq * n_pages + (n_steps - 1 - of_step % n_steps) * n_group
        copies = []
        for r in range(n_group):
            pid = pt_ref[first + r]
            copies.append(pltpu.make_async_copy(lt_hbm.at[pid], lt_buf.at[into_slot, r], sem.at[0, into_slot]))
            copies.append(pltpu.make_async_copy(kt_hbm.at[pid], kt_buf.at[into_slot, r], sem.at[1, into_slot]))
            copies.append(pltpu.make_async_copy(vt_hbm.at[pid], vt_buf.at[into_slot, r], sem.at[2, into_slot]))
        return copies

    @pl.when(step == 0)
    def _():
        for c in page_copies(step, slot):
            c.start()

    @pl.when(step + 1 < pl.num_programs(0) * n_steps)
    def _():
        for c in page_copies(step + 1, 1 - slot):
            c.start()

    for c in page_copies(step, slot):
        c.wait()
    lt_pages = [lt_buf[slot, r] for r in range(n_group)]

    def attend(s, v, v_dims):
        m_prev = m_ref[...]
        m_new = jnp.maximum(m_prev, jnp.max(s, axis=1, keepdims=True))
        alpha = jnp.exp(m_prev - m_new)
        p = jnp.exp(s - m_new)
        l_ref[...] = alpha * l_ref[...] + jnp.sum(p, axis=1, keepdims=True)
        acc_ref[...] = alpha * acc_ref[...] + lax.dot_general(p.astype(BF16), v, v_dims,
                                                              preferred_element_type=F32)
        m_ref[...] = m_new

    @pl.when(g == 0)
    def _():
        q = q_ref[...]
        rep = jnp.concatenate([jnp.broadcast_to(q[t:t + 1, :], (N_HEADS, D_GROUP)) for t in range(n_q)], axis=0)
        r_head = lax.broadcasted_iota(jnp.int32, (rows, D_GROUP), 0) % N_HEADS
        l_head = lax.broadcasted_iota(jnp.int32, (rows, D_GROUP), 1) // HEAD_DIM
        qbd_ref[...] = jnp.where(r_head == l_head, rep, 0.0).astype(BF16)
        m_ref[...] = jnp.full_like(m_ref, NEG_INF)
        l_ref[...] = jnp.zeros_like(l_ref)
        acc_ref[...] = jnp.zeros_like(acc_ref)
        carry_ref[...] = jnp.zeros_like(carry_ref)
        pad = jnp.zeros((page - n_q, D_GROUP), F32)
        k_new = jnp.concatenate([kn_ref[...], pad], axis=0).astype(BF16)
        v_new = jnp.concatenate([vn_ref[...], pad], axis=0).astype(BF16)
        s = lax.dot_general(qbd_ref[...], k_new, NT_DIMS, preferred_element_type=F32)
        key_pos = lax.broadcasted_iota(jnp.int32, (N_HEADS, page), 1)
        parts = []
        for t in range(n_q):
            s_t = s[t * N_HEADS:(t + 1) * N_HEADS, :] + cn[:, t:t + 1] - cn
            parts.append(jnp.where(key_pos <= t, s_t, NEG_INF))
        attend(jnp.concatenate(parts, axis=0), v_new, NN_DIMS)

    x = jnp.concatenate(lt_pages, axis=0)
    later = (lax.broadcasted_iota(jnp.int32, (page, page), 0)
             > lax.broadcasted_iota(jnp.int32, (page, page), 1))
    d_loc = _select_sum(jnp.where(later, 1.0, 0.0).astype(BF16), x, NN_DIMS, mask_first=False)
    tot = jnp.sum(x, axis=1, keepdims=True)
    run = carry_ref[...]
    d_pages = [None] * n_group
    for r in reversed(range(n_group)):
        d_pages[r] = d_loc[r * N_HEADS:(r + 1) * N_HEADS, :] + run
        run = run + tot[r * N_HEADS:(r + 1) * N_HEADS, :]
    carry_ref[...] = run
    n_sub = DECODE_SUBGROUPS if n_group % DECODE_SUBGROUPS == 0 else 1
    per_sub = n_group // n_sub
    subs = [list(range(a * per_sub, (a + 1) * per_sub)) for a in range(n_sub)]
    qbd = qbd_ref[...]

    def sub_scores(pages):
        return _dot(qbd, jnp.concatenate([kt_buf[slot, r].astype(BF16) for r in pages], axis=1))

    s_next = sub_scores(subs[0])
    for n, pages in enumerate(subs):
        s = s_next
        if n + 1 < n_sub:
            s_next = sub_scores(subs[n + 1])
        d_sub = jnp.concatenate([d_pages[r] for r in pages], axis=1)
        parts = [s[t * N_HEADS:(t + 1) * N_HEADS, :] + d_sub + cn[:, t:t + 1] for t in range(n_q)]
        vt_sub = jnp.concatenate([vt_buf[slot, r].astype(BF16) for r in pages], axis=1)
        attend(jnp.concatenate(parts, axis=0), vt_sub, NT_DIMS)

    @pl.when(g == pl.num_programs(1) - 1)
    def _():
        o = acc_ref[...] / l_ref[...]
        r_head = lax.broadcasted_iota(jnp.int32, (rows, D_GROUP), 0) % N_HEADS
        l_head = lax.broadcasted_iota(jnp.int32, (rows, D_GROUP), 1) // HEAD_DIM
        o = jnp.where(r_head == l_head, o, 0.0)
        o_ref[...] = jnp.concatenate(
            [jnp.sum(o[t * N_HEADS:(t + 1) * N_HEADS, :], axis=0, keepdims=True) for t in range(n_q)], axis=0)


def _fox_decode(fq, fk, fv, cn, cache_kt, cache_vt, cache_lt, page_table, n_q):
    batch, n_pages = page_table.shape
    n_pool, _, page = cache_kt.shape
    n_group = _pick_tile(n_pages, 16, 1)
    n_steps = n_pages // n_group
    rows = n_q * N_HEADS

    per_seq = pl.BlockSpec((n_q, D_GROUP), lambda b, g, pt: (b, 0))
    in_hbm = pl.BlockSpec(memory_space=pl.ANY)
    grid_spec = pltpu.PrefetchScalarGridSpec(
        num_scalar_prefetch=1,
        grid=(batch, n_steps),
        in_specs=[per_seq, per_seq, per_seq, pl.BlockSpec((1, N_HEADS, LANES), lambda b, g, pt: (b, 0, 0)),
                  in_hbm, in_hbm, in_hbm],
        out_specs=per_seq,
        scratch_shapes=[pltpu.VMEM((2, n_group, N_HEADS, page), F32),
                        pltpu.VMEM((2, n_group, D_GROUP, page), F32),
                        pltpu.VMEM((2, n_group, D_GROUP, page), F32),
                        pltpu.SemaphoreType.DMA((3, 2)),
                        pltpu.VMEM((rows, D_GROUP), BF16), pltpu.VMEM((rows, 1), F32),
                        pltpu.VMEM((rows, 1), F32), pltpu.VMEM((rows, D_GROUP), F32),
                        pltpu.VMEM((N_HEADS, 1), F32)],
    )
    return pl.pallas_call(
        functools.partial(_fox_decode_kernel, n_group, n_pages),
        grid_spec=grid_spec,
        out_shape=jax.ShapeDtypeStruct((batch * n_q, D_GROUP), F32),
        compiler_params=_params("arbitrary", "arbitrary"),
        name="fox_decode",
    )(page_table.reshape(-1), fq, fk, fv, cn, cache_lt, cache_kt, cache_vt)


def _outproj_kernel(fox_transposed, h_ref, a_ref, b_ref, w_ref, o_ref):
    half = a_ref.shape[1]
    b = b_ref[...].astype(F32).T if fox_transposed else b_ref[...]
    mix = _dot(a_ref[...].astype(BF16), w_ref[:half, :]) + _dot(b.astype(BF16), w_ref[half:, :])
    o_ref[...] = h_ref[...] + mix


def _outproj(h, o_ret, o_fox, w_out, fox_transposed):
    t, d = h.shape
    tm = _pick_tile(t, 1024, LANES if fox_transposed else 8)
    row = lambda i: (i, 0)
    fox_spec = pl.BlockSpec((D_GROUP, tm), lambda i: (0, i)) if fox_transposed else pl.BlockSpec((tm, D_GROUP), row)
    return pl.pallas_call(
        functools.partial(_outproj_kernel, fox_transposed),
        grid=(t // tm,),
        in_specs=[pl.BlockSpec((tm, d), row), pl.BlockSpec((tm, D_GROUP), row),
                  fox_spec, pl.BlockSpec(w_out.shape, lambda i: (0, 0))],
        out_specs=pl.BlockSpec((tm, d), row),
        out_shape=jax.ShapeDtypeStruct((t, d), F32),
        compiler_params=_params("parallel"),
        name="outproj",
    )(h, o_ret, o_fox, w_out)


def _ple_kernel(final, h_ref, p_ref, g_ref, wg_ref, bg_ref, wp_ref, gf_ref, o_ref):
    h = h_ref[...]
    gate = jax.nn.sigmoid(_dot(_rms(h, g_ref[...]).astype(BF16), wg_ref[...]) + bg_ref[...])
    h = h + gate * _dot(p_ref[...].astype(BF16), wp_ref[...])
    o_ref[...] = _rms(h, gf_ref[...]) if final else h


def _ple(h, p, g, w_gate, b_gate, w_proj, g_final, final):
    t, d = h.shape
    tm = _pick_tile(t, 1024, 8)
    row = lambda i: (i, 0)
    fixed = lambda i: (0, 0)
    return pl.pallas_call(
        functools.partial(_ple_kernel, final),
        grid=(t // tm,),
        in_specs=[pl.BlockSpec((tm, d), row), pl.BlockSpec((tm, p.shape[1]), row),
                  pl.BlockSpec((1, d), fixed), pl.BlockSpec(w_gate.shape, fixed),
                  pl.BlockSpec((1, d), fixed), pl.BlockSpec(w_proj.shape, fixed),
                  pl.BlockSpec((1, d), fixed)],
        out_specs=pl.BlockSpec((tm, d), row),
        out_shape=jax.ShapeDtypeStruct((t, d), F32),
        compiler_params=_params("parallel"),
        name="ple_gate",
    )(h, p, g, w_gate, b_gate, w_proj, g_final)


def _rope_tables(pos):
    half = HEAD_DIM // 2
    inv = ROPE_BASE ** (-jnp.arange(half, dtype=F32) / half)
    ang = pos.astype(F32)[:, None] * inv[None, :]
    cos = jnp.cos(ang)
    sin = jnp.sin(ang)
    return jnp.tile(cos, (1, LANES // half)), jnp.tile(jnp.concatenate([-sin, sin], axis=1), (1, LANES // HEAD_DIM))


def _row(v):
    return v.reshape(1, -1)


def _layer(x, p, pos, seq, s0, past, lw, g_final, final):
    t = x.shape[0]
    batch = t // seq
    prompt = past is None
    cos_t, sin_t = _rope_tables(pos)
    h = _ffn(x, _row(lw["g_ffn1"]), lw["w1_ffn1"], lw["w3_ffn1"], lw["w2_ffn1"])
    outs = _inproj(h, _row(lw["g_mix"]), lw["w_main"], lw["w_ff"], lw["b_ff"], cos_t, sin_t, prompt)
    rq, rk, rv, rg, fk, fv, lf = outs[:7]
    o_ret, s_ret = _retention(rq, rk, rv, rg, s0, _row(lw["gn_w"]), _row(lw["gn_b"]), seq,
                              BF16 if prompt else F32)
    if prompt:
        assert batch == 1, "the prompt path handles one sequence"
        qt, kh, vt, stats = outs[7:]
        qa, ka, skip = _cumsum_pack(lf, qt, kh, stats)
        o_fox = _fox_prompt(qa, ka, vt, skip).reshape(D_GROUP, t)
    else:
        (fq,) = outs[7:]
        _, ct = _cumsum(lf, seq)
        cache_kt, cache_vt, cache_lt, page_table = past
        cn = ct[:N_HEADS].reshape(N_HEADS, batch, seq).transpose(1, 0, 2)
        cn = jnp.pad(cn, ((0, 0), (0, 0), (0, LANES - seq)))
        o_fox = _fox_decode(fq, fk, fv, cn, cache_kt, cache_vt, cache_lt, page_table, seq)
    h = _outproj(h, o_ret, o_fox, lw["w_out"], prompt)
    h = _ffn(h, _row(lw["g_ffn2"]), lw["w1_ffn2"], lw["w3_ffn2"], lw["w2_ffn2"])
    h = _ple(h, p, _row(lw["g_ple"]), lw["w_ple_gate"], _row(lw["b_ple_gate"]), lw["w_ple_proj"],
             _row(g_final), final)
    return h, fk, fv, lf[:, :N_HEADS], s_ret


def kernel(x_prompt, x_sample, cache_fox_k, cache_fox_v, cache_fox_logf, state_ret, page_table, p_prompt, p_sample, g_ffn1, w1_ffn1, w3_ffn1, w2_ffn1, g_mix, w_in, b_forget, gn_w, gn_b, w_out, g_ffn2, w1_ffn2, w3_ffn2, w2_ffn2, g_ple, w_ple_gate, b_ple_gate, w_ple_proj, g_final):
    depth = w_in.shape[0]
    batch_p, seq_p, d = x_prompt.shape
    batch_s, seq_s, _ = x_sample.shape
    n_pages = page_table.shape[1]
    page = cache_fox_k.shape[2]
    n_pool = cache_fox_k.shape[1]
    n_main = w_in.shape[2] - N_HEADS
    hp = x_prompt.reshape(batch_p * seq_p, d)
    hs = x_sample.reshape(batch_s * seq_s, d)
    pos_p = jnp.tile(jnp.arange(seq_p), batch_p)
    pos_s = jnp.tile(n_pages * page + jnp.arange(seq_s), batch_s)
    per_layer = [[] for _ in range(8)]
    for i in range(depth):
        lw = {
            "g_ffn1": g_ffn1[i], "w1_ffn1": w1_ffn1[i].astype(BF16), "w3_ffn1": w3_ffn1[i].astype(BF16),
            "w2_ffn1": w2_ffn1[i].astype(BF16), "g_mix": g_mix[i],
            "w_main": w_in[i, :, :n_main].astype(BF16),
            "w_ff": jnp.pad(w_in[i, :, n_main:], ((0, 0), (0, LANES - N_HEADS))).astype(BF16),
            "b_ff": jnp.pad(b_forget[i], (0, LANES - N_HEADS)).reshape(1, LANES),
            "gn_w": gn_w[i], "gn_b": gn_b[i], "w_out": w_out[i].astype(BF16),
            "g_ffn2": g_ffn2[i], "w1_ffn2": w1_ffn2[i].astype(BF16), "w3_ffn2": w3_ffn2[i].astype(BF16),
            "w2_ffn2": w2_ffn2[i].astype(BF16), "g_ple": g_ple[i], "w_ple_gate": w_ple_gate[i].astype(BF16),
            "b_ple_gate": b_ple_gate[i], "w_ple_proj": w_ple_proj[i].astype(BF16),
        }
        final = i == depth - 1
        s0_p = jnp.zeros((batch_p, N_HEADS, HEAD_DIM, HEAD_DIM), F32)
        hp, kp, vp, lp, sp = _layer(hp, p_prompt[i].reshape(batch_p * seq_p, -1), pos_p, seq_p, s0_p, None,
                                    lw, g_final, final)
        past = (cache_fox_k[i].transpose(0, 2, 3, 1).reshape(n_pool, D_GROUP, page),
                cache_fox_v[i].transpose(0, 2, 3, 1).reshape(n_pool, D_GROUP, page),
                cache_fox_logf[i].transpose(0, 2, 1), page_table)
        hs, ks, vs, ls, ss = _layer(hs, p_sample[i].reshape(batch_s * seq_s, -1), pos_s, seq_s, state_ret[i],
                                    past, lw, g_final, final)
        for lst, val in zip(per_layer, (
                kp.reshape(batch_p, seq_p, N_HEADS, HEAD_DIM), vp.reshape(batch_p, seq_p, N_HEADS, HEAD_DIM),
                lp.reshape(batch_p, seq_p, N_HEADS), sp,
                ks.reshape(batch_s, seq_s, N_HEADS, HEAD_DIM), vs.reshape(batch_s, seq_s, N_HEADS, HEAD_DIM),
                ls.reshape(batch_s, seq_s, N_HEADS), ss)):
            lst.append(val)
    stacked = [jnp.stack(lst) for lst in per_layer]
    return (hp.reshape(batch_p, seq_p, d), hs.reshape(batch_s, seq_s, d), *stacked)
```

```python
import functools

import numpy as np
import jax
import jax.numpy as jnp
from jax import lax
from jax.experimental import pallas as pl
from jax.experimental.pallas import tpu as pltpu

F32 = jnp.float32
BF16 = jnp.bfloat16

HEAD_DIM = 64
N_HEADS = 8
D_GROUP = N_HEADS * HEAD_DIM
RET_CHUNK = 256
RET_SEQS_PER_STEP = 4
ROPE_BASE = 10000.0
NORM_EPS = 1e-6
GN_EPS = 1e-5
NEG_INF = -1e30
LANES = 128
LOG2E = 1.4426950408889634
BIAS_ROWS = 16
DECODE_SUBGROUPS = 2
FFN_SUBCHUNK = 512
NORM_MARGIN = 1.01
DIAG_SLACK = 2.0 ** -8
SKIP_LOG2 = -75.0
VMEM_LIMIT = 52 * 1024 * 1024

NN_DIMS = (((1,), (0,)), ((), ()))
NT_DIMS = (((1,), (1,)), ((), ()))
TN_DIMS = (((0,), (0,)), ((), ()))


def _pick_tile(n, pref, mult):
    best = None
    for t in range(mult, min(n, pref) + 1, mult):
        if n % t == 0:
            best = t
    return best if best is not None else n


def _params(*sem):
    return pltpu.CompilerParams(dimension_semantics=sem, vmem_limit_bytes=VMEM_LIMIT)


def _rms(x, g):
    r = lax.rsqrt(jnp.mean(x * x, axis=-1, keepdims=True) + NORM_EPS)
    return x * r * g


def _dot(a, b):
    return jnp.dot(a, b, preferred_element_type=F32)


def _split3(x):
    hi = x.astype(BF16).astype(F32)
    r = x - hi
    mid = r.astype(BF16).astype(F32)
    lo = (r - mid).astype(BF16).astype(F32)
    return hi, mid, lo


def _select_sum(mask, x, dims, mask_first=True):
    total = None
    for term in _split3(x):
        term = term.astype(BF16)
        part = lax.dot_general(*((mask, term) if mask_first else (term, mask)), dims, preferred_element_type=F32)
        total = part if total is None else total + part
    return total


def _ffn_kernel(x_ref, g_ref, w1_ref, w3_ref, w2_ref, o_ref):
    x = x_ref[...]
    xn = _rms(x, g_ref[...]).astype(BF16)
    tf = w1_ref.shape[1]
    bounds = [(lo, min(lo + FFN_SUBCHUNK, tf)) for lo in range(0, tf, FFN_SUBCHUNK)]

    def gate_up(lo, hi):
        return _dot(xn, w1_ref[:, lo:hi]), _dot(xn, w3_ref[:, lo:hi])

    nxt = gate_up(*bounds[0])
    total = None
    for n, (lo, hi) in enumerate(bounds):
        h1, h3 = nxt
        if n + 1 < len(bounds):
            nxt = gate_up(*bounds[n + 1])
        a = (h1 * jax.nn.sigmoid(h1) * h3).astype(BF16)
        part = _dot(a, w2_ref[lo:hi, :])
        total = part if total is None else total + part
    o_ref[...] = x + 0.5 * total


def _resident(shape):
    return pl.BlockSpec(shape, lambda i: (0,) * len(shape), pipeline_mode=pl.Buffered(1))


def _ffn(x, g, w1, w3, w2):
    t, d = x.shape
    tm = _pick_tile(t, 512, 8)
    row = lambda i: (i, 0)
    return pl.pallas_call(
        _ffn_kernel,
        grid=(t // tm,),
        in_specs=[pl.BlockSpec((tm, d), row), _resident((1, d)),
                  _resident(w1.shape), _resident(w3.shape), _resident(w2.shape)],
        out_specs=pl.BlockSpec((tm, d), row),
        out_shape=jax.ShapeDtypeStruct((t, d), F32),
        compiler_params=_params("parallel"),
        name="ffn",
    )(x, g, w1, w3, w2)


def _inproj_kernel(head_major, h_ref, g_ref, w_ref, wff_ref, bf_ref, cos_ref, sin_ref,
                   rq_ref, rk_ref, rv_ref, rg_ref, fk_ref, fv_ref, lf_ref, *q_refs):
    tm = h_ref.shape[0]
    u = _rms(h_ref[...], g_ref[...]).astype(BF16)

    def piece(p):
        return _dot(u, w_ref[:, p * D_GROUP:(p + 1) * D_GROUP])

    cos = cos_ref[...]
    sin = sin_ref[...]
    lane = lax.broadcasted_iota(jnp.int32, (tm, LANES), 1)
    first_half = (lane % HEAD_DIM) < (HEAD_DIM // 2)

    def rope(x):
        outs = []
        for cb in range(D_GROUP // LANES):
            xb = x[:, cb * LANES:(cb + 1) * LANES]
            partner = jnp.where(first_half,
                                pltpu.roll(xb, LANES - HEAD_DIM // 2, 1),
                                pltpu.roll(xb, HEAD_DIM // 2, 1))
            outs.append(xb * cos + partner * sin)
        return jnp.concatenate(outs, axis=1)

    scale = HEAD_DIM ** -0.5
    cur, nxt = piece(0), piece(1)
    rq_ref[...] = rope(cur).astype(rq_ref.dtype)
    cur, nxt = nxt, piece(2)
    rk_ref[...] = (rope(cur) * scale).astype(rk_ref.dtype)
    cur, nxt = nxt, piece(3)
    rv_ref[...] = cur.astype(rv_ref.dtype)
    cur, nxt = nxt, piece(4)
    rg_ref[...] = cur
    cur, nxt = nxt, piece(5)
    fq = cur * (scale * LOG2E if head_major else scale)
    if head_major:
        qt_ref, kh_ref, vt_ref, stats_ref = q_refs
        qt_ref[...] = fq.T.astype(BF16).reshape(N_HEADS, HEAD_DIM, tm)
    else:
        (fq_ref,) = q_refs
        fq_ref[...] = fq
    fk, nxt = nxt, piece(6)
    fk_ref[...] = fk
    if head_major:
        for h in range(N_HEADS):
            kh_ref[h] = fk[:, h * HEAD_DIM:(h + 1) * HEAD_DIM].astype(BF16)
        qb = fq.astype(BF16).astype(F32)
        kb = fk.astype(BF16).astype(F32)
        selector = (lax.broadcasted_iota(jnp.int32, (D_GROUP, LANES), 0) // HEAD_DIM
                    == lax.broadcasted_iota(jnp.int32, (D_GROUP, LANES), 1)).astype(BF16)
        per_head = lambda x: _dot(x.astype(BF16), selector)
        kn = jnp.sqrt(jnp.max(per_head(kb * kb), axis=0, keepdims=True)) * NORM_MARGIN
        qn = jnp.sqrt(jnp.max(per_head(qb * qb), axis=0, keepdims=True)) * NORM_MARGIN
        d_min = jnp.min(per_head(qb * kb), axis=0, keepdims=True) - DIAG_SLACK * kn * qn
        sub = lax.broadcasted_iota(jnp.int32, (N_HEADS, LANES), 0)
        stats_ref[0] = jnp.where(sub == 0, d_min, jnp.where(sub == 1, kn, qn))
    fv, ff = nxt, _dot(u, wff_ref[...])
    fv_ref[...] = fv
    if head_major:
        vt_ref[:, 0:HEAD_DIM, :] = fv.T.astype(BF16).reshape(N_HEADS, HEAD_DIM, tm)
        one_row = lax.broadcasted_iota(jnp.int32, (N_HEADS, BIAS_ROWS, tm), 1) == 0
        vt_ref[:, HEAD_DIM:, :] = jnp.where(one_row, 1.0, 0.0).astype(BF16)
    ff = ff + bf_ref[...]
    lf_ref[...] = jnp.minimum(ff, 0.0) - jnp.log1p(jnp.exp(-jnp.abs(ff)))


def _inproj(h, g, w_main, w_ff, b_ff, cos_t, sin_t, head_major):
    t, d = h.shape
    tm = _pick_tile(t, 512, 8)
    act = BF16 if head_major else F32
    row = lambda i: (i, 0)
    fixed = lambda i: (0, 0)
    wide = pl.BlockSpec((tm, D_GROUP), row)
    out_shape = [jax.ShapeDtypeStruct((t, D_GROUP), act)] * 3 + [
        jax.ShapeDtypeStruct((t, D_GROUP), F32)] * 3 + [jax.ShapeDtypeStruct((t, LANES), F32)]
    out_specs = [wide] * 6 + [pl.BlockSpec((tm, LANES), row)]
    if head_major:
        transposed = lambda rows: jax.ShapeDtypeStruct((N_HEADS, rows, t), BF16)
        transposed_spec = lambda rows: pl.BlockSpec((N_HEADS, rows, tm), lambda i: (0, 0, i))
        out_shape += [transposed(HEAD_DIM), jax.ShapeDtypeStruct((N_HEADS, t, HEAD_DIM), BF16),
                      transposed(HEAD_DIM + BIAS_ROWS)]
        out_specs += [transposed_spec(HEAD_DIM), pl.BlockSpec((N_HEADS, tm, HEAD_DIM), lambda i: (0, i, 0)),
                      transposed_spec(HEAD_DIM + BIAS_ROWS)]
        out_shape += [jax.ShapeDtypeStruct((t // tm, N_HEADS, LANES), F32)]
        out_specs += [pl.BlockSpec((1, N_HEADS, LANES), lambda i: (i, 0, 0))]
    else:
        out_shape += [jax.ShapeDtypeStruct((t, D_GROUP), F32)]
        out_specs += [wide]
    return pl.pallas_call(
        functools.partial(_inproj_kernel, head_major),
        grid=(t // tm,),
        in_specs=[
            pl.BlockSpec((tm, d), row),
            pl.BlockSpec((1, d), fixed),
            pl.BlockSpec(w_main.shape, fixed),
            pl.BlockSpec(w_ff.shape, fixed),
            pl.BlockSpec((1, LANES), fixed),
            pl.BlockSpec((tm, LANES), row),
            pl.BlockSpec((tm, LANES), row),
        ],
        out_specs=out_specs,
        out_shape=out_shape,
        compiler_params=_params("parallel"),
        name="inproj",
    )(h, g, w_main, w_ff, b_ff, cos_t, sin_t)


def _cumsum_kernel(seg, x_ref, c_ref, ct_ref, carry_ref):
    tb = x_ref.shape[0]
    row = lax.broadcasted_iota(jnp.int32, (tb, tb), 0)
    col = lax.broadcasted_iota(jnp.int32, (tb, tb), 1)
    keep = col <= row
    carried = seg > tb
    if not carried:
        keep = keep & ((row // seg) == (col // seg))
    tri = jnp.where(keep, 1.0, 0.0).astype(BF16)
    c = _select_sum(tri, x_ref[...], NN_DIMS)
    if carried:
        @pl.when(pl.program_id(0) == 0)
        def _():
            carry_ref[...] = jnp.zeros_like(carry_ref)

        c = c + carry_ref[...]
        carry_ref[...] = c[tb - 1:tb, :]
    c_ref[...] = c
    ct_ref[...] = c.T


def _cumsum(x, seg):
    t = x.shape[0]
    tb = _pick_tile(t, 512, LANES)
    assert seg % tb == 0 or tb % seg == 0
    return pl.pallas_call(
        functools.partial(_cumsum_kernel, seg),
        grid=(t // tb,),
        in_specs=[pl.BlockSpec((tb, LANES), lambda i: (i, 0))],
        out_specs=[pl.BlockSpec((tb, LANES), lambda i: (i, 0)),
                   pl.BlockSpec((LANES, tb), lambda i: (0, i))],
        out_shape=[jax.ShapeDtypeStruct((t, LANES), F32), jax.ShapeDtypeStruct((LANES, t), F32)],
        scratch_shapes=[pltpu.VMEM((1, LANES), F32)],
        compiler_params=_params("arbitrary"),
        name="forget_cumsum",
    )(x)


def _cumsum_pack_kernel(x_ref, qt_ref, kh_ref, stats_ref, qa_ref, ka_ref, skip_ref, carry_ref, kn_ref, cmin_ref):
    i = pl.program_id(0)
    tb = x_ref.shape[0]
    row = lax.broadcasted_iota(jnp.int32, (tb, tb), 0)
    col = lax.broadcasted_iota(jnp.int32, (tb, tb), 1)
    tri = jnp.where(col <= row, 1.0, 0.0).astype(BF16)

    @pl.when(i == 0)
    def _():
        carry_ref[...] = jnp.zeros_like(carry_ref)
        kn_ref[...] = jnp.zeros_like(kn_ref)
        cmin_ref[...] = jnp.zeros_like(cmin_ref)

    c_nat = _select_sum(tri, x_ref[...], NN_DIMS) + carry_ref[...]
    carry_ref[...] = c_nat[tb - 1:tb, :]
    c = c_nat * LOG2E
    ct = c.T
    sub = lax.broadcasted_iota(jnp.int32, (BIAS_ROWS, tb), 0)
    lane = lax.broadcasted_iota(jnp.int32, (tb, HEAD_DIM), 1)
    for h in range(N_HEADS):
        hi, mid, lo = _split3(ct[h:h + 1, :])
        extra = jnp.where(sub == 0, hi, jnp.where(sub == 1, mid, jnp.where(sub == 2, lo,
                                                                          jnp.where(sub < 6, 1.0, 0.0))))
        qa_ref[h, 0:HEAD_DIM, :] = qt_ref[h]
        qa_ref[h, HEAD_DIM:HEAD_DIM + BIAS_ROWS, :] = extra.astype(BF16)
        qa_ref[h, HEAD_DIM + BIAS_ROWS:, :] = jnp.zeros((LANES - HEAD_DIM - BIAS_ROWS, tb), BF16)
        hi, mid, lo = _split3(c[:, h:h + 1])
        extra = jnp.where(lane < 3, 1.0, jnp.where(lane == 3, -hi, jnp.where(lane == 4, -mid,
                                                                             jnp.where(lane == 5, -lo, 0.0))))
        ka_ref[h, :, 0:HEAD_DIM] = kh_ref[h]
        ka_ref[h, :, HEAD_DIM:] = extra.astype(BF16)
    stats = stats_ref[0]
    d_min, kn, qn = stats[0:1, :], stats[1:2, :], stats[2:3, :]
    c_max = jnp.max(c, axis=0, keepdims=True)
    c_min = jnp.min(c, axis=0, keepdims=True)
    kn_ref[pl.ds(i, 1), :] = kn
    cmin_ref[pl.ds(i, 1), :] = c_min
    bound = kn_ref[...] * qn - d_min + c_max - cmin_ref[...]
    skip_ref[0] = jnp.where(bound <= SKIP_LOG2, 1, 0).astype(jnp.int32)


def _cumsum_pack(x, qt, kh, stats):
    t = x.shape[0]
    nb = stats.shape[0]
    tb = t // nb
    nb_pad = -(-nb // 8) * 8
    return pl.pallas_call(
        _cumsum_pack_kernel,
        grid=(nb,),
        in_specs=[pl.BlockSpec((tb, LANES), lambda i: (i, 0)),
                  pl.BlockSpec((N_HEADS, HEAD_DIM, tb), lambda i: (0, 0, i)),
                  pl.BlockSpec((N_HEADS, tb, HEAD_DIM), lambda i: (0, i, 0)),
                  pl.BlockSpec((1, N_HEADS, LANES), lambda i: (i, 0, 0))],
        out_specs=[pl.BlockSpec((N_HEADS, LANES, tb), lambda i: (0, 0, i)),
                   pl.BlockSpec((N_HEADS, tb, LANES), lambda i: (0, i, 0)),
                   pl.BlockSpec((1, nb_pad, LANES), lambda i: (i, 0, 0))],
        out_shape=[jax.ShapeDtypeStruct((N_HEADS, LANES, t), BF16),
                   jax.ShapeDtypeStruct((N_HEADS, t, LANES), BF16),
                   jax.ShapeDtypeStruct((nb, nb_pad, LANES), jnp.int32)],
        scratch_shapes=[pltpu.VMEM((1, LANES), F32), pltpu.VMEM((nb_pad, LANES), F32),
                        pltpu.VMEM((nb_pad, LANES), F32)],
        compiler_params=_params("arbitrary"),
        name="forget_cumsum_pack",
    )(x, qt, kh, stats)


def _ret_kernel(q_ref, k_ref, v_ref, g_ref, s0_ref, intra_ref, qdec_ref, kdec_ref, cdec_ref,
                gnw_ref, gnb_ref, o_ref, sout_ref, s_ref):
    c = pl.program_id(1)
    n_seq = s0_ref.shape[0]
    chunk = q_ref.shape[0] // n_seq

    @pl.when(c == 0)
    def _():
        s_ref[...] = s0_ref[...]

    mm = BF16
    q = q_ref[...].astype(F32)
    k = k_ref[...].astype(F32)
    v = v_ref[...].astype(F32)
    units = [(s, h) for s in range(n_seq) for h in range(N_HEADS)]

    def part(x, s, h, scale_ref=None):
        x = x[s * chunk:(s + 1) * chunk, h * HEAD_DIM:(h + 1) * HEAD_DIM]
        if scale_ref is not None:
            x = x * scale_ref[:, h * HEAD_DIM:(h + 1) * HEAD_DIM]
        return x.astype(mm)

    atts = [lax.dot_general(part(q, s, h), part(k, s, h), NT_DIMS, preferred_element_type=F32) for s, h in units]
    inters = [_dot(part(q, s, h, qdec_ref), s_ref[s, h].astype(mm)) for s, h in units]
    for s, h in units:
        s_ref[s, h] = s_ref[s, h] * cdec_ref[h] + lax.dot_general(part(k, s, h, kdec_ref), part(v, s, h), TN_DIMS,
                                                                  preferred_element_type=F32)
    rows = []
    for s in range(n_seq):
        outs = []
        for h in range(N_HEADS):
            o = _dot((atts[s * N_HEADS + h] * intra_ref[h]).astype(mm), part(v, s, h)) + inters[s * N_HEADS + h]
            mu = jnp.mean(o, axis=-1, keepdims=True)
            dev = o - mu
            var = jnp.mean(dev * dev, axis=-1, keepdims=True)
            outs.append(dev * lax.rsqrt(var + GN_EPS))
        rows.append(jnp.concatenate(outs, axis=1))
    on = jnp.concatenate(rows, axis=0) if n_seq > 1 else rows[0]
    gate = g_ref[...]
    o_ref[...] = ((on * gnw_ref[...] + gnb_ref[...]) * (gate * jax.nn.sigmoid(gate))).astype(o_ref.dtype)

    @pl.when(c == pl.num_programs(1) - 1)
    def _():
        sout_ref[...] = s_ref[...]


def _ret_tables(chunk):
    lg = jnp.log1p(-(2.0 ** (-5.0 - jnp.arange(N_HEADS, dtype=F32))))
    i = jnp.arange(chunk, dtype=F32)
    diff = i[:, None] - i[None, :]
    intra = jnp.where(diff[None] >= 0, jnp.exp(jnp.maximum(diff, 0.0)[None] * lg[:, None, None]), 0.0)
    q_dec = jnp.exp((i[:, None] + 1.0) * lg[None, :])
    k_dec = jnp.exp((chunk - 1.0 - i)[:, None] * lg[None, :])
    c_dec = jnp.exp(chunk * lg)
    expand = lambda a: jnp.repeat(a, HEAD_DIM, axis=1)
    return intra, expand(q_dec), expand(k_dec), jnp.broadcast_to(c_dec[:, None, None], (N_HEADS, 1, HEAD_DIM))


def _retention(q, k, v, g, s0, gn_w, gn_b, seq, out_dtype):
    t = q.shape[0]
    batch = t // seq
    chunk = min(RET_CHUNK, seq)
    n = seq // chunk
    intra, q_dec, k_dec, c_dec = _ret_tables(chunk)
    n_seq = _pick_tile(batch, RET_SEQS_PER_STEP, 1) if n == 1 else 1
    blk = pl.BlockSpec((n_seq * chunk, D_GROUP), lambda b, c: (b * n + c, 0))
    fixed2 = lambda b, c: (0, 0)
    fixed3 = lambda b, c: (0, 0, 0)
    state = pl.BlockSpec((n_seq, N_HEADS, HEAD_DIM, HEAD_DIM), lambda b, c: (b, 0, 0, 0))
    return pl.pallas_call(
        _ret_kernel,
        grid=(batch // n_seq, n),
        in_specs=[blk, blk, blk, blk, state,
                  pl.BlockSpec(intra.shape, fixed3),
                  pl.BlockSpec(q_dec.shape, fixed2),
                  pl.BlockSpec(k_dec.shape, fixed2),
                  pl.BlockSpec(c_dec.shape, fixed3),
                  pl.BlockSpec((1, D_GROUP), fixed2),
                  pl.BlockSpec((1, D_GROUP), fixed2)],
        out_specs=[blk, state],
        out_shape=[jax.ShapeDtypeStruct((t, D_GROUP), out_dtype),
                   jax.ShapeDtypeStruct((batch, N_HEADS, HEAD_DIM, HEAD_DIM), F32)],
        scratch_shapes=[pltpu.VMEM((n_seq, N_HEADS, HEAD_DIM, HEAD_DIM), F32)],
        compiler_params=_params("parallel", "arbitrary"),
        name="retention",
    )(q, k, v, g, s0, intra, q_dec, k_dec, c_dec, gn_w, gn_b)


HEAD_GROUP = 2


def _fox_prompt_kernel(n_kblocks, qi_ref, kj_ref, skip_ref, qa_ref, ka_ref, vt_ref, o_ref, m_ref, acc_ref):
    step = pl.program_id(0)
    i = qi_ref[step]
    j = kj_ref[step]
    tk = ka_ref.shape[1]
    tq = qa_ref.shape[2]
    n_groups = N_HEADS // HEAD_GROUP

    @pl.when(j == 0)
    def _():
        m_ref[...] = jnp.full_like(m_ref, NEG_INF)
        acc_ref[...] = jnp.zeros_like(acc_ref)

    def scores(h):
        return _dot(ka_ref[h], qa_ref[h])

    def update(heads, causal):
        s_next = scores(heads[0])
        for n, h in enumerate(heads):
            s = s_next
            if n + 1 < len(heads):
                s_next = scores(heads[n + 1])
            if causal is not None:
                s = jnp.where(causal, s, NEG_INF)
            m_prev = m_ref[h]
            m_new = jnp.maximum(m_prev, jnp.max(s, axis=0, keepdims=True))
            p = jnp.exp2(s - m_new).astype(BF16)
            acc_ref[h] = jnp.exp2(m_prev - m_new) * acc_ref[h] + _dot(vt_ref[h], p)
            m_ref[h] = m_new

    @pl.when(j < i)
    def _():
        flags = (i * n_kblocks + j) * N_HEADS
        for grp in range(n_groups):
            heads = list(range(grp * HEAD_GROUP, (grp + 1) * HEAD_GROUP))
            skippable = functools.reduce(jnp.minimum, [skip_ref[flags + h] for h in heads])

            @pl.when(skippable == 0)
            def _():
                update(heads, None)

    @pl.when(j == i)
    def _():
        causal = (lax.broadcasted_iota(jnp.int32, (tk, tq), 0)
                  <= lax.broadcasted_iota(jnp.int32, (tk, tq), 1))
        update(list(range(N_HEADS)), causal)
        for h in range(N_HEADS):
            acc = acc_ref[h]
            o_ref[h] = (acc[:HEAD_DIM] / acc[HEAD_DIM:HEAD_DIM + 1]).astype(o_ref.dtype)


def _fox_prompt(qa, ka, vt, skip):
    t = ka.shape[1]
    nq = skip.shape[0]
    tq = t // nq
    qi = np.array([i for i in range(nq) for _ in range(i + 1)], np.int32)
    kj = np.array([j for i in range(nq) for j in range(i + 1)], np.int32)
    v_rows = vt.shape[1]
    grid_spec = pltpu.PrefetchScalarGridSpec(
        num_scalar_prefetch=3,
        grid=(len(qi),),
        in_specs=[pl.BlockSpec((N_HEADS, LANES, tq), lambda s, qi, kj, sk: (0, 0, qi[s])),
                  pl.BlockSpec((N_HEADS, tq, LANES), lambda s, qi, kj, sk: (0, kj[s], 0)),
                  pl.BlockSpec((N_HEADS, v_rows, tq), lambda s, qi, kj, sk: (0, 0, kj[s]))],
        out_specs=pl.BlockSpec((N_HEADS, HEAD_DIM, tq), lambda s, qi, kj, sk: (0, 0, qi[s])),
        scratch_shapes=[pltpu.VMEM((N_HEADS, 1, tq), F32), pltpu.VMEM((N_HEADS, v_rows, tq), F32)],
    )
    return pl.pallas_call(
        functools.partial(_fox_prompt_kernel, skip.shape[1]),
        grid_spec=grid_spec,
        out_shape=jax.ShapeDtypeStruct((N_HEADS, HEAD_DIM, t), BF16),
        compiler_params=_params("arbitrary"),
        name="fox_prompt",
    )(jnp.asarray(qi), jnp.asarray(kj), skip[:, :, :N_HEADS].reshape(-1), qa, ka, vt)


def _fox_decode_kernel(n_group, n_pages, pt_ref, q_ref, kn_ref, vn_ref, cn_ref, lt_hbm, kt_hbm, vt_hbm,
                       o_ref, lt_buf, kt_buf, vt_buf, sem, qbd_ref, m_ref, l_ref, acc_ref, carry_ref):
    b = pl.program_id(0)
    g = pl.program_id(1)
    n_steps = pl.num_programs(1)
    step = b * n_steps + g
    slot = step % 2
    n_q = q_ref.shape[0]
    rows = n_q * N_HEADS
    page = kt_buf.shape[3]
    cn = cn_ref[0]

    def page_copies(of_step, into_slot):
        seq = of_step // n_steps
        first = seq * n_pages + (n_steps - 1 - of_step % n_steps) * n_group
        copies = []
        for r in range(n_group):
            pid = pt_ref[first + r]
            copies.append(pltpu.make_async_copy(lt_hbm.at[pid], lt_buf.at[into_slot, r], sem.at[0, into_slot]))
            copies.append(pltpu.make_async_copy(kt_hbm.at[pid], kt_buf.at[into_slot, r], sem.at[1, into_slot]))
            copies.append(pltpu.make_async_copy(vt_hbm.at[pid], vt_buf.at[into_slot, r], sem.at[2, into_slot]))
        return copies

    @pl.when(step == 0)
    def _():
        for c in page_copies(step, slot):
            c.start()

    @pl.when(step + 1 < pl.num_programs(0) * n_steps)
    def _():
        for c in page_copies(step + 1, 1 - slot):
            c.start()

    for c in page_copies(step, slot):
        c.wait()
    lt_pages = [lt_buf[slot, r] for r in range(n_group)]

    def attend(s, v, v_dims):
        m_prev = m_ref[...]
        m_new = jnp.maximum(m_prev, jnp.max(s, axis=1, keepdims=True))
        alpha = jnp.exp(m_prev - m_new)
        p = jnp.exp(s - m_new)
        l_ref[...] = alpha * l_ref[...] + jnp.sum(p, axis=1, keepdims=True)
        acc_ref[...] = alpha * acc_ref[...] + lax.dot_general(p.astype(BF16), v, v_dims,
                                                              preferred_element_type=F32)
        m_ref[...] = m_new

    @pl.when(g == 0)
    def _():
        q = q_ref[...]
        rep = jnp.concatenate([jnp.broadcast_to(q[t:t + 1, :], (N_HEADS, D_GROUP)) for t in range(n_q)], axis=0)
        r_head = lax.broadcasted_iota(jnp.int32, (rows, D_GROUP), 0) % N_HEADS
        l_head = lax.broadcasted_iota(jnp.int32, (rows, D_GROUP), 1) // HEAD_DIM
        qbd_ref[...] = jnp.where(r_head == l_head, rep, 0.0).astype(BF16)
        m_ref[...] = jnp.full_like(m_ref, NEG_INF)
        l_ref[...] = jnp.zeros_like(l_ref)
        acc_ref[...] = jnp.zeros_like(acc_ref)
        carry_ref[...] = jnp.zeros_like(carry_ref)
        pad = jnp.zeros((page - n_q, D_GROUP), F32)
        k_new = jnp.concatenate([kn_ref[...], pad], axis=0).astype(BF16)
        v_new = jnp.concatenate([vn_ref[...], pad], axis=0).astype(BF16)
        s = lax.dot_general(qbd_ref[...], k_new, NT_DIMS, preferred_element_type=F32)
        key_pos = lax.broadcasted_iota(jnp.int32, (N_HEADS, page), 1)
        parts = []
        for t in range(n_q):
            s_t = s[t * N_HEADS:(t + 1) * N_HEADS, :] + cn[:, t:t + 1] - cn
            parts.append(jnp.where(key_pos <= t, s_t, NEG_INF))
        attend(jnp.concatenate(parts, axis=0), v_new, NN_DIMS)

    x = jnp.concatenate(lt_pages, axis=0)
    later = (lax.broadcasted_iota(jnp.int32, (page, page), 0)
             > lax.broadcasted_iota(jnp.int32, (page, page), 1))
    d_loc = _select_sum(jnp.where(later, 1.0, 0.0).astype(BF16), x, NN_DIMS, mask_first=False)
    tot = jnp.sum(x, axis=1, keepdims=True)
    run = carry_ref[...]
    d_pages = [None] * n_group
    for r in reversed(range(n_group)):
        d_pages[r] = d_loc[r * N_HEADS:(r + 1) * N_HEADS, :] + run
        run = run + tot[r * N_HEADS:(r + 1) * N_HEADS, :]
    carry_ref[...] = run
    n_sub = DECODE_SUBGROUPS if n_group % DECODE_SUBGROUPS == 0 else 1
    per_sub = n_group // n_sub
    subs = [list(range(a * per_sub, (a + 1) * per_sub)) for a in range(n_sub)]
    qbd = qbd_ref[...]

    def sub_scores(pages):
        return _dot(qbd, jnp.concatenate([kt_buf[slot, r].astype(BF16) for r in pages], axis=1))

    s_next = sub_scores(subs[0])
    for n, pages in enumerate(subs):
        s = s_next
        if n + 1 < n_sub:
            s_next = sub_scores(subs[n + 1])
        d_sub = jnp.concatenate([d_pages[r] for r in pages], axis=1)
        parts = [s[t * N_HEADS:(t + 1) * N_HEADS, :] + d_sub + cn[:, t:t + 1] for t in range(n_q)]
        vt_sub = jnp.concatenate([vt_buf[slot, r].astype(BF16) for r in pages], axis=1)
        attend(jnp.concatenate(parts, axis=0), vt_sub, NT_DIMS)

    @pl.when(g == pl.num_programs(1) - 1)
    def _():
        o = acc_ref[...] / l_ref[...]
        r_head = lax.broadcasted_iota(jnp.int32, (rows, D_GROUP), 0) % N_HEADS
        l_head = lax.broadcasted_iota(jnp.int32, (rows, D_GROUP), 1) // HEAD_DIM
        o = jnp.where(r_head == l_head, o, 0.0)
        o_ref[...] = jnp.concatenate(
            [jnp.sum(o[t * N_HEADS:(t + 1) * N_HEADS, :], axis=0, keepdims=True) for t in range(n_q)], axis=0)


def _fox_decode(fq, fk, fv, cn, cache_kt, cache_vt, cache_lt, page_table, n_q):
    batch, n_pages = page_table.shape
    n_pool, _, page = cache_kt.shape
    n_group = _pick_tile(n_pages, 16, 1)
    n_steps = n_pages // n_group
    rows = n_q * N_HEADS

    per_seq = pl.BlockSpec((n_q, D_GROUP), lambda b, g, pt: (b, 0))
    in_hbm = pl.BlockSpec(memory_space=pl.ANY)
    grid_spec = pltpu.PrefetchScalarGridSpec(
        num_scalar_prefetch=1,
        grid=(batch, n_steps),
        in_specs=[per_seq, per_seq, per_seq, pl.BlockSpec((1, N_HEADS, LANES), lambda b, g, pt: (b, 0, 0)),
                  in_hbm, in_hbm, in_hbm],
        out_specs=per_seq,
        scratch_shapes=[pltpu.VMEM((2, n_group, N_HEADS, page), F32),
                        pltpu.VMEM((2, n_group, D_GROUP, page), F32),
                        pltpu.VMEM((2, n_group, D_GROUP, page), F32),
                        pltpu.SemaphoreType.DMA((3, 2)),
                        pltpu.VMEM((rows, D_GROUP), BF16), pltpu.VMEM((rows, 1), F32),
                        pltpu.VMEM((rows, 1), F32), pltpu.VMEM((rows, D_GROUP), F32),
                        pltpu.VMEM((N_HEADS, 1), F32)],
    )
    return pl.pallas_call(
        functools.partial(_fox_decode_kernel, n_group, n_pages),
        grid_spec=grid_spec,
        out_shape=jax.ShapeDtypeStruct((batch * n_q, D_GROUP), F32),
        compiler_params=_params("arbitrary", "arbitrary"),
        name="fox_decode",
    )(page_table.reshape(-1), fq, fk, fv, cn, cache_lt, cache_kt, cache_vt)


def _outproj_kernel(fox_transposed, h_ref, a_ref, b_ref, w_ref, o_ref):
    half = a_ref.shape[1]
    b = b_ref[...].astype(F32).T if fox_transposed else b_ref[...]
    mix = _dot(a_ref[...].astype(BF16), w_ref[:half, :]) + _dot(b.astype(BF16), w_ref[half:, :])
    o_ref[...] = h_ref[...] + mix


def _outproj(h, o_ret, o_fox, w_out, fox_transposed):
    t, d = h.shape
    tm = _pick_tile(t, 1024, LANES if fox_transposed else 8)
    row = lambda i: (i, 0)
    fox_spec = pl.BlockSpec((D_GROUP, tm), lambda i: (0, i)) if fox_transposed else pl.BlockSpec((tm, D_GROUP), row)
    return pl.pallas_call(
        functools.partial(_outproj_kernel, fox_transposed),
        grid=(t // tm,),
        in_specs=[pl.BlockSpec((tm, d), row), pl.BlockSpec((tm, D_GROUP), row),
                  fox_spec, pl.BlockSpec(w_out.shape, lambda i: (0, 0))],
        out_specs=pl.BlockSpec((tm, d), row),
        out_shape=jax.ShapeDtypeStruct((t, d), F32),
        compiler_params=_params("parallel"),
        name="outproj",
    )(h, o_ret, o_fox, w_out)


def _ple_kernel(final, h_ref, p_ref, g_ref, wg_ref, bg_ref, wp_ref, gf_ref, o_ref):
    h = h_ref[...]
    gate = jax.nn.sigmoid(_dot(_rms(h, g_ref[...]).astype(BF16), wg_ref[...]) + bg_ref[...])
    h = h + gate * _dot(p_ref[...].astype(BF16), wp_ref[...])
    o_ref[...] = _rms(h, gf_ref[...]) if final else h


def _ple(h, p, g, w_gate, b_gate, w_proj, g_final, final):
    t, d = h.shape
    tm = _pick_tile(t, 1024, 8)
    row = lambda i: (i, 0)
    fixed = lambda i: (0, 0)
    return pl.pallas_call(
        functools.partial(_ple_kernel, final),
        grid=(t // tm,),
        in_specs=[pl.BlockSpec((tm, d), row), pl.BlockSpec((tm, p.shape[1]), row),
                  pl.BlockSpec((1, d), fixed), pl.BlockSpec(w_gate.shape, fixed),
                  pl.BlockSpec((1, d), fixed), pl.BlockSpec(w_proj.shape, fixed),
                  pl.BlockSpec((1, d), fixed)],
        out_specs=pl.BlockSpec((tm, d), row),
        out_shape=jax.ShapeDtypeStruct((t, d), F32),
        compiler_params=_params("parallel"),
        name="ple_gate",
    )(h, p, g, w_gate, b_gate, w_proj, g_final)


def _rope_tables(pos):
    half = HEAD_DIM // 2
    inv = ROPE_BASE ** (-jnp.arange(half, dtype=F32) / half)
    ang = pos.astype(F32)[:, None] * inv[None, :]
    cos = jnp.cos(ang)
    sin = jnp.sin(ang)
    return jnp.tile(cos, (1, LANES // half)), jnp.tile(jnp.concatenate([-sin, sin], axis=1), (1, LANES // HEAD_DIM))


def _row(v):
    return v.reshape(1, -1)


def _layer(x, p, pos, seq, s0, past, lw, g_final, final):
    t = x.shape[0]
    batch = t // seq
    prompt = past is None
    cos_t, sin_t = _rope_tables(pos)
    h = _ffn(x, _row(lw["g_ffn1"]), lw["w1_ffn1"], lw["w3_ffn1"], lw["w2_ffn1"])
    outs = _inproj(h, _row(lw["g_mix"]), lw["w_main"], lw["w_ff"], lw["b_ff"], cos_t, sin_t, prompt)
    rq, rk, rv, rg, fk, fv, lf = outs[:7]
    o_ret, s_ret = _retention(rq, rk, rv, rg, s0, _row(lw["gn_w"]), _row(lw["gn_b"]), seq,
                              BF16 if prompt else F32)
    if prompt:
        assert batch == 1, "the prompt path handles one sequence"
        qt, kh, vt, stats = outs[7:]
        qa, ka, skip = _cumsum_pack(lf, qt, kh, stats)
        o_fox = _fox_prompt(qa, ka, vt, skip).reshape(D_GROUP, t)
    else:
        (fq,) = outs[7:]
        _, ct = _cumsum(lf, seq)
        cache_kt, cache_vt, cache_lt, page_table = past
        cn = ct[:N_HEADS].reshape(N_HEADS, batch, seq).transpose(1, 0, 2)
        cn = jnp.pad(cn, ((0, 0), (0, 0), (0, LANES - seq)))
        o_fox = _fox_decode(fq, fk, fv, cn, cache_kt, cache_vt, cache_lt, page_table, seq)
    h = _outproj(h, o_ret, o_fox, lw["w_out"], prompt)
    h = _ffn(h, _row(lw["g_ffn2"]), lw["w1_ffn2"], lw["w3_ffn2"], lw["w2_ffn2"])
    h = _ple(h, p, _row(lw["g_ple"]), lw["w_ple_gate"], _row(lw["b_ple_gate"]), lw["w_ple_proj"],
             _row(g_final), final)
    return h, fk, fv, lf[:, :N_HEADS], s_ret


def kernel(x_prompt, x_sample, cache_fox_k, cache_fox_v, cache_fox_logf, state_ret, page_table, p_prompt, p_sample, g_ffn1, w1_ffn1, w3_ffn1, w2_ffn1, g_mix, w_in, b_forget, gn_w, gn_b, w_out, g_ffn2, w1_ffn2, w3_ffn2, w2_ffn2, g_ple, w_ple_gate, b_ple_gate, w_ple_proj, g_final):
    depth = w_in.shape[0]
    batch_p, seq_p, d = x_prompt.shape
    batch_s, seq_s, _ = x_sample.shape
    n_pages = page_table.shape[1]
    page = cache_fox_k.shape[2]
    n_pool = cache_fox_k.shape[1]
    n_main = w_in.shape[2] - N_HEADS
    hp = x_prompt.reshape(batch_p * seq_p, d)
    hs = x_sample.reshape(batch_s * seq_s, d)
    pos_p = jnp.tile(jnp.arange(seq_p), batch_p)
    pos_s = jnp.tile(n_pages * page + jnp.arange(seq_s), batch_s)
    per_layer = [[] for _ in range(8)]
    for i in range(depth):
        lw = {
            "g_ffn1": g_ffn1[i], "w1_ffn1": w1_ffn1[i].astype(BF16), "w3_ffn1": w3_ffn1[i].astype(BF16),
            "w2_ffn1": w2_ffn1[i].astype(BF16), "g_mix": g_mix[i],
            "w_main": w_in[i, :, :n_main].astype(BF16),
            "w_ff": jnp.pad(w_in[i, :, n_main:], ((0, 0), (0, LANES - N_HEADS))).astype(BF16),
            "b_ff": jnp.pad(b_forget[i], (0, LANES - N_HEADS)).reshape(1, LANES),
            "gn_w": gn_w[i], "gn_b": gn_b[i], "w_out": w_out[i].astype(BF16),
            "g_ffn2": g_ffn2[i], "w1_ffn2": w1_ffn2[i].astype(BF16), "w3_ffn2": w3_ffn2[i].astype(BF16),
            "w2_ffn2": w2_ffn2[i].astype(BF16), "g_ple": g_ple[i], "w_ple_gate": w_ple_gate[i].astype(BF16),
            "b_ple_gate": b_ple_gate[i], "w_ple_proj": w_ple_proj[i].astype(BF16),
        }
        final = i == depth - 1
        s0_p = jnp.zeros((batch_p, N_HEADS, HEAD_DIM, HEAD_DIM), F32)
        hp, kp, vp, lp, sp = _layer(hp, p_prompt[i].reshape(batch_p * seq_p, -1), pos_p, seq_p, s0_p, None,
                                    lw, g_final, final)
        past = (cache_fox_k[i].transpose(0, 2, 3, 1).reshape(n_pool, D_GROUP, page),
                cache_fox_v[i].transpose(0, 2, 3, 1).reshape(n_pool, D_GROUP, page),
                cache_fox_logf[i].transpose(0, 2, 1), page_table)
        hs, ks, vs, ls, ss = _layer(hs, p_sample[i].reshape(batch_s * seq_s, -1), pos_s, seq_s, state_ret[i],
                                    past, lw, g_final, final)
        for lst, val in zip(per_layer, (
                kp.reshape(batch_p, seq_p, N_HEADS, HEAD_DIM), vp.reshape(batch_p, seq_p, N_HEADS, HEAD_DIM),
                lp.reshape(batch_p, seq_p, N_HEADS), sp,
                ks.reshape(batch_s, seq_s, N_HEADS, HEAD_DIM), vs.reshape(batch_s, seq_s, N_HEADS, HEAD_DIM),
                ls.reshape(batch_s, seq_s, N_HEADS), ss)):
            lst.append(val)
    stacked = [jnp.stack(lst) for lst in per_layer]
    return (hp.reshape(batch_p, seq_p, d), hs.reshape(batch_s, seq_s, d), *stacked)
```

```python
import functools

import numpy as np
import jax
import jax.numpy as jnp
from jax import lax
from jax.experimental import pallas as pl
from jax.experimental.pallas import tpu as pltpu

F32 = jnp.float32
BF16 = jnp.bfloat16

HEAD_DIM = 64
N_HEADS = 8
D_GROUP = N_HEADS * HEAD_DIM
RET_CHUNK = 256
RET_SEQS_PER_STEP = 4
ROPE_BASE = 10000.0
NORM_EPS = 1e-6
GN_EPS = 1e-5
NEG_INF = -1e30
LANES = 128
LOG2E = 1.4426950408889634
BIAS_ROWS = 16
DECODE_SUBGROUPS = 2
FFN_SUBCHUNK = 512
NORM_MARGIN = 1.01
DIAG_SLACK = 2.0 ** -8
SKIP_LOG2 = -75.0
VMEM_LIMIT = 52 * 1024 * 1024

NN_DIMS = (((1,), (0,)), ((), ()))
NT_DIMS = (((1,), (1,)), ((), ()))
TN_DIMS = (((0,), (0,)), ((), ()))


def _pick_tile(n, pref, mult):
    best = None
    for t in range(mult, min(n, pref) + 1, mult):
        if n % t == 0:
            best = t
    return best if best is not None else n


def _params(*sem):
    return pltpu.CompilerParams(dimension_semantics=sem, vmem_limit_bytes=VMEM_LIMIT)


def _rms(x, g):
    r = lax.rsqrt(jnp.mean(x * x, axis=-1, keepdims=True) + NORM_EPS)
    return x * r * g


def _dot(a, b):
    return jnp.dot(a, b, preferred_element_type=F32)


def _split3(x):
    hi = x.astype(BF16).astype(F32)
    r = x - hi
    mid = r.astype(BF16).astype(F32)
    lo = (r - mid).astype(BF16).astype(F32)
    return hi, mid, lo


def _select_sum(mask, x, dims, mask_first=True):
    total = None
    for term in _split3(x):
        term = term.astype(BF16)
        part = lax.dot_general(*((mask, term) if mask_first else (term, mask)), dims, preferred_element_type=F32)
        total = part if total is None else total + part
    return total


def _ffn_kernel(x_ref, g_ref, w1_ref, w3_ref, w2_ref, o_ref):
    x = x_ref[...]
    xn = _rms(x, g_ref[...]).astype(BF16)
    tf = w1_ref.shape[1]
    bounds = [(lo, min(lo + FFN_SUBCHUNK, tf)) for lo in range(0, tf, FFN_SUBCHUNK)]

    def gate_up(lo, hi):
        return _dot(xn, w1_ref[:, lo:hi]), _dot(xn, w3_ref[:, lo:hi])

    nxt = gate_up(*bounds[0])
    total = None
    for n, (lo, hi) in enumerate(bounds):
        h1, h3 = nxt
        if n + 1 < len(bounds):
            nxt = gate_up(*bounds[n + 1])
        a = (h1 * jax.nn.sigmoid(h1) * h3).astype(BF16)
        part = _dot(a, w2_ref[lo:hi, :])
        total = part if total is None else total + part
    o_ref[...] = x + 0.5 * total


def _resident(shape):
    return pl.BlockSpec(shape, lambda i: (0,) * len(shape), pipeline_mode=pl.Buffered(1))


def _ffn(x, g, w1, w3, w2):
    t, d = x.shape
    tm = _pick_tile(t, 512, 8)
    row = lambda i: (i, 0)
    return pl.pallas_call(
        _ffn_kernel,
        grid=(t // tm,),
        in_specs=[pl.BlockSpec((tm, d), row), _resident((1, d)),
                  _resident(w1.shape), _resident(w3.shape), _resident(w2.shape)],
        out_specs=pl.BlockSpec((tm, d), row),
        out_shape=jax.ShapeDtypeStruct((t, d), F32),
        compiler_params=_params("parallel"),
        name="ffn",
    )(x, g, w1, w3, w2)


def _inproj_kernel(head_major, h_ref, g_ref, w_ref, wff_ref, bf_ref, cos_ref, sin_ref,
                   rq_ref, rk_ref, rv_ref, rg_ref, fk_ref, fv_ref, lf_ref, *q_refs):
    tm = h_ref.shape[0]
    u = _rms(h_ref[...], g_ref[...]).astype(BF16)

    def piece(p):
        return _dot(u, w_ref[:, p * D_GROUP:(p + 1) * D_GROUP])

    cos = cos_ref[...]
    sin = sin_ref[...]
    lane = lax.broadcasted_iota(jnp.int32, (tm, LANES), 1)
    first_half = (lane % HEAD_DIM) < (HEAD_DIM // 2)

    def rope(x):
        outs = []
        for cb in range(D_GROUP // LANES):
            xb = x[:, cb * LANES:(cb + 1) * LANES]
            partner = jnp.where(first_half,
                                pltpu.roll(xb, LANES - HEAD_DIM // 2, 1),
                                pltpu.roll(xb, HEAD_DIM // 2, 1))
            outs.append(xb * cos + partner * sin)
        return jnp.concatenate(outs, axis=1)

    scale = HEAD_DIM ** -0.5
    cur, nxt = piece(0), piece(1)
    rq_ref[...] = rope(cur).astype(rq_ref.dtype)
    cur, nxt = nxt, piece(2)
    rk_ref[...] = (rope(cur) * scale).astype(rk_ref.dtype)
    cur, nxt = nxt, piece(3)
    rv_ref[...] = cur.astype(rv_ref.dtype)
    cur, nxt = nxt, piece(4)
    rg_ref[...] = cur
    cur, nxt = nxt, piece(5)
    fq = cur * (scale * LOG2E if head_major else scale)
    if head_major:
        qt_ref, kh_ref, vt_ref, stats_ref = q_refs
        qt_ref[...] = fq.T.astype(BF16).reshape(N_HEADS, HEAD_DIM, tm)
    else:
        (fq_ref,) = q_refs
        fq_ref[...] = fq
    fk, nxt = nxt, piece(6)
    fk_ref[...] = fk
    if head_major:
        for h in range(N_HEADS):
            kh_ref[h] = fk[:, h * HEAD_DIM:(h + 1) * HEAD_DIM].astype(BF16)
        qb = fq.astype(BF16).astype(F32)
        kb = fk.astype(BF16).astype(F32)
        selector = (lax.broadcasted_iota(jnp.int32, (D_GROUP, LANES), 0) // HEAD_DIM
                    == lax.broadcasted_iota(jnp.int32, (D_GROUP, LANES), 1)).astype(BF16)
        per_head = lambda x: _dot(x.astype(BF16), selector)
        kn = jnp.sqrt(jnp.max(per_head(kb * kb), axis=0, keepdims=True)) * NORM_MARGIN
        qn = jnp.sqrt(jnp.max(per_head(qb * qb), axis=0, keepdims=True)) * NORM_MARGIN
        d_min = jnp.min(per_head(qb * kb), axis=0, keepdims=True) - DIAG_SLACK * kn * qn
        sub = lax.broadcasted_iota(jnp.int32, (N_HEADS, LANES), 0)
        stats_ref[0] = jnp.where(sub == 0, d_min, jnp.where(sub == 1, kn, qn))
    fv, ff = nxt, _dot(u, wff_ref[...])
    fv_ref[...] = fv
    if head_major:
        vt_ref[:, 0:HEAD_DIM, :] = fv.T.astype(BF16).reshape(N_HEADS, HEAD_DIM, tm)
        one_row = lax.broadcasted_iota(jnp.int32, (N_HEADS, BIAS_ROWS, tm), 1) == 0
        vt_ref[:, HEAD_DIM:, :] = jnp.where(one_row, 1.0, 0.0).astype(BF16)
    ff = ff + bf_ref[...]
    lf_ref[...] = jnp.minimum(ff, 0.0) - jnp.log1p(jnp.exp(-jnp.abs(ff)))


def _inproj(h, g, w_main, w_ff, b_ff, cos_t, sin_t, head_major):
    t, d = h.shape
    tm = _pick_tile(t, 512, 8)
    act = BF16 if head_major else F32
    row = lambda i: (i, 0)
    fixed = lambda i: (0, 0)
    wide = pl.BlockSpec((tm, D_GROUP), row)
    out_shape = [jax.ShapeDtypeStruct((t, D_GROUP), act)] * 3 + [
        jax.ShapeDtypeStruct((t, D_GROUP), F32)] * 3 + [jax.ShapeDtypeStruct((t, LANES), F32)]
    out_specs = [wide] * 6 + [pl.BlockSpec((tm, LANES), row)]
    if head_major:
        transposed = lambda rows: jax.ShapeDtypeStruct((N_HEADS, rows, t), BF16)
        transposed_spec = lambda rows: pl.BlockSpec((N_HEADS, rows, tm), lambda i: (0, 0, i))
        out_shape += [transposed(HEAD_DIM), jax.ShapeDtypeStruct((N_HEADS, t, HEAD_DIM), BF16),
                      transposed(HEAD_DIM + BIAS_ROWS)]
        out_specs += [transposed_spec(HEAD_DIM), pl.BlockSpec((N_HEADS, tm, HEAD_DIM), lambda i: (0, i, 0)),
                      transposed_spec(HEAD_DIM + BIAS_ROWS)]
        out_shape += [jax.ShapeDtypeStruct((t // tm, N_HEADS, LANES), F32)]
        out_specs += [pl.BlockSpec((1, N_HEADS, LANES), lambda i: (i, 0, 0))]
    else:
        out_shape += [jax.ShapeDtypeStruct((t, D_GROUP), F32)]
        out_specs += [wide]
    return pl.pallas_call(
        functools.partial(_inproj_kernel, head_major),
        grid=(t // tm,),
        in_specs=[
            pl.BlockSpec((tm, d), row),
            pl.BlockSpec((1, d), fixed),
            pl.BlockSpec(w_main.shape, fixed),
            pl.BlockSpec(w_ff.shape, fixed),
            pl.BlockSpec((1, LANES), fixed),
            pl.BlockSpec((tm, LANES), row),
            pl.BlockSpec((tm, LANES), row),
        ],
        out_specs=out_specs,
        out_shape=out_shape,
        compiler_params=_params("parallel"),
        name="inproj",
    )(h, g, w_main, w_ff, b_ff, cos_t, sin_t)


def _cumsum_kernel(seg, x_ref, c_ref, ct_ref, carry_ref):
    tb = x_ref.shape[0]
    row = lax.broadcasted_iota(jnp.int32, (tb, tb), 0)
    col = lax.broadcasted_iota(jnp.int32, (tb, tb), 1)
    keep = col <= row
    carried = seg > tb
    if not carried:
        keep = keep & ((row // seg) == (col // seg))
    tri = jnp.where(keep, 1.0, 0.0).astype(BF16)
    c = _select_sum(tri, x_ref[...], NN_DIMS)
    if carried:
        @pl.when(pl.program_id(0) == 0)
        def _():
            carry_ref[...] = jnp.zeros_like(carry_ref)

        c = c + carry_ref[...]
        carry_ref[...] = c[tb - 1:tb, :]
    c_ref[...] = c
    ct_ref[...] = c.T


def _cumsum(x, seg):
    t = x.shape[0]
    tb = _pick_tile(t, 512, LANES)
    assert seg % tb == 0 or tb % seg == 0
    return pl.pallas_call(
        functools.partial(_cumsum_kernel, seg),
        grid=(t // tb,),
        in_specs=[pl.BlockSpec((tb, LANES), lambda i: (i, 0))],
        out_specs=[pl.BlockSpec((tb, LANES), lambda i: (i, 0)),
                   pl.BlockSpec((LANES, tb), lambda i: (0, i))],
        out_shape=[jax.ShapeDtypeStruct((t, LANES), F32), jax.ShapeDtypeStruct((LANES, t), F32)],
        scratch_shapes=[pltpu.VMEM((1, LANES), F32)],
        compiler_params=_params("arbitrary"),
        name="forget_cumsum",
    )(x)


def _cumsum_pack_kernel(x_ref, qt_ref, kh_ref, stats_ref, qa_ref, ka_ref, skip_ref, carry_ref, kn_ref, cmin_ref):
    i = pl.program_id(0)
    tb = x_ref.shape[0]
    row = lax.broadcasted_iota(jnp.int32, (tb, tb), 0)
    col = lax.broadcasted_iota(jnp.int32, (tb, tb), 1)
    tri = jnp.where(col <= row, 1.0, 0.0).astype(BF16)

    @pl.when(i == 0)
    def _():
        carry_ref[...] = jnp.zeros_like(carry_ref)
        kn_ref[...] = jnp.zeros_like(kn_ref)
        cmin_ref[...] = jnp.zeros_like(cmin_ref)

    c_nat = _select_sum(tri, x_ref[...], NN_DIMS) + carry_ref[...]
    carry_ref[...] = c_nat[tb - 1:tb, :]
    c = c_nat * LOG2E
    ct = c.T
    sub = lax.broadcasted_iota(jnp.int32, (BIAS_ROWS, tb), 0)
    lane = lax.broadcasted_iota(jnp.int32, (tb, HEAD_DIM), 1)
    for h in range(N_HEADS):
        hi, mid, lo = _split3(ct[h:h + 1, :])
        extra = jnp.where(sub == 0, hi, jnp.where(sub == 1, mid, jnp.where(sub == 2, lo,
                                                                          jnp.where(sub < 6, 1.0, 0.0))))
        qa_ref[h, 0:HEAD_DIM, :] = qt_ref[h]
        qa_ref[h, HEAD_DIM:HEAD_DIM + BIAS_ROWS, :] = extra.astype(BF16)
        qa_ref[h, HEAD_DIM + BIAS_ROWS:, :] = jnp.zeros((LANES - HEAD_DIM - BIAS_ROWS, tb), BF16)
        hi, mid, lo = _split3(c[:, h:h + 1])
        extra = jnp.where(lane < 3, 1.0, jnp.where(lane == 3, -hi, jnp.where(lane == 4, -mid,
                                                                             jnp.where(lane == 5, -lo, 0.0))))
        ka_ref[h, :, 0:HEAD_DIM] = kh_ref[h]
        ka_ref[h, :, HEAD_DIM:] = extra.astype(BF16)
    stats = stats_ref[0]
    d_min, kn, qn = stats[0:1, :], stats[1:2, :], stats[2:3, :]
    c_max = jnp.max(c, axis=0, keepdims=True)
    c_min = jnp.min(c, axis=0, keepdims=True)
    kn_ref[pl.ds(i, 1), :] = kn
    cmin_ref[pl.ds(i, 1), :] = c_min
    bound = kn_ref[...] * qn - d_min + c_max - cmin_ref[...]
    skip_ref[0] = jnp.where(bound <= SKIP_LOG2, 1, 0).astype(jnp.int32)


def _cumsum_pack(x, qt, kh, stats):
    t = x.shape[0]
    nb = stats.shape[0]
    tb = t // nb
    nb_pad = -(-nb // 8) * 8
    return pl.pallas_call(
        _cumsum_pack_kernel,
        grid=(nb,),
        in_specs=[pl.BlockSpec((tb, LANES), lambda i: (i, 0)),
                  pl.BlockSpec((N_HEADS, HEAD_DIM, tb), lambda i: (0, 0, i)),
                  pl.BlockSpec((N_HEADS, tb, HEAD_DIM), lambda i: (0, i, 0)),
                  pl.BlockSpec((1, N_HEADS, LANES), lambda i: (i, 0, 0))],
        out_specs=[pl.BlockSpec((N_HEADS, LANES, tb), lambda i: (0, 0, i)),
                   pl.BlockSpec((N_HEADS, tb, LANES), lambda i: (0, i, 0)),
                   pl.BlockSpec((1, nb_pad, LANES), lambda i: (i, 0, 0))],
        out_shape=[jax.ShapeDtypeStruct((N_HEADS, LANES, t), BF16),
                   jax.ShapeDtypeStruct((N_HEADS, t, LANES), BF16),
                   jax.ShapeDtypeStruct((nb, nb_pad, LANES), jnp.int32)],
        scratch_shapes=[pltpu.VMEM((1, LANES), F32), pltpu.VMEM((nb_pad, LANES), F32),
                        pltpu.VMEM((nb_pad, LANES), F32)],
        compiler_params=_params("arbitrary"),
        name="forget_cumsum_pack",
    )(x, qt, kh, stats)


def _ret_kernel(q_ref, k_ref, v_ref, g_ref, s0_ref, intra_ref, qdec_ref, kdec_ref, cdec_ref,
                gnw_ref, gnb_ref, o_ref, sout_ref, s_ref):
    c = pl.program_id(1)
    n_seq = s0_ref.shape[0]
    chunk = q_ref.shape[0] // n_seq

    @pl.when(c == 0)
    def _():
        s_ref[...] = s0_ref[...]

    mm = BF16
    q = q_ref[...].astype(F32)
    k = k_ref[...].astype(F32)
    v = v_ref[...].astype(F32)
    units = [(s, h) for s in range(n_seq) for h in range(N_HEADS)]

    def part(x, s, h, scale_ref=None):
        x = x[s * chunk:(s + 1) * chunk, h * HEAD_DIM:(h + 1) * HEAD_DIM]
        if scale_ref is not None:
            x = x * scale_ref[:, h * HEAD_DIM:(h + 1) * HEAD_DIM]
        return x.astype(mm)

    atts = [lax.dot_general(part(q, s, h), part(k, s, h), NT_DIMS, preferred_element_type=F32) for s, h in units]
    inters = [_dot(part(q, s, h, qdec_ref), s_ref[s, h].astype(mm)) for s, h in units]
    for s, h in units:
        s_ref[s, h] = s_ref[s, h] * cdec_ref[h] + lax.dot_general(part(k, s, h, kdec_ref), part(v, s, h), TN_DIMS,
                                                                  preferred_element_type=F32)
    rows = []
    for s in range(n_seq):
        outs = []
        for h in range(N_HEADS):
            o = _dot((atts[s * N_HEADS + h] * intra_ref[h]).astype(mm), part(v, s, h)) + inters[s * N_HEADS + h]
            mu = jnp.mean(o, axis=-1, keepdims=True)
            dev = o - mu
            var = jnp.mean(dev * dev, axis=-1, keepdims=True)
            outs.append(dev * lax.rsqrt(var + GN_EPS))
        rows.append(jnp.concatenate(outs, axis=1))
    on = jnp.concatenate(rows, axis=0) if n_seq > 1 else rows[0]
    gate = g_ref[...]
    o_ref[...] = ((on * gnw_ref[...] + gnb_ref[...]) * (gate * jax.nn.sigmoid(gate))).astype(o_ref.dtype)

    @pl.when(c == pl.num_programs(1) - 1)
    def _():
        sout_ref[...] = s_ref[...]


def _ret_tables(chunk):
    lg = jnp.log1p(-(2.0 ** (-5.0 - jnp.arange(N_HEADS, dtype=F32))))
    i = jnp.arange(chunk, dtype=F32)
    diff = i[:, None] - i[None, :]
    intra = jnp.where(diff[None] >= 0, jnp.exp(jnp.maximum(diff, 0.0)[None] * lg[:, None, None]), 0.0)
    q_dec = jnp.exp((i[:, None] + 1.0) * lg[None, :])
    k_dec = jnp.exp((chunk - 1.0 - i)[:, None] * lg[None, :])
    c_dec = jnp.exp(chunk * lg)
    expand = lambda a: jnp.repeat(a, HEAD_DIM, axis=1)
    return intra, expand(q_dec), expand(k_dec), jnp.broadcast_to(c_dec[:, None, None], (N_HEADS, 1, HEAD_DIM))


def _retention(q, k, v, g, s0, gn_w, gn_b, seq, out_dtype):
    t = q.shape[0]
    batch = t // seq
    chunk = min(RET_CHUNK, seq)
    n = seq // chunk
    intra, q_dec, k_dec, c_dec = _ret_tables(chunk)
    n_seq = _pick_tile(batch, RET_SEQS_PER_STEP, 1) if n == 1 else 1
    blk = pl.BlockSpec((n_seq * chunk, D_GROUP), lambda b, c: (b * n + c, 0))
    fixed2 = lambda b, c: (0, 0)
    fixed3 = lambda b, c: (0, 0, 0)
    state = pl.BlockSpec((n_seq, N_HEADS, HEAD_DIM, HEAD_DIM), lambda b, c: (b, 0, 0, 0))
    return pl.pallas_call(
        _ret_kernel,
        grid=(batch // n_seq, n),
        in_specs=[blk, blk, blk, blk, state,
                  pl.BlockSpec(intra.shape, fixed3),
                  pl.BlockSpec(q_dec.shape, fixed2),
                  pl.BlockSpec(k_dec.shape, fixed2),
                  pl.BlockSpec(c_dec.shape, fixed3),
                  pl.BlockSpec((1, D_GROUP), fixed2),
                  pl.BlockSpec((1, D_GROUP), fixed2)],
        out_specs=[blk, state],
        out_shape=[jax.ShapeDtypeStruct((t, D_GROUP), out_dtype),
                   jax.ShapeDtypeStruct((batch, N_HEADS, HEAD_DIM, HEAD_DIM), F32)],
        scratch_shapes=[pltpu.VMEM((n_seq, N_HEADS, HEAD_DIM, HEAD_DIM), F32)],
        compiler_params=_params("parallel", "arbitrary"),
        name="retention",
    )(q, k, v, g, s0, intra, q_dec, k_dec, c_dec, gn_w, gn_b)


HEAD_GROUP = 2


def _fox_prompt_kernel(n_kblocks, qi_ref, kj_ref, skip_ref, qa_ref, ka_ref, vt_ref, o_ref, m_ref, acc_ref):
    step = pl.program_id(0)
    i = qi_ref[step]
    j = kj_ref[step]
    tk = ka_ref.shape[1]
    tq = qa_ref.shape[2]
    n_groups = N_HEADS // HEAD_GROUP

    @pl.when(j == 0)
    def _():
        m_ref[...] = jnp.full_like(m_ref, NEG_INF)
        acc_ref[...] = jnp.zeros_like(acc_ref)

    def scores(h):
        return _dot(ka_ref[h], qa_ref[h])

    def update(heads, causal):
        s_next = scores(heads[0])
        for n, h in enumerate(heads):
            s = s_next
            if n + 1 < len(heads):
                s_next = scores(heads[n + 1])
            if causal is not None:
                s = jnp.where(causal, s, NEG_INF)
            m_prev = m_ref[h]
            m_new = jnp.maximum(m_prev, jnp.max(s, axis=0, keepdims=True))
            p = jnp.exp2(s - m_new).astype(BF16)
            acc_ref[h] = jnp.exp2(m_prev - m_new) * acc_ref[h] + _dot(vt_ref[h], p)
            m_ref[h] = m_new

    @pl.when(j < i)
    def _():
        flags = (i * n_kblocks + j) * N_HEADS
        for grp in range(n_groups):
            heads = list(range(grp * HEAD_GROUP, (grp + 1) * HEAD_GROUP))
            skippable = functools.reduce(jnp.minimum, [skip_ref[flags + h] for h in heads])

            @pl.when(skippable == 0)
            def _():
                update(heads, None)

    @pl.when(j == i)
    def _():
        causal = (lax.broadcasted_iota(jnp.int32, (tk, tq), 0)
                  <= lax.broadcasted_iota(jnp.int32, (tk, tq), 1))
        update(list(range(N_HEADS)), causal)
        for h in range(N_HEADS):
            acc = acc_ref[h]
            o_ref[h] = (acc[:HEAD_DIM] / acc[HEAD_DIM:HEAD_DIM + 1]).astype(o_ref.dtype)


def _fox_prompt(qa, ka, vt, skip):
    t = ka.shape[1]
    nq = skip.shape[0]
    tq = t // nq
    qi = np.array([i for i in range(nq) for _ in range(i + 1)], np.int32)
    kj = np.array([j for i in range(nq) for j in range(i + 1)], np.int32)
    v_rows = vt.shape[1]
    grid_spec = pltpu.PrefetchScalarGridSpec(
        num_scalar_prefetch=3,
        grid=(len(qi),),
        in_specs=[pl.BlockSpec((N_HEADS, LANES, tq), lambda s, qi, kj, sk: (0, 0, qi[s])),
                  pl.BlockSpec((N_HEADS, tq, LANES), lambda s, qi, kj, sk: (0, kj[s], 0)),
                  pl.BlockSpec((N_HEADS, v_rows, tq), lambda s, qi, kj, sk: (0, 0, kj[s]))],
        out_specs=pl.BlockSpec((N_HEADS, HEAD_DIM, tq), lambda s, qi, kj, sk: (0, 0, qi[s])),
        scratch_shapes=[pltpu.VMEM((N_HEADS, 1, tq), F32), pltpu.VMEM((N_HEADS, v_rows, tq), F32)],
    )
    return pl.pallas_call(
        functools.partial(_fox_prompt_kernel, skip.shape[1]),
        grid_spec=grid_spec,
        out_shape=jax.ShapeDtypeStruct((N_HEADS, HEAD_DIM, t), BF16),
        compiler_params=_params("arbitrary"),
        name="fox_prompt",
    )(jnp.asarray(qi), jnp.asarray(kj), skip[:, :, :N_HEADS].reshape(-1), qa, ka, vt)


def _fox_decode_kernel(n_group, n_pages, pt_ref, q_ref, kn_ref, vn_ref, cn_ref, lt_hbm, kt_hbm, vt_hbm,
                       o_ref, lt_buf, kt_buf, vt_buf, sem, qbd_ref, m_ref, l_ref, acc_ref, carry_ref):
    b = pl.program_id(0)
    g = pl.program_id(1)
    n_steps = pl.num_programs(1)
    step = b * n_steps + g
    slot = step % 2
    n_q = q_ref.shape[0]
    rows = n_q * N_HEADS
    page = kt_buf.shape[3]
    cn = cn_ref[0]

    def page_copies(of_step, into_slot):
        seq = of_step // n_steps
        first = seq * n_pages + (n_steps - 1 - of_step % n_steps) * n_group
        copies = []
        for r in range(n_group):
            pid = pt_ref[first + r]
            copies.append(pltpu.make_async_copy(lt_hbm.at[pid], lt_buf.at[into_slot, r], sem.at[0, into_slot]))
            copies.append(pltpu.make_async_copy(kt_hbm.at[pid], kt_buf.at[into_slot, r], sem.at[1, into_slot]))
            copies.append(pltpu.make_async_copy(vt_hbm.at[pid], vt_buf.at[into_slot, r], sem.at[2, into_slot]))
        return copies

    def start_all(copies):
        for n, c in enumerate(copies):
            c.start(priority=0 if n % 3 == 1 else 1)

    @pl.when(step == 0)
    def _():
        start_all(page_copies(step, slot))

    @pl.when(step + 1 < pl.num_programs(0) * n_steps)
    def _():
        start_all(page_copies(step + 1, 1 - slot))

    for c in page_copies(step, slot):
        c.wait()
    lt_pages = [lt_buf[slot, r] for r in range(n_group)]

    def attend(s, v, v_dims):
        m_prev = m_ref[...]
        m_new = jnp.maximum(m_prev, jnp.max(s, axis=1, keepdims=True))
        alpha = jnp.exp(m_prev - m_new)
        p = jnp.exp(s - m_new)
        l_ref[...] = alpha * l_ref[...] + jnp.sum(p, axis=1, keepdims=True)
        acc_ref[...] = alpha * acc_ref[...] + lax.dot_general(p.astype(BF16), v, v_dims,
                                                              preferred_element_type=F32)
        m_ref[...] = m_new

    @pl.when(g == 0)
    def _():
        q = q_ref[...]
        rep = jnp.concatenate([jnp.broadcast_to(q[t:t + 1, :], (N_HEADS, D_GROUP)) for t in range(n_q)], axis=0)
        r_head = lax.broadcasted_iota(jnp.int32, (rows, D_GROUP), 0) % N_HEADS
        l_head = lax.broadcasted_iota(jnp.int32, (rows, D_GROUP), 1) // HEAD_DIM
        qbd_ref[...] = jnp.where(r_head == l_head, rep, 0.0).astype(BF16)
        m_ref[...] = jnp.full_like(m_ref, NEG_INF)
        l_ref[...] = jnp.zeros_like(l_ref)
        acc_ref[...] = jnp.zeros_like(acc_ref)
        carry_ref[...] = jnp.zeros_like(carry_ref)
        pad = jnp.zeros((page - n_q, D_GROUP), F32)
        k_new = jnp.concatenate([kn_ref[...], pad], axis=0).astype(BF16)
        v_new = jnp.concatenate([vn_ref[...], pad], axis=0).astype(BF16)
        s = lax.dot_general(qbd_ref[...], k_new, NT_DIMS, preferred_element_type=F32)
        key_pos = lax.broadcasted_iota(jnp.int32, (N_HEADS, page), 1)
        parts = []
        for t in range(n_q):
            s_t = s[t * N_HEADS:(t + 1) * N_HEADS, :] + cn[:, t:t + 1] - cn
            parts.append(jnp.where(key_pos <= t, s_t, NEG_INF))
        attend(jnp.concatenate(parts, axis=0), v_new, NN_DIMS)

    x = jnp.concatenate(lt_pages, axis=0)
    later = (lax.broadcasted_iota(jnp.int32, (page, page), 0)
             > lax.broadcasted_iota(jnp.int32, (page, page), 1))
    d_loc = _select_sum(jnp.where(later, 1.0, 0.0).astype(BF16), x, NN_DIMS, mask_first=False)
    tot = jnp.sum(x, axis=1, keepdims=True)
    run = carry_ref[...]
    d_pages = [None] * n_group
    for r in reversed(range(n_group)):
        d_pages[r] = d_loc[r * N_HEADS:(r + 1) * N_HEADS, :] + run
        run = run + tot[r * N_HEADS:(r + 1) * N_HEADS, :]
    carry_ref[...] = run
    n_sub = DECODE_SUBGROUPS if n_group % DECODE_SUBGROUPS == 0 else 1
    per_sub = n_group // n_sub
    subs = [list(range(a * per_sub, (a + 1) * per_sub)) for a in range(n_sub)]
    qbd = qbd_ref[...]

    def sub_scores(pages):
        return _dot(qbd, jnp.concatenate([kt_buf[slot, r].astype(BF16) for r in pages], axis=1))

    s_next = sub_scores(subs[0])
    for n, pages in enumerate(subs):
        s = s_next
        if n + 1 < n_sub:
            s_next = sub_scores(subs[n + 1])
        d_sub = jnp.concatenate([d_pages[r] for r in pages], axis=1)
        parts = [s[t * N_HEADS:(t + 1) * N_HEADS, :] + d_sub + cn[:, t:t + 1] for t in range(n_q)]
        vt_sub = jnp.concatenate([vt_buf[slot, r].astype(BF16) for r in pages], axis=1)
        attend(jnp.concatenate(parts, axis=0), vt_sub, NT_DIMS)

    @pl.when(g == pl.num_programs(1) - 1)
    def _():
        o = acc_ref[...] / l_ref[...]
        r_head = lax.broadcasted_iota(jnp.int32, (rows, D_GROUP), 0) % N_HEADS
        l_head = lax.broadcasted_iota(jnp.int32, (rows, D_GROUP), 1) // HEAD_DIM
        o = jnp.where(r_head == l_head, o, 0.0)
        o_ref[...] = jnp.concatenate(
            [jnp.sum(o[t * N_HEADS:(t + 1) * N_HEADS, :], axis=0, keepdims=True) for t in range(n_q)], axis=0)


def _fox_decode(fq, fk, fv, cn, cache_kt, cache_vt, cache_lt, page_table, n_q):
    batch, n_pages = page_table.shape
    n_pool, _, page = cache_kt.shape
    n_group = _pick_tile(n_pages, 16, 1)
    n_steps = n_pages // n_group
    rows = n_q * N_HEADS

    per_seq = pl.BlockSpec((n_q, D_GROUP), lambda b, g, pt: (b, 0))
    in_hbm = pl.BlockSpec(memory_space=pl.ANY)
    grid_spec = pltpu.PrefetchScalarGridSpec(
        num_scalar_prefetch=1,
        grid=(batch, n_steps),
        in_specs=[per_seq, per_seq, per_seq, pl.BlockSpec((1, N_HEADS, LANES), lambda b, g, pt: (b, 0, 0)),
                  in_hbm, in_hbm, in_hbm],
        out_specs=per_seq,
        scratch_shapes=[pltpu.VMEM((2, n_group, N_HEADS, page), F32),
                        pltpu.VMEM((2, n_group, D_GROUP, page), F32),
                        pltpu.VMEM((2, n_group, D_GROUP, page), F32),
                        pltpu.SemaphoreType.DMA((3, 2)),
                        pltpu.VMEM((rows, D_GROUP), BF16), pltpu.VMEM((rows, 1), F32),
                        pltpu.VMEM((rows, 1), F32), pltpu.VMEM((rows, D_GROUP), F32),
                        pltpu.VMEM((N_HEADS, 1), F32)],
    )
    return pl.pallas_call(
        functools.partial(_fox_decode_kernel, n_group, n_pages),
        grid_spec=grid_spec,
        out_shape=jax.ShapeDtypeStruct((batch * n_q, D_GROUP), F32),
        compiler_params=_params("arbitrary", "arbitrary"),
        name="fox_decode",
    )(page_table.reshape(-1), fq, fk, fv, cn, cache_lt, cache_kt, cache_vt)


def _outproj_kernel(fox_transposed, h_ref, a_ref, b_ref, w_ref, o_ref):
    half = a_ref.shape[1]
    b = b_ref[...].astype(F32).T if fox_transposed else b_ref[...]
    mix = _dot(a_ref[...].astype(BF16), w_ref[:half, :]) + _dot(b.astype(BF16), w_ref[half:, :])
    o_ref[...] = h_ref[...] + mix


def _outproj(h, o_ret, o_fox, w_out, fox_transposed):
    t, d = h.shape
    tm = _pick_tile(t, 1024, LANES if fox_transposed else 8)
    row = lambda i: (i, 0)
    fox_spec = pl.BlockSpec((D_GROUP, tm), lambda i: (0, i)) if fox_transposed else pl.BlockSpec((tm, D_GROUP), row)
    return pl.pallas_call(
        functools.partial(_outproj_kernel, fox_transposed),
        grid=(t // tm,),
        in_specs=[pl.BlockSpec((tm, d), row), pl.BlockSpec((tm, D_GROUP), row),
                  fox_spec, pl.BlockSpec(w_out.shape, lambda i: (0, 0))],
        out_specs=pl.BlockSpec((tm, d), row),
        out_shape=jax.ShapeDtypeStruct((t, d), F32),
        compiler_params=_params("parallel"),
        name="outproj",
    )(h, o_ret, o_fox, w_out)


def _ple_kernel(final, h_ref, p_ref, g_ref, wg_ref, bg_ref, wp_ref, gf_ref, o_ref):
    h = h_ref[...]
    gate = jax.nn.sigmoid(_dot(_rms(h, g_ref[...]).astype(BF16), wg_ref[...]) + bg_ref[...])
    h = h + gate * _dot(p_ref[...].astype(BF16), wp_ref[...])
    o_ref[...] = _rms(h, gf_ref[...]) if final else h


def _ple(h, p, g, w_gate, b_gate, w_proj, g_final, final):
    t, d = h.shape
    tm = _pick_tile(t, 1024, 8)
    row = lambda i: (i, 0)
    fixed = lambda i: (0, 0)
    return pl.pallas_call(
        functools.partial(_ple_kernel, final),
        grid=(t // tm,),
        in_specs=[pl.BlockSpec((tm, d), row), pl.BlockSpec((tm, p.shape[1]), row),
                  pl.BlockSpec((1, d), fixed), pl.BlockSpec(w_gate.shape, fixed),
                  pl.BlockSpec((1, d), fixed), pl.BlockSpec(w_proj.shape, fixed),
                  pl.BlockSpec((1, d), fixed)],
        out_specs=pl.BlockSpec((tm, d), row),
        out_shape=jax.ShapeDtypeStruct((t, d), F32),
        compiler_params=_params("parallel"),
        name="ple_gate",
    )(h, p, g, w_gate, b_gate, w_proj, g_final)


def _rope_tables(pos):
    half = HEAD_DIM // 2
    inv = ROPE_BASE ** (-jnp.arange(half, dtype=F32) / half)
    ang = pos.astype(F32)[:, None] * inv[None, :]
    cos = jnp.cos(ang)
    sin = jnp.sin(ang)
    return jnp.tile(cos, (1, LANES // half)), jnp.tile(jnp.concatenate([-sin, sin], axis=1), (1, LANES // HEAD_DIM))


def _row(v):
    return v.reshape(1, -1)


def _layer(x, p, pos, seq, s0, past, lw, g_final, final):
    t = x.shape[0]
    batch = t // seq
    prompt = past is None
    cos_t, sin_t = _rope_tables(pos)
    h = _ffn(x, _row(lw["g_ffn1"]), lw["w1_ffn1"], lw["w3_ffn1"], lw["w2_ffn1"])
    outs = _inproj(h, _row(lw["g_mix"]), lw["w_main"], lw["w_ff"], lw["b_ff"], cos_t, sin_t, prompt)
    rq, rk, rv, rg, fk, fv, lf = outs[:7]
    o_ret, s_ret = _retention(rq, rk, rv, rg, s0, _row(lw["gn_w"]), _row(lw["gn_b"]), seq,
                              BF16 if prompt else F32)
    if prompt:
        assert batch == 1, "the prompt path handles one sequence"
        qt, kh, vt, stats = outs[7:]
        qa, ka, skip = _cumsum_pack(lf, qt, kh, stats)
        o_fox = _fox_prompt(qa, ka, vt, skip).reshape(D_GROUP, t)
    else:
        (fq,) = outs[7:]
        _, ct = _cumsum(lf, seq)
        cache_kt, cache_vt, cache_lt, page_table = past
        cn = ct[:N_HEADS].reshape(N_HEADS, batch, seq).transpose(1, 0, 2)
        cn = jnp.pad(cn, ((0, 0), (0, 0), (0, LANES - seq)))
        o_fox = _fox_decode(fq, fk, fv, cn, cache_kt, cache_vt, cache_lt, page_table, seq)
    h = _outproj(h, o_ret, o_fox, lw["w_out"], prompt)
    h = _ffn(h, _row(lw["g_ffn2"]), lw["w1_ffn2"], lw["w3_ffn2"], lw["w2_ffn2"])
    h = _ple(h, p, _row(lw["g_ple"]), lw["w_ple_gate"], _row(lw["b_ple_gate"]), lw["w_ple_proj"],
             _row(g_final), final)
    return h, fk, fv, lf[:, :N_HEADS], s_ret


def kernel(x_prompt, x_sample, cache_fox_k, cache_fox_v, cache_fox_logf, state_ret, page_table, p_prompt, p_sample, g_ffn1, w1_ffn1, w3_ffn1, w2_ffn1, g_mix, w_in, b_forget, gn_w, gn_b, w_out, g_ffn2, w1_ffn2, w3_ffn2, w2_ffn2, g_ple, w_ple_gate, b_ple_gate, w_ple_proj, g_final):
    depth = w_in.shape[0]
    batch_p, seq_p, d = x_prompt.shape
    batch_s, seq_s, _ = x_sample.shape
    n_pages = page_table.shape[1]
    page = cache_fox_k.shape[2]
    n_pool = cache_fox_k.shape[1]
    n_main = w_in.shape[2] - N_HEADS
    hp = x_prompt.reshape(batch_p * seq_p, d)
    hs = x_sample.reshape(batch_s * seq_s, d)
    pos_p = jnp.tile(jnp.arange(seq_p), batch_p)
    pos_s = jnp.tile(n_pages * page + jnp.arange(seq_s), batch_s)
    per_layer = [[] for _ in range(8)]
    for i in range(depth):
        lw = {
            "g_ffn1": g_ffn1[i], "w1_ffn1": w1_ffn1[i].astype(BF16), "w3_ffn1": w3_ffn1[i].astype(BF16),
            "w2_ffn1": w2_ffn1[i].astype(BF16), "g_mix": g_mix[i],
            "w_main": w_in[i, :, :n_main].astype(BF16),
            "w_ff": jnp.pad(w_in[i, :, n_main:], ((0, 0), (0, LANES - N_HEADS))).astype(BF16),
            "b_ff": jnp.pad(b_forget[i], (0, LANES - N_HEADS)).reshape(1, LANES),
            "gn_w": gn_w[i], "gn_b": gn_b[i], "w_out": w_out[i].astype(BF16),
            "g_ffn2": g_ffn2[i], "w1_ffn2": w1_ffn2[i].astype(BF16), "w3_ffn2": w3_ffn2[i].astype(BF16),
            "w2_ffn2": w2_ffn2[i].astype(BF16), "g_ple": g_ple[i], "w_ple_gate": w_ple_gate[i].astype(BF16),
            "b_ple_gate": b_ple_gate[i], "w_ple_proj": w_ple_proj[i].astype(BF16),
        }
        final = i == depth - 1
        s0_p = jnp.zeros((batch_p, N_HEADS, HEAD_DIM, HEAD_DIM), F32)
        hp, kp, vp, lp, sp = _layer(hp, p_prompt[i].reshape(batch_p * seq_p, -1), pos_p, seq_p, s0_p, None,
                                    lw, g_final, final)
        past = (cache_fox_k[i].transpose(0, 2, 3, 1).reshape(n_pool, D_GROUP, page),
                cache_fox_v[i].transpose(0, 2, 3, 1).reshape(n_pool, D_GROUP, page),
                cache_fox_logf[i].transpose(0, 2, 1), page_table)
        hs, ks, vs, ls, ss = _layer(hs, p_sample[i].reshape(batch_s * seq_s, -1), pos_s, seq_s, state_ret[i],
                                    past, lw, g_final, final)
        for lst, val in zip(per_layer, (
                kp.reshape(batch_p, seq_p, N_HEADS, HEAD_DIM), vp.reshape(batch_p, seq_p, N_HEADS, HEAD_DIM),
                lp.reshape(batch_p, seq_p, N_HEADS), sp,
                ks.reshape(batch_s, seq_s, N_HEADS, HEAD_DIM), vs.reshape(batch_s, seq_s, N_HEADS, HEAD_DIM),
                ls.reshape(batch_s, seq_s, N_HEADS), ss)):
            lst.append(val)
    stacked = [jnp.stack(lst) for lst in per_layer]
    return (hp.reshape(batch_p, seq_p, d), hs.reshape(batch_s, seq_s, d), *stacked)
```
